```python
import jax, jax.numpy as jnp
from jax import lax
import numpy as np

D_MODEL = 1024
BATCH = 4
SEQ = 8192
DEPTH = 2

HEAD_DIM = 64
POOL_WINDOWS = (2, 4, 8, 16)
POOL_WIDTH = D_MODEL // 4
POOL_GROUP = POOL_WIDTH // len(POOL_WINDOWS)
DIL_PAIRS = ((128, 1), (512, 4), (2048, 16))
DIL_HEADS_PER_GROUP = D_MODEL // 512
DIL_HEADS = DIL_HEADS_PER_GROUP * len(DIL_PAIRS)
DIL_WIDTH = DIL_HEADS * HEAD_DIM
DIL_BLOCK = 64
WIN_Q_HEADS = DIL_HEADS
WIN_KV_HEADS = WIN_Q_HEADS // 3
WIN_GROUP = WIN_Q_HEADS // WIN_KV_HEADS
WIN_RADIUS = 128
WIN_BLOCK = 128
WIN_Q_WIDTH = WIN_Q_HEADS * HEAD_DIM
WIN_KV_WIDTH = WIN_KV_HEADS * HEAD_DIM
MIX_WIDTH = POOL_WIDTH + DIL_WIDTH + WIN_Q_WIDTH
IN_WIDTH = POOL_WIDTH + 3 * DIL_WIDTH + WIN_Q_WIDTH + 2 * WIN_KV_WIDTH
N_ATTN_HEADS = DIL_HEADS + WIN_Q_HEADS
D_FF = -(-8 * D_MODEL // (3 * 256)) * 256
N_MOD = 6
EPS = 1e-6
NEG = -1e30

kernel_name = "hybrid_pool_dilated_swa_encoder"


def rmsnorm(x, g):
    xf = x.astype(jnp.float32)
    y = xf * lax.rsqrt(jnp.mean(xf * xf, axis=-1, keepdims=True) + EPS)
    return (y * g.astype(jnp.float32)).astype(x.dtype)


def alibi_slopes():
    i = jnp.arange(1, N_ATTN_HEADS + 1, dtype=jnp.float32)
    return jnp.exp2(-8.0 * i / N_ATTN_HEADS)


def banded_attention(q, k, v, radius, block, slopes, dist_scale, sink):
    N, L, Hk, G, D = q.shape
    nb = -(-L // block)
    Lp = nb * block
    W = block + 2 * radius
    q = jnp.pad(q, ((0, 0), (0, Lp - L), (0, 0), (0, 0), (0, 0)))
    kpad = ((0, 0), (radius, Lp - L + radius), (0, 0), (0, 0))
    k = jnp.pad(k, kpad)
    v = jnp.pad(v, kpad)
    idx = (jnp.arange(nb) * block)[:, None] + jnp.arange(W)[None, :]
    kb = k[:, idx]
    vb = v[:, idx]
    qb = q.reshape(N, nb, block, Hk, G, D)
    s = jnp.einsum('nbqhgd,nbkhd->nbhgqk', qb, kb).astype(jnp.float32) * (D ** -0.5)
    rel = jnp.arange(W)[None, :] - radius - jnp.arange(block)[:, None]
    dist = jnp.abs(rel)
    key_pos = idx - radius
    valid = (dist <= radius)[None] & ((key_pos >= 0) & (key_pos < L))[:, None, :]
    bias = -(slopes.astype(jnp.float32) * dist_scale)[:, :, None, None] * dist.astype(jnp.float32)
    s = jnp.where(valid[None, :, None, None], s + bias[None, None], NEG)
    m = jnp.max(s, axis=-1)
    if sink is not None:
        sk = sink.astype(jnp.float32)[None, None, :, :, None]
        m = jnp.maximum(m, sk)
    e = jnp.exp(s - m[..., None])
    denom = jnp.sum(e, axis=-1)
    if sink is not None:
        denom = denom + jnp.exp(sk - m)
    p = e / denom[..., None]
    lse = m + jnp.log(denom)
    out = jnp.einsum('nbhgqk,nbkhd->nbqhgd', p.astype(v.dtype), vb)
    out = out.reshape(N, Lp, Hk, G, D)[:, :L]
    lse = lse.transpose(0, 1, 4, 2, 3).reshape(N, Lp, Hk, G)[:, :L]
    return out, lse


def multiscale_pool(u, w_pool, scale):
    B, S, C = u.shape
    uf = u.astype(jnp.float32)
    cs = jnp.pad(jnp.cumsum(uf, axis=1), ((0, 0), (1, 0), (0, 0)))
    t = jnp.arange(S)
    outs = []
    for g, w in enumerate(POOL_WINDOWS):
        r = w // 2
        lo = jnp.clip(t - r, 0, S)
        hi = jnp.clip(t + r + 1, 0, S)
        sl = slice(g * POOL_GROUP, (g + 1) * POOL_GROUP)
        csg = cs[..., sl]
        mean = (csg[:, hi] - csg[:, lo]) / (hi - lo).astype(jnp.float32)[None, :, None]
        outs.append(mean - uf[..., sl])
    pooled = jnp.stack(outs, axis=2).astype(u.dtype)
    y = jnp.einsum('bsgc,gcd->bsgd', pooled, w_pool).reshape(B, S, C)
    return y * scale


def dilated_attention(q, k, v, slopes):
    B, S, H, D = q.shape
    hg = DIL_HEADS_PER_GROUP
    outs, lses = [], []
    for g, (w, d) in enumerate(DIL_PAIRS):
        hs = slice(g * hg, (g + 1) * hg)

        def to_res(t):
            return t[:, :, hs].reshape(B, S // d, d, hg, D).transpose(0, 2, 1, 3, 4).reshape(B * d, S // d, hg, D)

        o, lse = banded_attention(to_res(q)[:, :, :, None, :], to_res(k), to_res(v),
                                  w // (2 * d), DIL_BLOCK, slopes[hs][:, None], d, None)
        outs.append(o.reshape(B, d, S // d, hg, D).transpose(0, 2, 1, 3, 4).reshape(B, S, hg, D))
        lses.append(lse.reshape(B, d, S // d, hg).transpose(0, 2, 1, 3).reshape(B, S, hg))
    alpha = jax.nn.softmax(jnp.stack(lses, axis=0), axis=0)
    y = jnp.concatenate([alpha[g][..., None].astype(outs[g].dtype) * outs[g]
                         for g in range(len(DIL_PAIRS))], axis=2)
    return y.reshape(B, S, DIL_WIDTH)


def windowed_gqa(q, k, v, slopes, sink):
    B, S, _ = q.shape
    qh = q.reshape(B, S, WIN_KV_HEADS, WIN_GROUP, HEAD_DIM)
    kh = k.reshape(B, S, WIN_KV_HEADS, HEAD_DIM)
    vh = v.reshape(B, S, WIN_KV_HEADS, HEAD_DIM)
    o, _ = banded_attention(qh, kh, vh, WIN_RADIUS, WIN_BLOCK,
                            slopes.reshape(WIN_KV_HEADS, WIN_GROUP), 1,
                            sink.reshape(WIN_KV_HEADS, WIN_GROUP))
    return o.reshape(B, S, WIN_Q_WIDTH)


def setup_inputs(seed: int = 0) -> dict:
    key = jax.random.key(seed)
    ks = jax.random.split(key, 16)
    f32 = jnp.float32
    nrm = lambda k, shape, s: jax.random.normal(k, shape, f32) * s
    return {
        "x": nrm(ks[0], (BATCH, SEQ, D_MODEL), 1.0),
        "c": nrm(ks[1], (BATCH, D_MODEL), 1.0),
        "norm1_g": 1.0 + nrm(ks[2], (DEPTH, D_MODEL), 0.1),
        "norm2_g": 1.0 + nrm(ks[3], (DEPTH, D_MODEL), 0.1),
        "w_ada": nrm(ks[4], (DEPTH, D_MODEL, N_MOD * D_MODEL), 0.5 * D_MODEL ** -0.5),
        "b_ada": nrm(ks[5], (DEPTH, N_MOD * D_MODEL), 0.02),
        "w_in": nrm(ks[6], (DEPTH, D_MODEL, IN_WIDTH), D_MODEL ** -0.5),
        "w_pool": nrm(ks[7], (DEPTH, len(POOL_WINDOWS), POOL_GROUP, POOL_GROUP), POOL_GROUP ** -0.5),
        "pool_scale": 1.0 + nrm(ks[8], (DEPTH, POOL_WIDTH), 0.1),
        "sink_logit": nrm(ks[9], (DEPTH, WIN_Q_HEADS), 1.0),
        "w_out": nrm(ks[10], (DEPTH, MIX_WIDTH, D_MODEL), MIX_WIDTH ** -0.5),
        "w_gate": nrm(ks[11], (DEPTH, D_MODEL, D_FF), D_MODEL ** -0.5),
        "w_up": nrm(ks[12], (DEPTH, D_MODEL, D_FF), D_MODEL ** -0.5),
        "w_down": nrm(ks[13], (DEPTH, D_FF, D_MODEL), D_FF ** -0.5),
        "final_g": 1.0 + nrm(ks[14], (D_MODEL,), 0.1),
    }


def reference(x, c, norm1_g, norm2_g, w_ada, b_ada, w_in, w_pool, pool_scale, sink_logit,
              w_out, w_gate, w_up, w_down, final_g):
    slopes = alibi_slopes()
    slopes_win = slopes[:WIN_Q_HEADS]
    slopes_dil = slopes[WIN_Q_HEADS:].astype(jnp.float32)
    o_q_b = POOL_WIDTH
    o_k_b = o_q_b + DIL_WIDTH
    o_v_b = o_k_b + DIL_WIDTH
    o_q_c = o_v_b + DIL_WIDTH
    o_k_c = o_q_c + WIN_Q_WIDTH
    o_v_c = o_k_c + WIN_KV_WIDTH
    B, S, _ = x.shape
    c_act = jax.nn.silu(c)
    for l in range(DEPTH):
        mod = (c_act @ w_ada[l] + b_ada[l])[:, None, :]
        sh1, sc1, g1, sh2, sc2, g2 = jnp.split(mod, N_MOD, axis=-1)
        h = rmsnorm(x, norm1_g[l]) * (1.0 + sc1) + sh1
        z = h @ w_in[l]
        y_a = multiscale_pool(z[..., :o_q_b], w_pool[l], pool_scale[l])
        qb = z[..., o_q_b:o_k_b].reshape(B, S, DIL_HEADS, HEAD_DIM)
        kb = z[..., o_k_b:o_v_b].reshape(B, S, DIL_HEADS, HEAD_DIM)
        vb = z[..., o_v_b:o_q_c].reshape(B, S, DIL_HEADS, HEAD_DIM)
        y_b = dilated_attention(qb, kb, vb, slopes_dil)
        y_c = windowed_gqa(z[..., o_q_c:o_k_c], z[..., o_k_c:o_v_c], z[..., o_v_c:],
                           slopes_win, sink_logit[l])
        mix = jnp.concatenate([y_a, y_b, y_c], axis=-1) @ w_out[l]
        x = x + g1 * mix
        h = rmsnorm(x, norm2_g[l]) * (1.0 + sc2) + sh2
        ffn = (jax.nn.silu(h @ w_gate[l]) * (h @ w_up[l])) @ w_down[l]
        x = x + g2 * ffn
    return rmsnorm(x, final_g)
```

```python
import functools
import math

import jax
import jax.numpy as jnp
import numpy as np
from jax import lax
from jax.experimental import pallas as pl
from jax.experimental.pallas import tpu as pltpu

LANES = 128
HEAD_DIM = 64
HALF = LANES // 2
POOL_WINDOWS = (2, 4, 8, 16)
POOL_HALO = 8
DIL_PAIRS = ((128, 1), (512, 4), (2048, 16))
DIL_RADIUS = 64
WIN_RADIUS = 128
WIN_GROUP = 3
N_MOD = 6
EPS = 1e-6
NEG = -1e30
QBLK = 128
VMEM_LIMIT = 56 * 1024 * 1024

_F32 = jnp.float32
_BF16 = jnp.bfloat16


def _alibi_slopes(n):
    i = np.arange(1, n + 1, dtype=np.float32)
    return np.exp2(np.float32(-8.0) * i / np.float32(n)).astype(np.float32)


def _cparams(sem):
    return pltpu.CompilerParams(dimension_semantics=sem, vmem_limit_bytes=VMEM_LIMIT)


def _mod_kernel(c_ref, w_ref, b_ref, o_ref):
    c = c_ref[...]
    act = (c * (1.0 / (1.0 + jnp.exp(-c)))).astype(_BF16)
    w = w_ref[0].astype(_BF16)
    o_ref[0] = jnp.dot(act, w, preferred_element_type=_F32) + b_ref[0]


def _modulation(c, w_ada, b_ada):
    depth, d, nd = w_ada.shape
    b = c.shape[0]
    tn = d
    return pl.pallas_call(
        _mod_kernel,
        out_shape=jax.ShapeDtypeStruct((depth, b, nd), _F32),
        grid=(depth, nd // tn),
        in_specs=[
            pl.BlockSpec((b, d), lambda l, j: (0, 0)),
            pl.BlockSpec((1, d, tn), lambda l, j: (l, 0, j)),
            pl.BlockSpec((1, 1, tn), lambda l, j: (l, 0, j)),
        ],
        out_specs=pl.BlockSpec((1, b, tn), lambda l, j: (l, 0, j)),
        compiler_params=_cparams(("arbitrary", "arbitrary")),
        name="modulation",
    )(c, w_ada, b_ada.reshape(depth, 1, nd))


def _modulated_norm(x, g, shift, scale):
    ms = jnp.mean(x * x, axis=-1, keepdims=True)
    return (x * lax.rsqrt(ms + EPS)) * (g * (1.0 + scale)) + shift


def _in_proj_kernel(x_ref, mod_ref, g_ref, w_ref,
                    u_ref, q0_ref, k0_ref, v0_ref, q1_ref, k1_ref, v1_ref,
                    q2_ref, k2_ref, v2_ref, qc_ref, kc_ref, vc_ref, zs_ref):
    mod = mod_ref[0, 0]
    h = _modulated_norm(x_ref[0], g_ref[...], mod[0:1], mod[1:2]).astype(_BF16)
    z = jnp.dot(h, w_ref[...], preferred_element_type=_F32)
    tm = z.shape[0]

    def tile(i):
        return z[:, i * LANES:(i + 1) * LANES]

    u_ref[0] = z[:, :2 * LANES]
    q0_ref[0] = tile(2).astype(_BF16)
    k0_ref[0] = tile(3).astype(_BF16)
    v0_ref[0] = tile(4).astype(_BF16)
    for s in range(6):
        zs_ref[s] = tile(5 + s)
    for refs, (_, dil), base in (((q1_ref, k1_ref, v1_ref), DIL_PAIRS[1], 0),
                                 ((q2_ref, k2_ref, v2_ref), DIL_PAIRS[2], 3)):
        rows = tm // dil
        for j, ref in enumerate(refs):
            for r in range(dil):
                ref[0, r] = zs_ref[base + j, pl.ds(r, rows, stride=dil), :].astype(_BF16)
    for t in range(WIN_GROUP):
        qc_ref[0, t] = tile(11 + t).astype(_BF16)
    kc_ref[0] = tile(14).astype(_BF16)
    vc_ref[0] = tile(15).astype(_BF16)


def _in_proj(x, mod, l, g, w_in_p, tm):
    b, s, d = x.shape
    n_in = w_in_p.shape[1]
    d1, d2 = DIL_PAIRS[1][1], DIL_PAIRS[2][1]
    nat = lambda: pl.BlockSpec((1, tm, LANES), lambda bi, i: (bi, i, 0))
    res = lambda dil: pl.BlockSpec((1, dil, tm // dil, LANES), lambda bi, i: (bi, 0, i, 0))
    sds = jax.ShapeDtypeStruct
    out_shape = (
        [sds((b, s, 2 * LANES), _F32)]
        + [sds((b, s, LANES), _BF16)] * 3
        + [sds((b, d1, s // d1, LANES), _BF16)] * 3
        + [sds((b, d2, s // d2, LANES), _BF16)] * 3
        + [sds((b, WIN_GROUP, s, LANES), _BF16)]
        + [sds((b, s, LANES), _BF16)] * 2
    )
    out_specs = (
        [pl.BlockSpec((1, tm, 2 * LANES), lambda bi, i: (bi, i, 0))]
        + [nat() for _ in range(3)]
        + [res(d1) for _ in range(3)]
        + [res(d2) for _ in range(3)]
        + [pl.BlockSpec((1, WIN_GROUP, tm, LANES), lambda bi, i: (bi, 0, i, 0))]
        + [nat() for _ in range(2)]
    )
    return pl.pallas_call(
        _in_proj_kernel,
        out_shape=out_shape,
        grid=(b, s // tm),
        in_specs=[
            pl.BlockSpec((1, tm, d), lambda bi, i: (bi, i, 0)),
            pl.BlockSpec((1, 1, N_MOD, d), lambda bi, i: (l, bi, 0, 0)),
            pl.BlockSpec((1, d), lambda bi, i: (0, 0)),
            pl.BlockSpec((d, n_in), lambda bi, i: (0, 0)),
        ],
        out_specs=out_specs,
        scratch_shapes=[pltpu.VMEM((6, tm, LANES), _F32)],
        compiler_params=_cparams(("arbitrary", "arbitrary")),
        name="in_proj",
    )(x, mod, g, w_in_p)


def _band_bias(shape, row_heads, slopes, dist_scale, radius, offset):
    rows, cols = shape
    row = lax.broadcasted_iota(jnp.int32, shape, 0)
    col = lax.broadcasted_iota(jnp.int32, shape, 1)
    dist = jnp.abs(col - offset - (row % QBLK))
    slope = jnp.full(shape, slopes[row_heads[0]] * dist_scale, _F32)
    for i, hd in enumerate(row_heads[1:], start=1):
        slope = jnp.where(row >= i * QBLK, np.float32(slopes[hd] * dist_scale), slope)
    return jnp.where(dist <= radius, -slope * dist.astype(_F32), NEG)


def _window(i, nblk, radius, width):
    seq = nblk * QBLK
    ws = jnp.clip(i * QBLK - radius, 0, seq - width)
    variant = jnp.where(i == 0, 0, jnp.where(i == nblk - 1, 2, 1))
    return ws, variant


def _trans_b_dot(a, b):
    return lax.dot_general(a, b, (((1,), (1,)), ((), ())), preferred_element_type=_F32)


def _band_attn_kernel(q_ref, k_ref, v_ref, o_ref, lse_ref, vext_ref, bias_ref, *,
                      dil, slopes):
    s_len = q_ref.shape[1]
    n = s_len // dil
    nblk = n // QBLK
    width = QBLK + 2 * DIL_RADIUS

    @pl.when(pl.program_id(0) == 0)
    def _():
        for var, off in enumerate((0, DIL_RADIUS, width - QBLK)):
            for hd in range(2):
                bias_ref[var, hd] = _band_bias((QBLK, width), (hd,), slopes, float(dil),
                                               DIL_RADIUS, off)

    vext_ref[:, :LANES] = v_ref[0]
    vext_ref[:, LANES:] = jnp.ones((s_len, LANES), _BF16)

    lane = lax.broadcasted_iota(jnp.int32, (QBLK, LANES), 1)
    lo = lane < HALF

    def body(blk, carry):
        r = blk // nblk
        i = blk % nblk
        ws, variant = _window(i, nblk, DIL_RADIUS, width)
        qstart = pl.multiple_of(blk * QBLK, QBLK)
        kstart = pl.multiple_of(r * n + ws, DIL_RADIUS)
        q = q_ref[0, pl.ds(qstart, QBLK), :]
        kw = k_ref[0, pl.ds(kstart, width), :]
        vw = vext_ref[pl.ds(kstart, width), :]
        outs, lses = [], []
        for hd in range(2):
            qh = jnp.where(lo if hd == 0 else ~lo, q, jnp.zeros_like(q))
            sc = _trans_b_dot(qh, kw) + bias_ref[variant, hd]
            m = jnp.max(sc, axis=-1, keepdims=True)
            e = jnp.exp(sc - m).astype(_BF16)
            o = jnp.dot(e, vw, preferred_element_type=_F32)
            den = o[:, LANES:]
            outs.append(o[:, :LANES] / den)
            lses.append(m + jnp.log(den))
        out = jnp.where(lo, outs[0], outs[1])
        lse = jnp.where(lo, lses[0], lses[1])
        if dil == 1:
            o_ref[0, pl.ds(qstart, QBLK), :] = out
            lse_ref[0, pl.ds(qstart, QBLK), :] = lse
        else:
            tstart = i * (QBLK * dil) + r
            o_ref[0, pl.ds(tstart, QBLK, stride=dil), :] = out
            lse_ref[0, pl.ds(tstart, QBLK, stride=dil), :] = lse
        return carry

    lax.fori_loop(0, s_len // QBLK, body, 0)


def _band_attn(q, k, v, dil, slopes):
    b, s, _ = q.shape
    width = QBLK + 2 * DIL_RADIUS
    spec = lambda: pl.BlockSpec((1, s, LANES), lambda bi: (bi, 0, 0))
    return pl.pallas_call(
        functools.partial(_band_attn_kernel, dil=dil, slopes=slopes),
        out_shape=[jax.ShapeDtypeStruct((b, s, LANES), _F32)] * 2,
        grid=(b,),
        in_specs=[spec(), spec(), spec()],
        out_specs=[spec(), spec()],
        scratch_shapes=[pltpu.VMEM((s, 2 * LANES), _BF16),
                        pltpu.VMEM((3, 2, QBLK, width), _F32)],
        compiler_params=_cparams(("arbitrary",)),
        name=f"band_attn_d{dil}",
    )(q, k, v)


def _win_attn_kernel(sink_ref, q_ref, k_ref, v_ref, o_ref, vext_ref, bias_ref, *, slopes):
    s_len = k_ref.shape[1]
    tc = q_ref.shape[2]
    nblk = s_len // QBLK
    width = QBLK + 2 * WIN_RADIUS
    rows = WIN_GROUP * QBLK

    @pl.when((pl.program_id(0) == 0) & (pl.program_id(1) == 0))
    def _():
        for var, off in enumerate((0, WIN_RADIUS, width - QBLK)):
            for j in range(2):
                heads = tuple(j * WIN_GROUP + t for t in range(WIN_GROUP))
                bias_ref[var, j] = _band_bias((rows, width), heads, slopes, 1.0,
                                              WIN_RADIUS, off)

    @pl.when(pl.program_id(1) == 0)
    def _():
        vext_ref[:, :LANES] = v_ref[0]
        vext_ref[:, LANES:] = jnp.ones((s_len, LANES), _BF16)

    lane = lax.broadcasted_iota(jnp.int32, (rows, LANES), 1)
    lo = lane < HALF
    base = pl.program_id(1) * (tc // QBLK)

    def body(bl, carry):
        i = base + bl
        ws, variant = _window(i, nblk, WIN_RADIUS, width)
        qstart = pl.multiple_of(bl * QBLK, QBLK)
        kstart = pl.multiple_of(ws, QBLK)
        q = q_ref[0, :, pl.ds(qstart, QBLK), :].reshape(rows, LANES)
        kw = k_ref[0, pl.ds(kstart, width), :]
        vw = vext_ref[pl.ds(kstart, width), :]
        outs = []
        for j in range(2):
            qh = jnp.where(lo if j == 0 else ~lo, q, jnp.zeros_like(q))
            sc = _trans_b_dot(qh, kw) + bias_ref[variant, j]
            m = jnp.concatenate(
                [jnp.maximum(jnp.max(sc[t * QBLK:(t + 1) * QBLK], axis=-1, keepdims=True),
                             sink_ref[j * WIN_GROUP + t])
                 for t in range(WIN_GROUP)], axis=0)
            sink = jnp.concatenate(
                [jnp.full((QBLK, 1), sink_ref[j * WIN_GROUP + t], _F32)
                 for t in range(WIN_GROUP)], axis=0)
            e = jnp.exp(sc - m).astype(_BF16)
            o = jnp.dot(e, vw, preferred_element_type=_F32)
            den = o[:, LANES:] + jnp.exp(sink - m)
            outs.append(o[:, :LANES] / den)
        out = jnp.where(lo, outs[0], outs[1]).astype(_BF16)
        for t in range(WIN_GROUP):
            o_ref[0, pl.ds(qstart, QBLK), t * LANES:(t + 1) * LANES] = (
                out[t * QBLK:(t + 1) * QBLK])
        return carry

    lax.fori_loop(0, tc // QBLK, body, 0)


def _win_attn(sink, q, k, v, slopes, tc):
    b, _, s, _ = q.shape
    width = QBLK + 2 * WIN_RADIUS
    rows = WIN_GROUP * QBLK
    return pl.pallas_call(
        functools.partial(_win_attn_kernel, slopes=slopes),
        out_shape=jax.ShapeDtypeStruct((b, s, WIN_GROUP * LANES), _BF16),
        grid=(b, s // tc),
        in_specs=[
            pl.BlockSpec(memory_space=pltpu.SMEM),
            pl.BlockSpec((1, WIN_GROUP, tc, LANES), lambda bi, i: (bi, 0, i, 0)),
            pl.BlockSpec((1, s, LANES), lambda bi, i: (bi, 0, 0)),
            pl.BlockSpec((1, s, LANES), lambda bi, i: (bi, 0, 0)),
        ],
        out_specs=pl.BlockSpec((1, tc, WIN_GROUP * LANES), lambda bi, i: (bi, i, 0)),
        scratch_shapes=[pltpu.VMEM((s, 2 * LANES), _BF16),
                        pltpu.VMEM((3, 2, rows, width), _F32)],
        compiler_params=_cparams(("arbitrary", "arbitrary")),
        name="win_attn",
    )(sink, q, k, v)


def _mix_out_kernel(x_ref, mod_ref, up_ref, u_ref, un_ref, wp_ref, ps_ref,
                    o0_ref, o1_ref, o2_ref, l0_ref, l1_ref, l2_ref, yc_ref, wo_ref,
                    out_ref, ubuf_ref, mix_ref, *, s_len):
    tm = x_ref.shape[1]
    i = pl.program_id(1)
    cw = 2 * LANES

    ubuf_ref[0:POOL_HALO] = jnp.where(i > 0, up_ref[0], 0.0)
    ubuf_ref[POOL_HALO:POOL_HALO + tm] = u_ref[0]
    ubuf_ref[POOL_HALO + tm:] = jnp.where(i < pl.num_programs(1) - 1, un_ref[0], 0.0)
    lane = lax.broadcasted_iota(jnp.int32, (1, cw), 1)
    radius = jnp.full((1, cw), POOL_WINDOWS[0] // 2, jnp.int32)
    for g, w in enumerate(POOL_WINDOWS[1:], start=1):
        radius = jnp.where(lane >= g * HEAD_DIM, w // 2, radius)
    u = u_ref[0]
    wsum = u
    for k in range(1, POOL_HALO + 1):
        pair = (ubuf_ref[POOL_HALO - k:POOL_HALO - k + tm]
                + ubuf_ref[POOL_HALO + k:POOL_HALO + k + tm])
        wsum = wsum + jnp.where(radius >= k, pair, 0.0)
    t = i * tm + lax.broadcasted_iota(jnp.int32, (tm, cw), 0)
    cnt = jnp.minimum(t + radius + 1, s_len) - jnp.maximum(t - radius, 0)
    pooled = (wsum / cnt.astype(_F32) - u).astype(_BF16)
    ya = jnp.dot(pooled, wp_ref[...], preferred_element_type=_F32) * ps_ref[...]
    mix_ref[:, :cw] = ya.astype(_BF16)

    lses = [l0_ref[0], l1_ref[0], l2_ref[0]]
    mx = jnp.maximum(jnp.maximum(lses[0], lses[1]), lses[2])
    es = [jnp.exp(ls - mx) for ls in lses]
    den = es[0] + es[1] + es[2]
    for g, o_ref in enumerate((o0_ref, o1_ref, o2_ref)):
        yb = (es[g] / den) * o_ref[0]
        mix_ref[:, cw + g * LANES:cw + (g + 1) * LANES] = yb.astype(_BF16)
    mix_ref[:, cw + 3 * LANES:] = yc_ref[0]

    mod = mod_ref[0, 0]
    y = jnp.dot(mix_ref[...], wo_ref[...], preferred_element_type=_F32)
    out_ref[0] = x_ref[0] + mod[2:3] * y


def _mix_out(x, mod, l, u, wpool_bd, pool_scale, outs, lses, yc, w_out_p, tm):
    b, s, d = x.shape
    cw = 2 * LANES
    hb = tm // POOL_HALO
    nhb = s // POOL_HALO
    tok = lambda width: pl.BlockSpec((1, tm, width), lambda bi, i: (bi, i, 0))
    full = lambda shape: pl.BlockSpec(shape, lambda bi, i: (0,) * len(shape))
    return pl.pallas_call(
        functools.partial(_mix_out_kernel, s_len=s),
        out_shape=jax.ShapeDtypeStruct((b, s, d), _F32),
        grid=(b, s // tm),
        in_specs=[
            tok(d),
            pl.BlockSpec((1, 1, N_MOD, d), lambda bi, i: (l, bi, 0, 0)),
            pl.BlockSpec((1, POOL_HALO, cw), lambda bi, i: (bi, jnp.maximum(i * hb - 1, 0), 0)),
            tok(cw),
            pl.BlockSpec((1, POOL_HALO, cw),
                         lambda bi, i: (bi, jnp.minimum((i + 1) * hb, nhb - 1), 0)),
            full((cw, cw)),
            full((1, cw)),
            tok(LANES), tok(LANES), tok(LANES),
            tok(LANES), tok(LANES), tok(LANES),
            tok(WIN_GROUP * LANES),
            full((d, d)),
        ],
        out_specs=tok(d),
        scratch_shapes=[pltpu.VMEM((tm + 2 * POOL_HALO, cw), _F32),
                        pltpu.VMEM((tm, d), _BF16)],
        compiler_params=_cparams(("arbitrary", "arbitrary")),
        name="mix_out",
    )(x, mod, u, u, u, wpool_bd, pool_scale, *outs, *lses, yc, w_out_p)


def _ffn_kernel(x_ref, mod_ref, g_ref, wg_ref, wu_ref, wd_ref, fg_ref, out_ref, *, final):
    mod = mod_ref[0, 0]
    x = x_ref[0]
    h = _modulated_norm(x, g_ref[...], mod[3:4], mod[4:5]).astype(_BF16)
    gate = jnp.dot(h, wg_ref[...], preferred_element_type=_F32)
    up = jnp.dot(h, wu_ref[...], preferred_element_type=_F32)
    act = ((gate * (1.0 / (1.0 + jnp.exp(-gate)))) * up).astype(_BF16)
    y = x + mod[5:6] * jnp.dot(act, wd_ref[...], preferred_element_type=_F32)
    if final:
        ms = jnp.mean(y * y, axis=-1, keepdims=True)
        y = (y * lax.rsqrt(ms + EPS)) * fg_ref[...]
    out_ref[0] = y


def _ffn(x, mod, l, g, wg, wu, wd, final_g, final, tm):
    b, s, d = x.shape
    dff = wg.shape[1]
    tok = pl.BlockSpec((1, tm, d), lambda bi, i: (bi, i, 0))
    const = lambda shape: pl.BlockSpec(shape, lambda bi, i: (0,) * len(shape),
                                       pipeline_mode=pl.Buffered(1))
    return pl.pallas_call(
        functools.partial(_ffn_kernel, final=final),
        out_shape=jax.ShapeDtypeStruct((b, s, d), _F32),
        grid=(b, s // tm),
        in_specs=[
            tok,
            pl.BlockSpec((1, 1, N_MOD, d), lambda bi, i: (l, bi, 0, 0)),
            const((1, d)),
            const((d, dff)), const((d, dff)), const((dff, d)),
            const((1, d)),
        ],
        out_specs=tok,
        compiler_params=_cparams(("arbitrary", "arbitrary")),
        name="ffn",
    )(x, mod, g, wg, wu, wd, final_g)


def _in_proj_columns(d_model):
    pool_w = d_model // 4
    dil_w = 2 * len(DIL_PAIRS) * HEAD_DIM
    o_qb = pool_w
    o_kb = o_qb + dil_w
    o_vb = o_kb + dil_w
    o_qc = o_vb + dil_w
    o_kc = o_qc + 2 * WIN_GROUP * HEAD_DIM
    o_vc = o_kc + 2 * HEAD_DIM
    cols = list(range(pool_w))
    for g in range(len(DIL_PAIRS)):
        for base in (o_qb, o_kb, o_vb):
            cols += range(base + g * LANES, base + (g + 1) * LANES)
    win_heads = []
    for t in range(WIN_GROUP):
        for j in range(2):
            win_heads.append(j * WIN_GROUP + t)
            cols += range(o_qc + (j * WIN_GROUP + t) * HEAD_DIM,
                          o_qc + (j * WIN_GROUP + t + 1) * HEAD_DIM)
    cols += range(o_kc, o_kc + LANES)
    cols += range(o_vc, o_vc + LANES)
    qscale = np.ones((len(cols),), np.float32)
    for g in range(len(DIL_PAIRS)):
        qscale[pool_w + 3 * g * LANES:pool_w + (3 * g + 1) * LANES] = HEAD_DIM ** -0.5
    q0 = pool_w + 3 * len(DIL_PAIRS) * LANES
    qscale[q0:q0 + WIN_GROUP * LANES] = HEAD_DIM ** -0.5
    return np.asarray(cols), qscale, win_heads


def kernel(x, c, norm1_g, norm2_g, w_ada, b_ada, w_in, w_pool, pool_scale, sink_logit,
           w_out, w_gate, w_up, w_down, final_g):
    b, s, d = x.shape
    depth = w_in.shape[0]
    assert HEAD_DIM ** -0.5 == 0.125
    slopes = _alibi_slopes(2 * WIN_GROUP + 2 * len(DIL_PAIRS))
    slopes_win = tuple(float(v) for v in slopes[:2 * WIN_GROUP])
    slopes_dil = slopes[2 * WIN_GROUP:]
    cols, qscale, win_heads = _in_proj_columns(d)
    pool_w = d // 4
    dil_w = 2 * len(DIL_PAIRS) * HEAD_DIM
    mix_rows = np.concatenate([
        np.arange(pool_w + dil_w),
        np.concatenate([pool_w + dil_w + h * HEAD_DIM + np.arange(HEAD_DIM) for h in win_heads]),
    ])
    n_grp = len(POOL_WINDOWS)
    pg = pool_w // n_grp

    mod = _modulation(c, w_ada, b_ada).reshape(depth, b, N_MOD, d)
    tm = 512
    for l in range(depth):
        w_in_p = (w_in[l][:, cols] * qscale).astype(_BF16)
        w_out_p = w_out[l][mix_rows].astype(_BF16)
        wpool_bd = jnp.zeros((pool_w, pool_w), _F32)
        for g in range(n_grp):
            wpool_bd = wpool_bd.at[g * pg:(g + 1) * pg, g * pg:(g + 1) * pg].set(w_pool[l, g])
        wpool_bd = wpool_bd.astype(_BF16)
        sink = sink_logit[l]

        (u, q0, k0, v0, q1, k1, v1, q2, k2, v2, qc, kc, vc) = _in_proj(
            x, mod, l, norm1_g[l].reshape(1, d), w_in_p, tm)
        outs, lses = [], []
        for g, ((_, dil), (q, k, v)) in enumerate(zip(
                DIL_PAIRS, ((q0, k0, v0), (q1, k1, v1), (q2, k2, v2)))):
            sl = tuple(float(v_) for v_ in slopes_dil[2 * g:2 * g + 2])
            o, ls = _band_attn(q.reshape(b, s, LANES), k.reshape(b, s, LANES),
                               v.reshape(b, s, LANES), dil, sl)
            outs.append(o)
            lses.append(ls)
        yc = _win_attn(sink, qc, kc, vc, slopes_win, 2048)
        x = _mix_out(x, mod, l, u, wpool_bd, pool_scale[l].reshape(1, pool_w), outs, lses, yc,
                     w_out_p, tm)
        x = _ffn(x, mod, l, norm2_g[l].reshape(1, d), w_gate[l].astype(_BF16),
                 w_up[l].astype(_BF16), w_down[l].astype(_BF16), final_g.reshape(1, d),
                 l == depth - 1, tm)
    return x
```

```python
import functools
import math

import jax
import jax.numpy as jnp
import numpy as np
from jax import lax
from jax.experimental import pallas as pl
from jax.experimental.pallas import tpu as pltpu

LANES = 128
HEAD_DIM = 64
HALF = LANES // 2
POOL_WINDOWS = (2, 4, 8, 16)
POOL_HALO = 8
DIL_PAIRS = ((128, 1), (512, 4), (2048, 16))
DIL_RADIUS = 64
WIN_RADIUS = 128
WIN_GROUP = 3
N_MOD = 6
EPS = 1e-6
NEG = -1e30
QBLK = 128
VMEM_LIMIT = 56 * 1024 * 1024

_F32 = jnp.float32
_BF16 = jnp.bfloat16


def _alibi_slopes(n):
    i = np.arange(1, n + 1, dtype=np.float32)
    return np.exp2(np.float32(-8.0) * i / np.float32(n)).astype(np.float32)


def _cparams(sem):
    return pltpu.CompilerParams(dimension_semantics=sem, vmem_limit_bytes=VMEM_LIMIT)


def _mod_kernel(c_ref, w_ref, b_ref, o_ref):
    c = c_ref[...]
    act = (c * (1.0 / (1.0 + jnp.exp(-c)))).astype(_BF16)
    w = w_ref[0].astype(_BF16)
    o_ref[0] = jnp.dot(act, w, preferred_element_type=_F32) + b_ref[0]


def _modulation(c, w_ada, b_ada):
    depth, d, nd = w_ada.shape
    b = c.shape[0]
    tn = d
    return pl.pallas_call(
        _mod_kernel,
        out_shape=jax.ShapeDtypeStruct((depth, b, nd), _F32),
        grid=(depth, nd // tn),
        in_specs=[
            pl.BlockSpec((b, d), lambda l, j: (0, 0)),
            pl.BlockSpec((1, d, tn), lambda l, j: (l, 0, j)),
            pl.BlockSpec((1, 1, tn), lambda l, j: (l, 0, j)),
        ],
        out_specs=pl.BlockSpec((1, b, tn), lambda l, j: (l, 0, j)),
        compiler_params=_cparams(("arbitrary", "arbitrary")),
        name="modulation",
    )(c, w_ada, b_ada.reshape(depth, 1, nd))


def _modulated_norm(x, g, shift, scale):
    ms = jnp.mean(x * x, axis=-1, keepdims=True)
    return (x * lax.rsqrt(ms + EPS)) * (g * (1.0 + scale)) + shift


def _in_proj_kernel(x_ref, mod_ref, g_ref, w_ref,
                    u_ref, q0_ref, k0_ref, v0_ref, q1_ref, k1_ref, v1_ref,
                    q2_ref, k2_ref, v2_ref, qc_ref, kc_ref, vc_ref, zs_ref):
    mod = mod_ref[0, 0]
    h = _modulated_norm(x_ref[0], g_ref[...], mod[0:1], mod[1:2]).astype(_BF16)
    z = jnp.dot(h, w_ref[...], preferred_element_type=_F32)
    tm = z.shape[0]

    def tile(i):
        return z[:, i * LANES:(i + 1) * LANES]

    u_ref[0] = z[:, :2 * LANES]
    q0_ref[0] = tile(2).astype(_BF16)
    k0_ref[0] = tile(3).astype(_BF16)
    v0_ref[0] = tile(4).astype(_BF16)
    for s in range(6):
        zs_ref[s] = tile(5 + s)
    for refs, (_, dil), base in (((q1_ref, k1_ref, v1_ref), DIL_PAIRS[1], 0),
                                 ((q2_ref, k2_ref, v2_ref), DIL_PAIRS[2], 3)):
        rows = tm // dil
        for j, ref in enumerate(refs):
            for r in range(dil):
                ref[0, r] = zs_ref[base + j, pl.ds(r, rows, stride=dil), :].astype(_BF16)
    for t in range(WIN_GROUP):
        qc_ref[0, t] = tile(11 + t).astype(_BF16)
    kc_ref[0] = tile(14).astype(_BF16)
    vc_ref[0] = tile(15).astype(_BF16)


def _in_proj(x, mod, l, g, w_in_p, tm):
    b, s, d = x.shape
    n_in = w_in_p.shape[1]
    d1, d2 = DIL_PAIRS[1][1], DIL_PAIRS[2][1]
    nat = lambda: pl.BlockSpec((1, tm, LANES), lambda bi, i: (bi, i, 0))
    res = lambda dil: pl.BlockSpec((1, dil, tm // dil, LANES), lambda bi, i: (bi, 0, i, 0))
    sds = jax.ShapeDtypeStruct
    out_shape = (
        [sds((b, s, 2 * LANES), _F32)]
        + [sds((b, s, LANES), _BF16)] * 3
        + [sds((b, d1, s // d1, LANES), _BF16)] * 3
        + [sds((b, d2, s // d2, LANES), _BF16)] * 3
        + [sds((b, WIN_GROUP, s, LANES), _BF16)]
        + [sds((b, s, LANES), _BF16)] * 2
    )
    out_specs = (
        [pl.BlockSpec((1, tm, 2 * LANES), lambda bi, i: (bi, i, 0))]
        + [nat() for _ in range(3)]
        + [res(d1) for _ in range(3)]
        + [res(d2) for _ in range(3)]
        + [pl.BlockSpec((1, WIN_GROUP, tm, LANES), lambda bi, i: (bi, 0, i, 0))]
        + [nat() for _ in range(2)]
    )
    return pl.pallas_call(
        _in_proj_kernel,
        out_shape=out_shape,
        grid=(b, s // tm),
        in_specs=[
            pl.BlockSpec((1, tm, d), lambda bi, i: (bi, i, 0)),
            pl.BlockSpec((1, 1, N_MOD, d), lambda bi, i: (l, bi, 0, 0)),
            pl.BlockSpec((1, d), lambda bi, i: (0, 0)),
            pl.BlockSpec((d, n_in), lambda bi, i: (0, 0)),
        ],
        out_specs=out_specs,
        scratch_shapes=[pltpu.VMEM((6, tm, LANES), _F32)],
        compiler_params=_cparams(("arbitrary", "arbitrary")),
        name="in_proj",
    )(x, mod, g, w_in_p)


def _band_bias(shape, row_heads, slopes, dist_scale, radius, offset):
    rows, cols = shape
    row = lax.broadcasted_iota(jnp.int32, shape, 0)
    col = lax.broadcasted_iota(jnp.int32, shape, 1)
    dist = jnp.abs(col - offset - (row % QBLK))
    slope = jnp.full(shape, slopes[row_heads[0]] * dist_scale, _F32)
    for i, hd in enumerate(row_heads[1:], start=1):
        slope = jnp.where(row >= i * QBLK, np.float32(slopes[hd] * dist_scale), slope)
    return jnp.where(dist <= radius, -slope * dist.astype(_F32), NEG)


def _window(i, nblk, radius, width):
    seq = nblk * QBLK
    ws = jnp.clip(i * QBLK - radius, 0, seq - width)
    variant = jnp.where(i == 0, 0, jnp.where(i == nblk - 1, 2, 1))
    return ws, variant


def _trans_b_dot(a, b):
    return lax.dot_general(a, b, (((1,), (1,)), ((), ())), preferred_element_type=_F32)


def _band_attn_kernel(q_ref, k_ref, v_ref, o_ref, lse_ref, vext_ref, bias_ref, *scratch,
                      dil, slopes):
    s_len = q_ref.shape[1]
    n = s_len // dil
    nblk = n // QBLK
    width = QBLK + 2 * DIL_RADIUS

    @pl.when(pl.program_id(0) == 0)
    def _():
        for var, off in enumerate((0, DIL_RADIUS, width - QBLK)):
            bias_ref[var] = _band_bias((2 * QBLK, width), (0, 1), slopes, float(dil),
                                       DIL_RADIUS, off)

    vext_ref[:, :LANES] = v_ref[0]
    vext_ref[:, LANES:] = jnp.ones((s_len, LANES), _BF16)

    lane = lax.broadcasted_iota(jnp.int32, (QBLK, LANES), 1)
    lo = lane < HALF

    def body(blk, carry):
        r = blk // nblk
        i = blk % nblk
        ws, variant = _window(i, nblk, DIL_RADIUS, width)
        qstart = pl.multiple_of(blk * QBLK, QBLK)
        kstart = pl.multiple_of(r * n + ws, DIL_RADIUS)
        q = q_ref[0, pl.ds(qstart, QBLK), :]
        kw = k_ref[0, pl.ds(kstart, width), :]
        vw = vext_ref[pl.ds(kstart, width), :]
        zero = jnp.zeros_like(q)
        qs = jnp.concatenate([jnp.where(lo, q, zero), jnp.where(lo, zero, q)], axis=0)
        sc = _trans_b_dot(qs, kw) + bias_ref[variant]
        m = jnp.max(sc, axis=-1, keepdims=True)
        e = jnp.exp(sc - m).astype(_BF16)
        o = jnp.dot(e, vw, preferred_element_type=_F32)
        den = o[:, LANES:]
        res = o[:, :LANES] / den
        lse_all = m + jnp.log(den)
        out = jnp.where(lo, res[:QBLK], res[QBLK:])
        lse = jnp.where(lo, lse_all[:QBLK], lse_all[QBLK:])
        if dil == 1:
            o_ref[0, pl.ds(qstart, QBLK), :] = out.astype(_BF16)
            lse_ref[0, pl.ds(qstart, QBLK), :] = lse
        else:
            tstart = i * (QBLK * dil) + r
            scratch[0][pl.ds(tstart, QBLK, stride=dil), :] = out
            lse_ref[0, pl.ds(tstart, QBLK, stride=dil), :] = lse
        return carry

    lax.fori_loop(0, s_len // QBLK, body, 0, unroll=4)
    if dil != 1:
        o_ref[0] = scratch[0][...].astype(_BF16)


def _band_attn(q, k, v, dil, slopes):
    b, s, _ = q.shape
    width = QBLK + 2 * DIL_RADIUS
    spec = lambda: pl.BlockSpec((1, s, LANES), lambda bi: (bi, 0, 0))
    scratch = [pltpu.VMEM((s, 2 * LANES), _BF16), pltpu.VMEM((3, 2 * QBLK, width), _F32)]
    if dil != 1:
        scratch.append(pltpu.VMEM((s, LANES), _F32))
    return pl.pallas_call(
        functools.partial(_band_attn_kernel, dil=dil, slopes=slopes),
        out_shape=[jax.ShapeDtypeStruct((b, s, LANES), _BF16),
                   jax.ShapeDtypeStruct((b, s, LANES), _F32)],
        grid=(b,),
        in_specs=[spec(), spec(), spec()],
        out_specs=[spec(), spec()],
        scratch_shapes=scratch,
        compiler_params=_cparams(("arbitrary",)),
        name=f"band_attn_d{dil}",
    )(q, k, v)


def _win_attn_kernel(sink_ref, q_ref, k_ref, v_ref, o_ref, vext_ref, bias_ref, *, slopes):
    s_len = k_ref.shape[1]
    tc = q_ref.shape[2]
    nblk = s_len // QBLK
    width = QBLK + 2 * WIN_RADIUS
    rows = WIN_GROUP * QBLK

    @pl.when((pl.program_id(0) == 0) & (pl.program_id(1) == 0))
    def _():
        for var, off in enumerate((0, WIN_RADIUS, width - QBLK)):
            for j in range(2):
                heads = tuple(j * WIN_GROUP + t for t in range(WIN_GROUP))
                bias_ref[var, j] = _band_bias((rows, width), heads, slopes, 1.0,
                                              WIN_RADIUS, off)

    @pl.when(pl.program_id(1) == 0)
    def _():
        vext_ref[:, :LANES] = v_ref[0]
        vext_ref[:, LANES:] = jnp.ones((s_len, LANES), _BF16)

    lane = lax.broadcasted_iota(jnp.int32, (rows, LANES), 1)
    lo = lane < HALF
    base = pl.program_id(1) * (tc // QBLK)

    def body(bl, carry):
        i = base + bl
        ws, variant = _window(i, nblk, WIN_RADIUS, width)
        qstart = pl.multiple_of(bl * QBLK, QBLK)
        kstart = pl.multiple_of(ws, QBLK)
        q = q_ref[0, :, pl.ds(qstart, QBLK), :].reshape(rows, LANES)
        kw = k_ref[0, pl.ds(kstart, width), :]
        vw = vext_ref[pl.ds(kstart, width), :]
        outs = []
        for j in range(2):
            qh = jnp.where(lo if j == 0 else ~lo, q, jnp.zeros_like(q))
            sc = _trans_b_dot(qh, kw) + bias_ref[variant, j]
            m = jnp.concatenate(
                [jnp.maximum(jnp.max(sc[t * QBLK:(t + 1) * QBLK], axis=-1, keepdims=True),
                             sink_ref[j * WIN_GROUP + t])
                 for t in range(WIN_GROUP)], axis=0)
            sink = jnp.concatenate(
                [jnp.full((QBLK, 1), sink_ref[j * WIN_GROUP + t], _F32)
                 for t in range(WIN_GROUP)], axis=0)
            e = jnp.exp(sc - m).astype(_BF16)
            o = jnp.dot(e, vw, preferred_element_type=_F32)
            den = o[:, LANES:] + jnp.exp(sink - m)
            outs.append(o[:, :LANES] / den)
        out = jnp.where(lo, outs[0], outs[1]).astype(_BF16)
        for t in range(WIN_GROUP):
            o_ref[0, pl.ds(qstart, QBLK), t * LANES:(t + 1) * LANES] = (
                out[t * QBLK:(t + 1) * QBLK])
        return carry

    lax.fori_loop(0, tc // QBLK, body, 0, unroll=2)


def _win_attn(sink, q, k, v, slopes, tc):
    b, _, s, _ = q.shape
    width = QBLK + 2 * WIN_RADIUS
    rows = WIN_GROUP * QBLK
    return pl.pallas_call(
        functools.partial(_win_attn_kernel, slopes=slopes),
        out_shape=jax.ShapeDtypeStruct((b, s, WIN_GROUP * LANES), _BF16),
        grid=(b, s // tc),
        in_specs=[
            pl.BlockSpec(memory_space=pltpu.SMEM),
            pl.BlockSpec((1, WIN_GROUP, tc, LANES), lambda bi, i: (bi, 0, i, 0)),
            pl.BlockSpec((1, s, LANES), lambda bi, i: (bi, 0, 0)),
            pl.BlockSpec((1, s, LANES), lambda bi, i: (bi, 0, 0)),
        ],
        out_specs=pl.BlockSpec((1, tc, WIN_GROUP * LANES), lambda bi, i: (bi, i, 0)),
        scratch_shapes=[pltpu.VMEM((s, 2 * LANES), _BF16),
                        pltpu.VMEM((3, 2, rows, width), _F32)],
        compiler_params=_cparams(("arbitrary", "arbitrary")),
        name="win_attn",
    )(sink, q, k, v)


def _mix_out_kernel(x_ref, mod_ref, up_ref, u_ref, un_ref, wp_ref, ps_ref,
                    o0_ref, o1_ref, o2_ref, l0_ref, l1_ref, l2_ref, yc_ref, wo_ref,
                    out_ref, ubuf_ref, mix_ref, *, s_len):
    tm = x_ref.shape[1]
    i = pl.program_id(1)
    cw = 2 * LANES

    ubuf_ref[0:POOL_HALO] = jnp.where(i > 0, up_ref[0], 0.0)
    ubuf_ref[POOL_HALO:POOL_HALO + tm] = u_ref[0]
    ubuf_ref[POOL_HALO + tm:] = jnp.where(i < pl.num_programs(1) - 1, un_ref[0], 0.0)
    lane = lax.broadcasted_iota(jnp.int32, (1, cw), 1)
    radius = jnp.full((1, cw), POOL_WINDOWS[0] // 2, jnp.int32)
    for g, w in enumerate(POOL_WINDOWS[1:], start=1):
        radius = jnp.where(lane >= g * HEAD_DIM, w // 2, radius)
    u = u_ref[0]
    wsum = u
    for k in range(1, POOL_HALO + 1):
        pair = (ubuf_ref[POOL_HALO - k:POOL_HALO - k + tm]
                + ubuf_ref[POOL_HALO + k:POOL_HALO + k + tm])
        wsum = wsum + jnp.where(radius >= k, pair, 0.0)
    t = i * tm + lax.broadcasted_iota(jnp.int32, (tm, cw), 0)
    cnt = jnp.minimum(t + radius + 1, s_len) - jnp.maximum(t - radius, 0)
    pooled = (wsum / cnt.astype(_F32) - u).astype(_BF16)
    ya = jnp.dot(pooled, wp_ref[...], preferred_element_type=_F32) * ps_ref[...]
    mix_ref[:, :cw] = ya.astype(_BF16)

    lses = [l0_ref[0], l1_ref[0], l2_ref[0]]
    mx = jnp.maximum(jnp.maximum(lses[0], lses[1]), lses[2])
    es = [jnp.exp(ls - mx) for ls in lses]
    den = es[0] + es[1] + es[2]
    for g, o_ref in enumerate((o0_ref, o1_ref, o2_ref)):
        yb = (es[g] / den) * o_ref[0]
        mix_ref[:, cw + g * LANES:cw + (g + 1) * LANES] = yb.astype(_BF16)
    mix_ref[:, cw + 3 * LANES:] = yc_ref[0]

    mod = mod_ref[0, 0]
    y = jnp.dot(mix_ref[...], wo_ref[...], preferred_element_type=_F32)
    out_ref[0] = x_ref[0] + mod[2:3] * y


def _mix_out(x, mod, l, u, wpool_bd, pool_scale, outs, lses, yc, w_out_p, tm):
    b, s, d = x.shape
    cw = 2 * LANES
    hb = tm // POOL_HALO
    nhb = s // POOL_HALO
    tok = lambda width: pl.BlockSpec((1, tm, width), lambda bi, i: (bi, i, 0))
    full = lambda shape: pl.BlockSpec(shape, lambda bi, i: (0,) * len(shape))
    return pl.pallas_call(
        functools.partial(_mix_out_kernel, s_len=s),
        out_shape=jax.ShapeDtypeStruct((b, s, d), _F32),
        grid=(b, s // tm),
        in_specs=[
            tok(d),
            pl.BlockSpec((1, 1, N_MOD, d), lambda bi, i: (l, bi, 0, 0)),
            pl.BlockSpec((1, POOL_HALO, cw), lambda bi, i: (bi, jnp.maximum(i * hb - 1, 0), 0)),
            tok(cw),
            pl.BlockSpec((1, POOL_HALO, cw),
                         lambda bi, i: (bi, jnp.minimum((i + 1) * hb, nhb - 1), 0)),
            full((cw, cw)),
            full((1, cw)),
            tok(LANES), tok(LANES), tok(LANES),
            tok(LANES), tok(LANES), tok(LANES),
            tok(WIN_GROUP * LANES),
            full((d, d)),
        ],
        out_specs=tok(d),
        scratch_shapes=[pltpu.VMEM((tm + 2 * POOL_HALO, cw), _F32),
                        pltpu.VMEM((tm, d), _BF16)],
        compiler_params=_cparams(("arbitrary", "arbitrary")),
        name="mix_out",
    )(x, mod, u, u, u, wpool_bd, pool_scale, *outs, *lses, yc, w_out_p)


def _ffn_kernel(x_ref, mod_ref, g_ref, wg_ref, wu_ref, wd_ref, fg_ref, out_ref, *, final):
    mod = mod_ref[0, 0]
    x = x_ref[0]
    h = _modulated_norm(x, g_ref[...], mod[3:4], mod[4:5]).astype(_BF16)
    gate = jnp.dot(h, wg_ref[...], preferred_element_type=_F32)
    up = jnp.dot(h, wu_ref[...], preferred_element_type=_F32)
    act = ((gate * (1.0 / (1.0 + jnp.exp(-gate)))) * up).astype(_BF16)
    y = x + mod[5:6] * jnp.dot(act, wd_ref[...], preferred_element_type=_F32)
    if final:
        ms = jnp.mean(y * y, axis=-1, keepdims=True)
        y = (y * lax.rsqrt(ms + EPS)) * fg_ref[...]
    out_ref[0] = y


def _ffn(x, mod, l, g, wg, wu, wd, final_g, final, tm):
    b, s, d = x.shape
    dff = wg.shape[1]
    tok = pl.BlockSpec((1, tm, d), lambda bi, i: (bi, i, 0))
    const = lambda shape: pl.BlockSpec(shape, lambda bi, i: (0,) * len(shape),
                                       pipeline_mode=pl.Buffered(1))
    return pl.pallas_call(
        functools.partial(_ffn_kernel, final=final),
        out_shape=jax.ShapeDtypeStruct((b, s, d), _F32),
        grid=(b, s // tm),
        in_specs=[
            tok,
            pl.BlockSpec((1, 1, N_MOD, d), lambda bi, i: (l, bi, 0, 0)),
            const((1, d)),
            const((d, dff)), const((d, dff)), const((dff, d)),
            const((1, d)),
        ],
        out_specs=tok,
        compiler_params=_cparams(("arbitrary", "arbitrary")),
        name="ffn",
    )(x, mod, g, wg, wu, wd, final_g)


def _in_proj_columns(d_model):
    pool_w = d_model // 4
    dil_w = 2 * len(DIL_PAIRS) * HEAD_DIM
    o_qb = pool_w
    o_kb = o_qb + dil_w
    o_vb = o_kb + dil_w
    o_qc = o_vb + dil_w
    o_kc = o_qc + 2 * WIN_GROUP * HEAD_DIM
    o_vc = o_kc + 2 * HEAD_DIM
    cols = list(range(pool_w))
    for g in range(len(DIL_PAIRS)):
        for base in (o_qb, o_kb, o_vb):
            cols += range(base + g * LANES, base + (g + 1) * LANES)
    win_heads = []
    for t in range(WIN_GROUP):
        for j in range(2):
            win_heads.append(j * WIN_GROUP + t)
            cols += range(o_qc + (j * WIN_GROUP + t) * HEAD_DIM,
                          o_qc + (j * WIN_GROUP + t + 1) * HEAD_DIM)
    cols += range(o_kc, o_kc + LANES)
    cols += range(o_vc, o_vc + LANES)
    qscale = np.ones((len(cols),), np.float32)
    for g in range(len(DIL_PAIRS)):
        qscale[pool_w + 3 * g * LANES:pool_w + (3 * g + 1) * LANES] = HEAD_DIM ** -0.5
    q0 = pool_w + 3 * len(DIL_PAIRS) * LANES
    qscale[q0:q0 + WIN_GROUP * LANES] = HEAD_DIM ** -0.5
    return np.asarray(cols), qscale, win_heads


def kernel(x, c, norm1_g, norm2_g, w_ada, b_ada, w_in, w_pool, pool_scale, sink_logit,
           w_out, w_gate, w_up, w_down, final_g):
    b, s, d = x.shape
    depth = w_in.shape[0]
    assert HEAD_DIM ** -0.5 == 0.125
    slopes = _alibi_slopes(2 * WIN_GROUP + 2 * len(DIL_PAIRS))
    slopes_win = tuple(float(v) for v in slopes[:2 * WIN_GROUP])
    slopes_dil = slopes[2 * WIN_GROUP:]
    cols, qscale, win_heads = _in_proj_columns(d)
    pool_w = d // 4
    dil_w = 2 * len(DIL_PAIRS) * HEAD_DIM
    mix_rows = np.concatenate([
        np.arange(pool_w + dil_w),
        np.concatenate([pool_w + dil_w + h * HEAD_DIM + np.arange(HEAD_DIM) for h in win_heads]),
    ])
    n_grp = len(POOL_WINDOWS)
    pg = pool_w // n_grp

    mod = _modulation(c, w_ada, b_ada).reshape(depth, b, N_MOD, d)
    tm = 512
    for l in range(depth):
        w_in_p = (w_in[l][:, cols] * qscale).astype(_BF16)
        w_out_p = w_out[l][mix_rows].astype(_BF16)
        wpool_bd = jnp.zeros((pool_w, pool_w), _F32)
        for g in range(n_grp):
            wpool_bd = wpool_bd.at[g * pg:(g + 1) * pg, g * pg:(g + 1) * pg].set(w_pool[l, g])
        wpool_bd = wpool_bd.astype(_BF16)
        sink = sink_logit[l]

        (u, q0, k0, v0, q1, k1, v1, q2, k2, v2, qc, kc, vc) = _in_proj(
            x, mod, l, norm1_g[l].reshape(1, d), w_in_p, tm)
        outs, lses = [], []
        for g, ((_, dil), (q, k, v)) in enumerate(zip(
                DIL_PAIRS, ((q0, k0, v0), (q1, k1, v1), (q2, k2, v2)))):
            sl = tuple(float(v_) for v_ in slopes_dil[2 * g:2 * g + 2])
            o, ls = _band_attn(q.reshape(b, s, LANES), k.reshape(b, s, LANES),
                               v.reshape(b, s, LANES), dil, sl)
            outs.append(o)
            lses.append(ls)
        yc = _win_attn(sink, qc, kc, vc, slopes_win, 2048)
        x = _mix_out(x, mod, l, u, wpool_bd, pool_scale[l].reshape(1, pool_w), outs, lses, yc,
                     w_out_p, tm)
        x = _ffn(x, mod, l, norm2_g[l].reshape(1, d), w_gate[l].astype(_BF16),
                 w_up[l].astype(_BF16), w_down[l].astype(_BF16), final_g.reshape(1, d),
                 l == depth - 1, tm)
    return x
```

```python
import functools
import math

import jax
import jax.numpy as jnp
import numpy as np
from jax import lax
from jax.experimental import pallas as pl
from jax.experimental.pallas import tpu as pltpu

LANES = 128
HEAD_DIM = 64
HALF = LANES // 2
POOL_WINDOWS = (2, 4, 8, 16)
POOL_HALO = 8
DIL_PAIRS = ((128, 1), (512, 4), (2048, 16))
DIL_RADIUS = 64
WIN_RADIUS = 128
WIN_GROUP = 3
N_MOD = 6
EPS = 1e-6
NEG = -1e30
QBLK = 128
VMEM_LIMIT = 56 * 1024 * 1024

_F32 = jnp.float32
_BF16 = jnp.bfloat16


def _alibi_slopes(n):
    i = np.arange(1, n + 1, dtype=np.float32)
    return np.exp2(np.float32(-8.0) * i / np.float32(n)).astype(np.float32)


def _cparams(sem):
    return pltpu.CompilerParams(dimension_semantics=sem, vmem_limit_bytes=VMEM_LIMIT)


def _mod_kernel(c_ref, w_ref, b_ref, o_ref):
    c = c_ref[...]
    act = (c * (1.0 / (1.0 + jnp.exp(-c)))).astype(_BF16)
    w = w_ref[0].astype(_BF16)
    o_ref[0] = jnp.dot(act, w, preferred_element_type=_F32) + b_ref[0]


def _modulation(c, w_ada, b_ada):
    depth, d, nd = w_ada.shape
    b = c.shape[0]
    tn = d
    return pl.pallas_call(
        _mod_kernel,
        out_shape=jax.ShapeDtypeStruct((depth, b, nd), _F32),
        grid=(depth, nd // tn),
        in_specs=[
            pl.BlockSpec((b, d), lambda l, j: (0, 0)),
            pl.BlockSpec((1, d, tn), lambda l, j: (l, 0, j)),
            pl.BlockSpec((1, 1, tn), lambda l, j: (l, 0, j)),
        ],
        out_specs=pl.BlockSpec((1, b, tn), lambda l, j: (l, 0, j)),
        compiler_params=_cparams(("arbitrary", "arbitrary")),
        name="modulation",
    )(c, w_ada, b_ada.reshape(depth, 1, nd))


def _modulated_norm(x, g, shift, scale):
    ms = jnp.mean(x * x, axis=-1, keepdims=True)
    return (x * lax.rsqrt(ms + EPS)) * (g * (1.0 + scale)) + shift


def _in_proj_kernel(x_ref, mod_ref, g_ref, w_ref,
                    u_ref, q0_ref, k0_ref, v0_ref, q1_ref, k1_ref, v1_ref,
                    q2_ref, k2_ref, v2_ref, qc_ref, kc_ref, vc_ref, zs_ref):
    mod = mod_ref[0, 0]
    h = _modulated_norm(x_ref[0], g_ref[...], mod[0:1], mod[1:2]).astype(_BF16)
    z = jnp.dot(h, w_ref[...], preferred_element_type=_F32)
    tm = z.shape[0]

    def tile(i):
        return z[:, i * LANES:(i + 1) * LANES]

    u_ref[0] = z[:, :2 * LANES]
    q0_ref[0] = tile(2).astype(_BF16)
    k0_ref[0] = tile(3).astype(_BF16)
    v0_ref[0] = tile(4).astype(_BF16)
    for s in range(6):
        zs_ref[s] = tile(5 + s)
    for refs, (_, dil), base in (((q1_ref, k1_ref, v1_ref), DIL_PAIRS[1], 0),
                                 ((q2_ref, k2_ref, v2_ref), DIL_PAIRS[2], 3)):
        rows = tm // dil
        for j, ref in enumerate(refs):
            for r in range(dil):
                ref[0, r] = zs_ref[base + j, pl.ds(r, rows, stride=dil), :].astype(_BF16)
    for t in range(WIN_GROUP):
        qc_ref[0, t] = tile(11 + t).astype(_BF16)
    kc_ref[0] = tile(14).astype(_BF16)
    vc_ref[0] = tile(15).astype(_BF16)


def _in_proj(x, mod, l, g, w_in_p, tm):
    b, s, d = x.shape
    n_in = w_in_p.shape[1]
    d1, d2 = DIL_PAIRS[1][1], DIL_PAIRS[2][1]
    nat = lambda: pl.BlockSpec((1, tm, LANES), lambda bi, i: (bi, i, 0))
    res = lambda dil: pl.BlockSpec((1, dil, tm // dil, LANES), lambda bi, i: (bi, 0, i, 0))
    sds = jax.ShapeDtypeStruct
    out_shape = (
        [sds((b, s, 2 * LANES), _F32)]
        + [sds((b, s, LANES), _BF16)] * 3
        + [sds((b, d1, s // d1, LANES), _BF16)] * 3
        + [sds((b, d2, s // d2, LANES), _BF16)] * 3
        + [sds((b, WIN_GROUP, s, LANES), _BF16)]
        + [sds((b, s, LANES), _BF16)] * 2
    )
    out_specs = (
        [pl.BlockSpec((1, tm, 2 * LANES), lambda bi, i: (bi, i, 0))]
        + [nat() for _ in range(3)]
        + [res(d1) for _ in range(3)]
        + [res(d2) for _ in range(3)]
        + [pl.BlockSpec((1, WIN_GROUP, tm, LANES), lambda bi, i: (bi, 0, i, 0))]
        + [nat() for _ in range(2)]
    )
    return pl.pallas_call(
        _in_proj_kernel,
        out_shape=out_shape,
        grid=(b, s // tm),
        in_specs=[
            pl.BlockSpec((1, tm, d), lambda bi, i: (bi, i, 0)),
            pl.BlockSpec((1, 1, N_MOD, d), lambda bi, i: (l, bi, 0, 0)),
            pl.BlockSpec((1, d), lambda bi, i: (0, 0)),
            pl.BlockSpec((d, n_in), lambda bi, i: (0, 0)),
        ],
        out_specs=out_specs,
        scratch_shapes=[pltpu.VMEM((6, tm, LANES), _F32)],
        compiler_params=_cparams(("arbitrary", "arbitrary")),
        name="in_proj",
    )(x, mod, g, w_in_p)


def _band_bias(shape, row_heads, slopes, dist_scale, radius, offset):
    rows, cols = shape
    row = lax.broadcasted_iota(jnp.int32, shape, 0)
    col = lax.broadcasted_iota(jnp.int32, shape, 1)
    dist = jnp.abs(col - offset - (row % QBLK))
    slope = jnp.full(shape, slopes[row_heads[0]] * dist_scale, _F32)
    for i, hd in enumerate(row_heads[1:], start=1):
        slope = jnp.where(row >= i * QBLK, np.float32(slopes[hd] * dist_scale), slope)
    return jnp.where(dist <= radius, -slope * dist.astype(_F32), NEG)


def _window(i, nblk, radius, width):
    seq = nblk * QBLK
    ws = jnp.clip(i * QBLK - radius, 0, seq - width)
    variant = jnp.where(i == 0, 0, jnp.where(i == nblk - 1, 2, 1))
    return ws, variant


def _trans_b_dot(a, b):
    return lax.dot_general(a, b, (((1,), (1,)), ((), ())), preferred_element_type=_F32)


def _band_attn_kernel(q_ref, k_ref, v_ref, o_ref, lse_ref, vext_ref, bias_ref, *scratch,
                      dil, slopes):
    s_len = q_ref.shape[1]
    n = s_len // dil
    nblk = n // QBLK
    width = QBLK + 2 * DIL_RADIUS

    @pl.when(pl.program_id(0) == 0)
    def _():
        for var, off in enumerate((0, DIL_RADIUS, width - QBLK)):
            bias_ref[var] = _band_bias((2 * QBLK, width), (0, 1), slopes, float(dil),
                                       DIL_RADIUS, off)

    vext_ref[:, :LANES] = v_ref[0]
    vext_ref[:, LANES:] = jnp.ones((s_len, LANES), _BF16)

    lane = lax.broadcasted_iota(jnp.int32, (QBLK, LANES), 1)
    lo = lane < HALF

    def body(blk, carry):
        r = blk // nblk
        i = blk % nblk
        ws, variant = _window(i, nblk, DIL_RADIUS, width)
        qstart = pl.multiple_of(blk * QBLK, QBLK)
        kstart = pl.multiple_of(r * n + ws, DIL_RADIUS)
        q = q_ref[0, pl.ds(qstart, QBLK), :]
        kw = k_ref[0, pl.ds(kstart, width), :]
        vw = vext_ref[pl.ds(kstart, width), :]
        zero = jnp.zeros_like(q)
        qs = jnp.concatenate([jnp.where(lo, q, zero), jnp.where(lo, zero, q)], axis=0)
        sc = _trans_b_dot(qs, kw) + bias_ref[variant]
        m = jnp.max(sc, axis=-1, keepdims=True)
        e = jnp.exp(sc - m).astype(_BF16)
        o = jnp.dot(e, vw, preferred_element_type=_F32)
        den = o[:, LANES:]
        res = o[:, :LANES] / den
        lse_all = m + jnp.log(den)
        out = jnp.where(lo, res[:QBLK], res[QBLK:])
        lse = jnp.where(lo, lse_all[:QBLK], lse_all[QBLK:])
        if dil == 1:
            o_ref[0, pl.ds(qstart, QBLK), :] = out.astype(_BF16)
            lse_ref[0, pl.ds(qstart, QBLK), :] = lse
        else:
            tstart = i * (QBLK * dil) + r
            scratch[0][pl.ds(tstart, QBLK, stride=dil), :] = out
            lse_ref[0, pl.ds(tstart, QBLK, stride=dil), :] = lse
        return carry

    lax.fori_loop(0, s_len // QBLK, body, 0, unroll=4)
    if dil != 1:
        o_ref[0] = scratch[0][...].astype(_BF16)


def _band_attn(q, k, v, dil, slopes):
    b, s, _ = q.shape
    width = QBLK + 2 * DIL_RADIUS
    spec = lambda: pl.BlockSpec((1, s, LANES), lambda bi: (bi, 0, 0))
    scratch = [pltpu.VMEM((s, 2 * LANES), _BF16), pltpu.VMEM((3, 2 * QBLK, width), _F32)]
    if dil != 1:
        scratch.append(pltpu.VMEM((s, LANES), _F32))
    return pl.pallas_call(
        functools.partial(_band_attn_kernel, dil=dil, slopes=slopes),
        out_shape=[jax.ShapeDtypeStruct((b, s, LANES), _BF16),
                   jax.ShapeDtypeStruct((b, s, LANES), _F32)],
        grid=(b,),
        in_specs=[spec(), spec(), spec()],
        out_specs=[spec(), spec()],
        scratch_shapes=scratch,
        compiler_params=_cparams(("arbitrary",)),
        name=f"band_attn_d{dil}",
    )(q, k, v)


def _win_attn_kernel(sink_ref, q_ref, k_ref, v_ref, o_ref, vext_ref, bias_ref,
                     s0_ref, s1_ref, e0_ref, e1_ref, t0_ref, t1_ref, *, slopes):
    s_len = k_ref.shape[1]
    tc = q_ref.shape[2]
    nblk = s_len // QBLK
    nloc = tc // QBLK
    width = QBLK + 2 * WIN_RADIUS
    rows = WIN_GROUP * QBLK
    s_bufs, e_bufs, t_bufs = (s0_ref, s1_ref), (e0_ref, e1_ref), (t0_ref, t1_ref)

    @pl.when((pl.program_id(0) == 0) & (pl.program_id(1) == 0))
    def _():
        for var, off in enumerate((0, WIN_RADIUS, width - QBLK)):
            for j in range(2):
                heads = tuple(j * WIN_GROUP + t for t in range(WIN_GROUP))
                bias_ref[var, j] = _band_bias((rows, width), heads, slopes, 1.0,
                                              WIN_RADIUS, off)

    @pl.when(pl.program_id(1) == 0)
    def _():
        vext_ref[:, :LANES] = v_ref[0]
        vext_ref[:, LANES:] = jnp.ones((s_len, LANES), _BF16)

    lane = lax.broadcasted_iota(jnp.int32, (rows, LANES), 1)
    lo = lane < HALF
    row = lax.broadcasted_iota(jnp.int32, (rows, 1), 0)
    base = pl.program_id(1) * nloc

    def sink_col(j):
        col = jnp.full((rows, 1), sink_ref[j * WIN_GROUP], _F32)
        for t in range(1, WIN_GROUP):
            col = jnp.where(row >= t * QBLK, sink_ref[j * WIN_GROUP + t], col)
        return col

    def kstart_of(bl):
        ws, variant = _window(base + bl, nblk, WIN_RADIUS, width)
        return pl.multiple_of(ws, QBLK), variant

    def scores(bl, par):
        kstart, variant = kstart_of(bl)
        qstart = pl.multiple_of(bl * QBLK, QBLK)
        q = q_ref[0, :, pl.ds(qstart, QBLK), :].reshape(rows, LANES)
        kw = k_ref[0, pl.ds(kstart, width), :]
        zero = jnp.zeros_like(q)
        for j in range(2):
            qh = jnp.where(lo, q, zero) if j == 0 else jnp.where(lo, zero, q)
            s_bufs[par][j] = _trans_b_dot(qh, kw) + bias_ref[variant, j]

    def numerators(par):
        for j in range(2):
            sc = s_bufs[par][j]
            mx = sc[:, :LANES]
            for c in range(1, width // LANES):
                mx = jnp.maximum(mx, sc[:, c * LANES:(c + 1) * LANES])
            sink = sink_col(j)
            m = jnp.maximum(jnp.max(mx, axis=-1, keepdims=True), sink)
            e_bufs[par][j] = jnp.exp(sc - m).astype(_BF16)
            t_bufs[par][j] = jnp.broadcast_to(jnp.exp(sink - m), (rows, LANES))

    def outputs(bl, par):
        kstart, _ = kstart_of(bl)
        qstart = pl.multiple_of(bl * QBLK, QBLK)
        vw = vext_ref[pl.ds(kstart, width), :]
        res = []
        for j in range(2):
            o = jnp.dot(e_bufs[par][j], vw, preferred_element_type=_F32)
            res.append(o[:, :LANES] / (o[:, LANES:] + t_bufs[par][j]))
        out = jnp.where(lo, res[0], res[1]).astype(_BF16)
        for t in range(WIN_GROUP):
            o_ref[0, pl.ds(qstart, QBLK), t * LANES:(t + 1) * LANES] = (
                out[t * QBLK:(t + 1) * QBLK])

    scores(0, 0)
    scores(1, 1)
    numerators(0)

    def body(p, carry):
        it = 2 * p
        scores(it, 0)
        numerators(1)
        outputs(it - 2, 0)
        scores(it + 1, 1)
        numerators(0)
        outputs(it - 1, 1)
        return carry

    lax.fori_loop(1, nloc // 2, body, 0)
    numerators(1)
    outputs(nloc - 2, 0)
    outputs(nloc - 1, 1)


def _win_attn(sink, q, k, v, slopes, tc):
    b, _, s, _ = q.shape
    width = QBLK + 2 * WIN_RADIUS
    rows = WIN_GROUP * QBLK
    assert (tc // QBLK) % 2 == 0 and tc // QBLK >= 4
    return pl.pallas_call(
        functools.partial(_win_attn_kernel, slopes=slopes),
        out_shape=jax.ShapeDtypeStruct((b, s, WIN_GROUP * LANES), _BF16),
        grid=(b, s // tc),
        in_specs=[
            pl.BlockSpec(memory_space=pltpu.SMEM),
            pl.BlockSpec((1, WIN_GROUP, tc, LANES), lambda bi, i: (bi, 0, i, 0)),
            pl.BlockSpec((1, s, LANES), lambda bi, i: (bi, 0, 0)),
            pl.BlockSpec((1, s, LANES), lambda bi, i: (bi, 0, 0)),
        ],
        out_specs=pl.BlockSpec((1, tc, WIN_GROUP * LANES), lambda bi, i: (bi, i, 0)),
        scratch_shapes=[pltpu.VMEM((s, 2 * LANES), _BF16),
                        pltpu.VMEM((3, 2, rows, width), _F32)]
                       + [pltpu.VMEM((2, rows, width), _F32)] * 2
                       + [pltpu.VMEM((2, rows, width), _BF16)] * 2
                       + [pltpu.VMEM((2, rows, LANES), _F32)] * 2,
        compiler_params=_cparams(("arbitrary", "arbitrary")),
        name="win_attn",
    )(sink, q, k, v)


def _mix_out_kernel(x_ref, mod_ref, up_ref, u_ref, un_ref, wp_ref, ps_ref,
                    o0_ref, o1_ref, o2_ref, l0_ref, l1_ref, l2_ref, yc_ref, wo_ref,
                    out_ref, ubuf_ref, mix_ref, *, s_len):
    tm = x_ref.shape[1]
    i = pl.program_id(1)
    cw = 2 * LANES

    ubuf_ref[0:POOL_HALO] = jnp.where(i > 0, up_ref[0], 0.0)
    ubuf_ref[POOL_HALO:POOL_HALO + tm] = u_ref[0]
    ubuf_ref[POOL_HALO + tm:] = jnp.where(i < pl.num_programs(1) - 1, un_ref[0], 0.0)
    lane = lax.broadcasted_iota(jnp.int32, (1, cw), 1)
    radius = jnp.full((1, cw), POOL_WINDOWS[0] // 2, jnp.int32)
    for g, w in enumerate(POOL_WINDOWS[1:], start=1):
        radius = jnp.where(lane >= g * HEAD_DIM, w // 2, radius)
    u = u_ref[0]
    wsum = u
    for k in range(1, POOL_HALO + 1):
        pair = (ubuf_ref[POOL_HALO - k:POOL_HALO - k + tm]
                + ubuf_ref[POOL_HALO + k:POOL_HALO + k + tm])
        wsum = wsum + jnp.where(radius >= k, pair, 0.0)
    t = i * tm + lax.broadcasted_iota(jnp.int32, (tm, cw), 0)
    cnt = jnp.minimum(t + radius + 1, s_len) - jnp.maximum(t - radius, 0)
    pooled = (wsum / cnt.astype(_F32) - u).astype(_BF16)
    ya = jnp.dot(pooled, wp_ref[...], preferred_element_type=_F32) * ps_ref[...]
    mix_ref[:, :cw] = ya.astype(_BF16)

    lses = [l0_ref[0], l1_ref[0], l2_ref[0]]
    mx = jnp.maximum(jnp.maximum(lses[0], lses[1]), lses[2])
    es = [jnp.exp(ls - mx) for ls in lses]
    den = es[0] + es[1] + es[2]
    for g, o_ref in enumerate((o0_ref, o1_ref, o2_ref)):
        yb = (es[g] / den) * o_ref[0]
        mix_ref[:, cw + g * LANES:cw + (g + 1) * LANES] = yb.astype(_BF16)
    mix_ref[:, cw + 3 * LANES:] = yc_ref[0]

    mod = mod_ref[0, 0]
    y = jnp.dot(mix_ref[...], wo_ref[...], preferred_element_type=_F32)
    out_ref[0] = x_ref[0] + mod[2:3] * y


def _mix_out(x, mod, l, u, wpool_bd, pool_scale, outs, lses, yc, w_out_p, tm):
    b, s, d = x.shape
    cw = 2 * LANES
    hb = tm // POOL_HALO
    nhb = s // POOL_HALO
    tok = lambda width: pl.BlockSpec((1, tm, width), lambda bi, i: (bi, i, 0))
    full = lambda shape: pl.BlockSpec(shape, lambda bi, i: (0,) * len(shape))
    return pl.pallas_call(
        functools.partial(_mix_out_kernel, s_len=s),
        out_shape=jax.ShapeDtypeStruct((b, s, d), _F32),
        grid=(b, s // tm),
        in_specs=[
            tok(d),
            pl.BlockSpec((1, 1, N_MOD, d), lambda bi, i: (l, bi, 0, 0)),
            pl.BlockSpec((1, POOL_HALO, cw), lambda bi, i: (bi, jnp.maximum(i * hb - 1, 0), 0)),
            tok(cw),
            pl.BlockSpec((1, POOL_HALO, cw),
                         lambda bi, i: (bi, jnp.minimum((i + 1) * hb, nhb - 1), 0)),
            full((cw, cw)),
            full((1, cw)),
            tok(LANES), tok(LANES), tok(LANES),
            tok(LANES), tok(LANES), tok(LANES),
            tok(WIN_GROUP * LANES),
            full((d, d)),
        ],
        out_specs=tok(d),
        scratch_shapes=[pltpu.VMEM((tm + 2 * POOL_HALO, cw), _F32),
                        pltpu.VMEM((tm, d), _BF16)],
        compiler_params=_cparams(("arbitrary", "arbitrary")),
        name="mix_out",
    )(x, mod, u, u, u, wpool_bd, pool_scale, *outs, *lses, yc, w_out_p)


def _ffn_kernel(x_ref, mod_ref, g_ref, wg_ref, wu_ref, wd_ref, fg_ref, out_ref, *, final):
    mod = mod_ref[0, 0]
    x = x_ref[0]
    h = _modulated_norm(x, g_ref[...], mod[3:4], mod[4:5]).astype(_BF16)
    gate = jnp.dot(h, wg_ref[...], preferred_element_type=_F32)
    up = jnp.dot(h, wu_ref[...], preferred_element_type=_F32)
    act = ((gate * (1.0 / (1.0 + jnp.exp(-gate)))) * up).astype(_BF16)
    y = x + mod[5:6] * jnp.dot(act, wd_ref[...], preferred_element_type=_F32)
    if final:
        ms = jnp.mean(y * y, axis=-1, keepdims=True)
        y = (y * lax.rsqrt(ms + EPS)) * fg_ref[...]
    out_ref[0] = y


def _ffn(x, mod, l, g, wg, wu, wd, final_g, final, tm):
    b, s, d = x.shape
    dff = wg.shape[1]
    tok = pl.BlockSpec((1, tm, d), lambda bi, i: (bi, i, 0))
    const = lambda shape: pl.BlockSpec(shape, lambda bi, i: (0,) * len(shape),
                                       pipeline_mode=pl.Buffered(1))
    return pl.pallas_call(
        functools.partial(_ffn_kernel, final=final),
        out_shape=jax.ShapeDtypeStruct((b, s, d), _F32),
        grid=(b, s // tm),
        in_specs=[
            tok,
            pl.BlockSpec((1, 1, N_MOD, d), lambda bi, i: (l, bi, 0, 0)),
            const((1, d)),
            const((d, dff)), const((d, dff)), const((dff, d)),
            const((1, d)),
        ],
        out_specs=tok,
        compiler_params=_cparams(("arbitrary", "arbitrary")),
        name="ffn",
    )(x, mod, g, wg, wu, wd, final_g)


def _in_proj_columns(d_model):
    pool_w = d_model // 4
    dil_w = 2 * len(DIL_PAIRS) * HEAD_DIM
    o_qb = pool_w
    o_kb = o_qb + dil_w
    o_vb = o_kb + dil_w
    o_qc = o_vb + dil_w
    o_kc = o_qc + 2 * WIN_GROUP * HEAD_DIM
    o_vc = o_kc + 2 * HEAD_DIM
    cols = list(range(pool_w))
    for g in range(len(DIL_PAIRS)):
        for base in (o_qb, o_kb, o_vb):
            cols += range(base + g * LANES, base + (g + 1) * LANES)
    win_heads = []
    for t in range(WIN_GROUP):
        for j in range(2):
            win_heads.append(j * WIN_GROUP + t)
            cols += range(o_qc + (j * WIN_GROUP + t) * HEAD_DIM,
                          o_qc + (j * WIN_GROUP + t + 1) * HEAD_DIM)
    cols += range(o_kc, o_kc + LANES)
    cols += range(o_vc, o_vc + LANES)
    qscale = np.ones((len(cols),), np.float32)
    for g in range(len(DIL_PAIRS)):
        qscale[pool_w + 3 * g * LANES:pool_w + (3 * g + 1) * LANES] = HEAD_DIM ** -0.5
    q0 = pool_w + 3 * len(DIL_PAIRS) * LANES
    qscale[q0:q0 + WIN_GROUP * LANES] = HEAD_DIM ** -0.5
    return np.asarray(cols), qscale, win_heads


def kernel(x, c, norm1_g, norm2_g, w_ada, b_ada, w_in, w_pool, pool_scale, sink_logit,
           w_out, w_gate, w_up, w_down, final_g):
    b, s, d = x.shape
    depth = w_in.shape[0]
    assert HEAD_DIM ** -0.5 == 0.125
    slopes = _alibi_slopes(2 * WIN_GROUP + 2 * len(DIL_PAIRS))
    slopes_win = tuple(float(v) for v in slopes[:2 * WIN_GROUP])
    slopes_dil = slopes[2 * WIN_GROUP:]
    cols, qscale, win_heads = _in_proj_columns(d)
    pool_w = d // 4
    dil_w = 2 * len(DIL_PAIRS) * HEAD_DIM
    mix_rows = np.concatenate([
        np.arange(pool_w + dil_w),
        np.concatenate([pool_w + dil_w + h * HEAD_DIM + np.arange(HEAD_DIM) for h in win_heads]),
    ])
    n_grp = len(POOL_WINDOWS)
    pg = pool_w // n_grp

    mod = _modulation(c, w_ada, b_ada).reshape(depth, b, N_MOD, d)
    tm = 512
    for l in range(depth):
        w_in_p = (w_in[l][:, cols] * qscale).astype(_BF16)
        w_out_p = w_out[l][mix_rows].astype(_BF16)
        wpool_bd = jnp.zeros((pool_w, pool_w), _F32)
        for g in range(n_grp):
            wpool_bd = wpool_bd.at[g * pg:(g + 1) * pg, g * pg:(g + 1) * pg].set(w_pool[l, g])
        wpool_bd = wpool_bd.astype(_BF16)
        sink = sink_logit[l]

        (u, q0, k0, v0, q1, k1, v1, q2, k2, v2, qc, kc, vc) = _in_proj(
            x, mod, l, norm1_g[l].reshape(1, d), w_in_p, tm)
        outs, lses = [], []
        for g, ((_, dil), (q, k, v)) in enumerate(zip(
                DIL_PAIRS, ((q0, k0, v0), (q1, k1, v1), (q2, k2, v2)))):
            sl = tuple(float(v_) for v_ in slopes_dil[2 * g:2 * g + 2])
            o, ls = _band_attn(q.reshape(b, s, LANES), k.reshape(b, s, LANES),
                               v.reshape(b, s, LANES), dil, sl)
            outs.append(o)
            lses.append(ls)
        yc = _win_attn(sink, qc, kc, vc, slopes_win, min(s, 4096))
        x = _mix_out(x, mod, l, u, wpool_bd, pool_scale[l].reshape(1, pool_w), outs, lses, yc,
                     w_out_p, tm)
        x = _ffn(x, mod, l, norm2_g[l].reshape(1, d), w_gate[l].astype(_BF16),
                 w_up[l].astype(_BF16), w_down[l].astype(_BF16), final_g.reshape(1, d),
                 l == depth - 1, tm)
    return x
```

```python
import functools
import math

import jax
import jax.numpy as jnp
import numpy as np
from jax import lax
from jax.experimental import pallas as pl
from jax.experimental.pallas import tpu as pltpu

LANES = 128
HEAD_DIM = 64
HALF = LANES // 2
POOL_WINDOWS = (2, 4, 8, 16)
POOL_HALO = 8
DIL_PAIRS = ((128, 1), (512, 4), (2048, 16))
DIL_RADIUS = 64
WIN_RADIUS = 128
WIN_GROUP = 3
N_MOD = 6
EPS = 1e-6
NEG = -1e30
QBLK = 128
BAND_UNROLL = 2
VMEM_LIMIT = 56 * 1024 * 1024

_F32 = jnp.float32
_BF16 = jnp.bfloat16


def _alibi_slopes(n):
    i = np.arange(1, n + 1, dtype=np.float32)
    return np.exp2(np.float32(-8.0) * i / np.float32(n)).astype(np.float32)


def _cparams(sem):
    return pltpu.CompilerParams(dimension_semantics=sem, vmem_limit_bytes=VMEM_LIMIT)


def _mod_kernel(c_ref, w_ref, b_ref, o_ref):
    c = c_ref[...]
    act = (c * (1.0 / (1.0 + jnp.exp(-c)))).astype(_BF16)
    w = w_ref[0].astype(_BF16)
    o_ref[0] = jnp.dot(act, w, preferred_element_type=_F32) + b_ref[0]


def _modulation(c, w_ada, b_ada):
    depth, d, nd = w_ada.shape
    b = c.shape[0]
    tn = d
    return pl.pallas_call(
        _mod_kernel,
        out_shape=jax.ShapeDtypeStruct((depth, b, nd), _F32),
        grid=(depth, nd // tn),
        in_specs=[
            pl.BlockSpec((b, d), lambda l, j: (0, 0)),
            pl.BlockSpec((1, d, tn), lambda l, j: (l, 0, j)),
            pl.BlockSpec((1, 1, tn), lambda l, j: (l, 0, j)),
        ],
        out_specs=pl.BlockSpec((1, b, tn), lambda l, j: (l, 0, j)),
        compiler_params=_cparams(("arbitrary", "arbitrary")),
        name="modulation",
    )(c, w_ada, b_ada.reshape(depth, 1, nd))


def _modulated_norm(x, g, shift, scale):
    ms = jnp.mean(x * x, axis=-1, keepdims=True)
    return (x * lax.rsqrt(ms + EPS)) * (g * (1.0 + scale)) + shift


def _in_proj_kernel(x_ref, mod_ref, g_ref, w_ref,
                    u_ref, q0_ref, k0_ref, v0_ref, q1_ref, k1_ref, v1_ref,
                    q2_ref, k2_ref, v2_ref, qc_ref, kc_ref, vc_ref, zs_ref):
    mod = mod_ref[0, 0]
    h = _modulated_norm(x_ref[0], g_ref[...], mod[0:1], mod[1:2]).astype(_BF16)
    z = jnp.dot(h, w_ref[...], preferred_element_type=_F32)
    tm = z.shape[0]

    def tile(i):
        return z[:, i * LANES:(i + 1) * LANES]

    u_ref[0] = z[:, :2 * LANES]
    q0_ref[0] = tile(2).astype(_BF16)
    k0_ref[0] = tile(3).astype(_BF16)
    v0_ref[0] = tile(4).astype(_BF16)
    for s in range(6):
        zs_ref[s] = tile(5 + s)
    for refs, (_, dil), base in (((q1_ref, k1_ref, v1_ref), DIL_PAIRS[1], 0),
                                 ((q2_ref, k2_ref, v2_ref), DIL_PAIRS[2], 3)):
        rows = tm // dil
        for j, ref in enumerate(refs):
            for r in range(dil):
                ref[0, r] = zs_ref[base + j, pl.ds(r, rows, stride=dil), :].astype(_BF16)
    for t in range(WIN_GROUP):
        qc_ref[0, t] = tile(11 + t).astype(_BF16)
    kc_ref[0] = tile(14).astype(_BF16)
    vc_ref[0] = tile(15).astype(_BF16)


def _in_proj(x, mod, l, g, w_in_p, tm):
    b, s, d = x.shape
    n_in = w_in_p.shape[1]
    d1, d2 = DIL_PAIRS[1][1], DIL_PAIRS[2][1]
    nat = lambda: pl.BlockSpec((1, tm, LANES), lambda bi, i: (bi, i, 0))
    res = lambda dil: pl.BlockSpec((1, dil, tm // dil, LANES), lambda bi, i: (bi, 0, i, 0))
    sds = jax.ShapeDtypeStruct
    out_shape = (
        [sds((b, s, 2 * LANES), _F32)]
        + [sds((b, s, LANES), _BF16)] * 3
        + [sds((b, d1, s // d1, LANES), _BF16)] * 3
        + [sds((b, d2, s // d2, LANES), _BF16)] * 3
        + [sds((b, WIN_GROUP, s, LANES), _BF16)]
        + [sds((b, s, LANES), _BF16)] * 2
    )
    out_specs = (
        [pl.BlockSpec((1, tm, 2 * LANES), lambda bi, i: (bi, i, 0))]
        + [nat() for _ in range(3)]
        + [res(d1) for _ in range(3)]
        + [res(d2) for _ in range(3)]
        + [pl.BlockSpec((1, WIN_GROUP, tm, LANES), lambda bi, i: (bi, 0, i, 0))]
        + [nat() for _ in range(2)]
    )
    return pl.pallas_call(
        _in_proj_kernel,
        out_shape=out_shape,
        grid=(b, s // tm),
        in_specs=[
            pl.BlockSpec((1, tm, d), lambda bi, i: (bi, i, 0)),
            pl.BlockSpec((1, 1, N_MOD, d), lambda bi, i: (l, bi, 0, 0)),
            pl.BlockSpec((1, d), lambda bi, i: (0, 0)),
            pl.BlockSpec((d, n_in), lambda bi, i: (0, 0)),
        ],
        out_specs=out_specs,
        scratch_shapes=[pltpu.VMEM((6, tm, LANES), _F32)],
        compiler_params=_cparams(("arbitrary", "arbitrary")),
        name="in_proj",
    )(x, mod, g, w_in_p)


def _band_bias(shape, row_heads, slopes, dist_scale, radius, offset):
    rows, cols = shape
    row = lax.broadcasted_iota(jnp.int32, shape, 0)
    col = lax.broadcasted_iota(jnp.int32, shape, 1)
    dist = jnp.abs(col - offset - (row % QBLK))
    slope = jnp.full(shape, slopes[row_heads[0]] * dist_scale, _F32)
    for i, hd in enumerate(row_heads[1:], start=1):
        slope = jnp.where(row >= i * QBLK, np.float32(slopes[hd] * dist_scale), slope)
    return jnp.where(dist <= radius, -slope * dist.astype(_F32), NEG)


def _window(i, nblk, radius, width):
    seq = nblk * QBLK
    ws = jnp.clip(i * QBLK - radius, 0, seq - width)
    variant = jnp.where(i == 0, 0, jnp.where(i == nblk - 1, 2, 1))
    return ws, variant


def _trans_b_dot(a, b):
    return lax.dot_general(a, b, (((1,), (1,)), ((), ())), preferred_element_type=_F32)


def _band_attn_kernel(q_ref, k_ref, v_ref, o_ref, lse_ref, vext_ref, bias_ref, *scratch,
                      dil, slopes):
    s_len = q_ref.shape[1]
    n = s_len // dil
    nblk = n // QBLK
    width = QBLK + 2 * DIL_RADIUS

    @pl.when(pl.program_id(0) == 0)
    def _():
        for var, off in enumerate((0, DIL_RADIUS, width - QBLK)):
            bias_ref[var] = _band_bias((2 * QBLK, width), (0, 1), slopes, float(dil),
                                       DIL_RADIUS, off)

    vext_ref[:, :LANES] = v_ref[0]
    vext_ref[:, LANES:] = jnp.ones((s_len, LANES), _BF16)

    lane = lax.broadcasted_iota(jnp.int32, (QBLK, LANES), 1)
    lo = lane < HALF
    s_bufs, e_bufs, m_bufs = scratch[0:2], scratch[2:4], scratch[4:6]
    ntot = s_len // QBLK

    def locate(blk):
        r = blk // nblk
        i = blk % nblk
        ws, variant = _window(i, nblk, DIL_RADIUS, width)
        return r, i, pl.multiple_of(r * n + ws, DIL_RADIUS), variant

    def scores(blk, par):
        _, _, kstart, variant = locate(blk)
        qstart = pl.multiple_of(blk * QBLK, QBLK)
        q = q_ref[0, pl.ds(qstart, QBLK), :]
        kw = k_ref[0, pl.ds(kstart, width), :]
        zero = jnp.zeros_like(q)
        qs = jnp.concatenate([jnp.where(lo, q, zero), jnp.where(lo, zero, q)], axis=0)
        s_bufs[par][...] = _trans_b_dot(qs, kw) + bias_ref[variant]

    def numerators(par):
        sc = s_bufs[par][...]
        mx = sc[:, :LANES]
        for c in range(1, width // LANES):
            mx = jnp.maximum(mx, sc[:, c * LANES:(c + 1) * LANES])
        m = jnp.max(mx, axis=-1, keepdims=True)
        e_bufs[par][...] = jnp.exp(sc - m).astype(_BF16)
        m_bufs[par][...] = jnp.broadcast_to(m, (2 * QBLK, LANES))

    def outputs(blk, par):
        r, i, kstart, _ = locate(blk)
        vw = vext_ref[pl.ds(kstart, width), :]
        o = jnp.dot(e_bufs[par][...], vw, preferred_element_type=_F32)
        den = o[:, LANES:]
        res = o[:, :LANES] / den
        lse_all = m_bufs[par][...] + jnp.log(den)
        out = jnp.where(lo, res[:QBLK], res[QBLK:])
        lse = jnp.where(lo, lse_all[:QBLK], lse_all[QBLK:])
        if dil == 1:
            qstart = pl.multiple_of(blk * QBLK, QBLK)
            o_ref[0, pl.ds(qstart, QBLK), :] = out.astype(_BF16)
            lse_ref[0, pl.ds(qstart, QBLK), :] = lse
        else:
            tstart = i * (QBLK * dil) + r
            scratch[6][pl.ds(tstart, QBLK, stride=dil), :] = out
            lse_ref[0, pl.ds(tstart, QBLK, stride=dil), :] = lse

    scores(0, 0)
    scores(1, 1)
    numerators(0)

    def body(p, carry):
        it = 2 * p
        scores(it, 0)
        numerators(1)
        outputs(it - 2, 0)
        scores(it + 1, 1)
        numerators(0)
        outputs(it - 1, 1)
        return carry

    lax.fori_loop(1, ntot // 2, body, 0, unroll=BAND_UNROLL)
    numerators(1)
    outputs(ntot - 2, 0)
    outputs(ntot - 1, 1)
    if dil != 1:
        o_ref[0] = scratch[6][...].astype(_BF16)


def _band_attn(q, k, v, dil, slopes):
    b, s, _ = q.shape
    width = QBLK + 2 * DIL_RADIUS
    spec = lambda: pl.BlockSpec((1, s, LANES), lambda bi: (bi, 0, 0))
    scratch = ([pltpu.VMEM((s, 2 * LANES), _BF16), pltpu.VMEM((3, 2 * QBLK, width), _F32)]
               + [pltpu.VMEM((2 * QBLK, width), _F32)] * 2
               + [pltpu.VMEM((2 * QBLK, width), _BF16)] * 2
               + [pltpu.VMEM((2 * QBLK, LANES), _F32)] * 2)
    if dil != 1:
        scratch.append(pltpu.VMEM((s, LANES), _F32))
    return pl.pallas_call(
        functools.partial(_band_attn_kernel, dil=dil, slopes=slopes),
        out_shape=[jax.ShapeDtypeStruct((b, s, LANES), _BF16),
                   jax.ShapeDtypeStruct((b, s, LANES), _F32)],
        grid=(b,),
        in_specs=[spec(), spec(), spec()],
        out_specs=[spec(), spec()],
        scratch_shapes=scratch,
        compiler_params=_cparams(("arbitrary",)),
        name=f"band_attn_d{dil}",
    )(q, k, v)


def _win_attn_kernel(sink_ref, q_ref, k_ref, v_ref, o_ref, vext_ref, bias_ref,
                     s0_ref, s1_ref, e0_ref, e1_ref, t0_ref, t1_ref, *, slopes):
    s_len = k_ref.shape[1]
    tc = q_ref.shape[2]
    nblk = s_len // QBLK
    nloc = tc // QBLK
    width = QBLK + 2 * WIN_RADIUS
    rows = WIN_GROUP * QBLK
    s_bufs, e_bufs, t_bufs = (s0_ref, s1_ref), (e0_ref, e1_ref), (t0_ref, t1_ref)

    @pl.when((pl.program_id(0) == 0) & (pl.program_id(1) == 0))
    def _():
        for var, off in enumerate((0, WIN_RADIUS, width - QBLK)):
            for j in range(2):
                heads = tuple(j * WIN_GROUP + t for t in range(WIN_GROUP))
                bias_ref[var, j] = _band_bias((rows, width), heads, slopes, 1.0,
                                              WIN_RADIUS, off)

    @pl.when(pl.program_id(1) == 0)
    def _():
        vext_ref[:, :LANES] = v_ref[0]
        vext_ref[:, LANES:] = jnp.ones((s_len, LANES), _BF16)

    lane = lax.broadcasted_iota(jnp.int32, (rows, LANES), 1)
    lo = lane < HALF
    row = lax.broadcasted_iota(jnp.int32, (rows, 1), 0)
    base = pl.program_id(1) * nloc

    def sink_col(j):
        col = jnp.full((rows, 1), sink_ref[j * WIN_GROUP], _F32)
        for t in range(1, WIN_GROUP):
            col = jnp.where(row >= t * QBLK, sink_ref[j * WIN_GROUP + t], col)
        return col

    def kstart_of(bl):
        ws, variant = _window(base + bl, nblk, WIN_RADIUS, width)
        return pl.multiple_of(ws, QBLK), variant

    def scores(bl, par):
        kstart, variant = kstart_of(bl)
        qstart = pl.multiple_of(bl * QBLK, QBLK)
        q = q_ref[0, :, pl.ds(qstart, QBLK), :].reshape(rows, LANES)
        kw = k_ref[0, pl.ds(kstart, width), :]
        zero = jnp.zeros_like(q)
        for j in range(2):
            qh = jnp.where(lo, q, zero) if j == 0 else jnp.where(lo, zero, q)
            s_bufs[par][j] = _trans_b_dot(qh, kw) + bias_ref[variant, j]

    def numerators(par):
        for j in range(2):
            sc = s_bufs[par][j]
            mx = sc[:, :LANES]
            for c in range(1, width // LANES):
                mx = jnp.maximum(mx, sc[:, c * LANES:(c + 1) * LANES])
            sink = sink_col(j)
            m = jnp.maximum(jnp.max(mx, axis=-1, keepdims=True), sink)
            e_bufs[par][j] = jnp.exp(sc - m).astype(_BF16)
            t_bufs[par][j] = jnp.broadcast_to(jnp.exp(sink - m), (rows, LANES))

    def outputs(bl, par):
        kstart, _ = kstart_of(bl)
        qstart = pl.multiple_of(bl * QBLK, QBLK)
        vw = vext_ref[pl.ds(kstart, width), :]
        res = []
        for j in range(2):
            o = jnp.dot(e_bufs[par][j], vw, preferred_element_type=_F32)
            res.append(o[:, :LANES] / (o[:, LANES:] + t_bufs[par][j]))
        out = jnp.where(lo, res[0], res[1]).astype(_BF16)
        for t in range(WIN_GROUP):
            o_ref[0, pl.ds(qstart, QBLK), t * LANES:(t + 1) * LANES] = (
                out[t * QBLK:(t + 1) * QBLK])

    scores(0, 0)
    scores(1, 1)
    numerators(0)

    def body(p, carry):
        it = 2 * p
        scores(it, 0)
        numerators(1)
        outputs(it - 2, 0)
        scores(it + 1, 1)
        numerators(0)
        outputs(it - 1, 1)
        return carry

    lax.fori_loop(1, nloc // 2, body, 0)
    numerators(1)
    outputs(nloc - 2, 0)
    outputs(nloc - 1, 1)


def _win_attn(sink, q, k, v, slopes, tc):
    b, _, s, _ = q.shape
    width = QBLK + 2 * WIN_RADIUS
    rows = WIN_GROUP * QBLK
    assert (tc // QBLK) % 2 == 0 and tc // QBLK >= 4
    return pl.pallas_call(
        functools.partial(_win_attn_kernel, slopes=slopes),
        out_shape=jax.ShapeDtypeStruct((b, s, WIN_GROUP * LANES), _BF16),
        grid=(b, s // tc),
        in_specs=[
            pl.BlockSpec(memory_space=pltpu.SMEM),
            pl.BlockSpec((1, WIN_GROUP, tc, LANES), lambda bi, i: (bi, 0, i, 0)),
            pl.BlockSpec((1, s, LANES), lambda bi, i: (bi, 0, 0)),
            pl.BlockSpec((1, s, LANES), lambda bi, i: (bi, 0, 0)),
        ],
        out_specs=pl.BlockSpec((1, tc, WIN_GROUP * LANES), lambda bi, i: (bi, i, 0)),
        scratch_shapes=[pltpu.VMEM((s, 2 * LANES), _BF16),
                        pltpu.VMEM((3, 2, rows, width), _F32)]
                       + [pltpu.VMEM((2, rows, width), _F32)] * 2
                       + [pltpu.VMEM((2, rows, width), _BF16)] * 2
                       + [pltpu.VMEM((2, rows, LANES), _F32)] * 2,
        compiler_params=_cparams(("arbitrary", "arbitrary")),
        name="win_attn",
    )(sink, q, k, v)


def _pool_mixer(up_ref, u_ref, un_ref, wp_ref, ps_ref, ubuf_ref, i, s_len):
    tm = u_ref.shape[1]
    cw = 2 * LANES
    ubuf_ref[0:POOL_HALO] = jnp.where(i > 0, up_ref[0], 0.0)
    ubuf_ref[POOL_HALO:POOL_HALO + tm] = u_ref[0]
    ubuf_ref[POOL_HALO + tm:] = jnp.where(i < pl.num_programs(1) - 1, un_ref[0], 0.0)
    lane = lax.broadcasted_iota(jnp.int32, (1, cw), 1)
    radius = jnp.full((1, cw), POOL_WINDOWS[0] // 2, jnp.int32)
    for g, w in enumerate(POOL_WINDOWS[1:], start=1):
        radius = jnp.where(lane >= g * HEAD_DIM, w // 2, radius)
    u = u_ref[0]
    wsum = u
    for k in range(1, POOL_HALO + 1):
        pair = (ubuf_ref[POOL_HALO - k:POOL_HALO - k + tm]
                + ubuf_ref[POOL_HALO + k:POOL_HALO + k + tm])
        wsum = wsum + jnp.where(radius >= k, pair, 0.0)
    t = i * tm + lax.broadcasted_iota(jnp.int32, (tm, cw), 0)
    cnt = jnp.minimum(t + radius + 1, s_len) - jnp.maximum(t - radius, 0)
    pooled = (wsum / cnt.astype(_F32) - u).astype(_BF16)
    return jnp.dot(pooled, wp_ref[...], preferred_element_type=_F32) * ps_ref[...]


def _token_kernel(x_ref, mod_ref, up_ref, u_ref, un_ref, wp_ref, ps_ref,
                  o0_ref, o1_ref, o2_ref, l0_ref, l1_ref, l2_ref, yc_ref, wo_ref,
                  g2_ref, wg_ref, wu_ref, wd_ref, fg_ref,
                  out_ref, ubuf_ref, mix_ref, *, s_len, final, ff_chunks):
    cw = 2 * LANES
    mod = mod_ref[0, 0]
    ya = _pool_mixer(up_ref, u_ref, un_ref, wp_ref, ps_ref, ubuf_ref, pl.program_id(1), s_len)
    mix_ref[:, :cw] = ya.astype(_BF16)

    lses = [l0_ref[0], l1_ref[0], l2_ref[0]]
    mx = jnp.maximum(jnp.maximum(lses[0], lses[1]), lses[2])
    es = [jnp.exp(ls - mx) for ls in lses]
    den = es[0] + es[1] + es[2]
    for g, o_ref in enumerate((o0_ref, o1_ref, o2_ref)):
        yb = (es[g] / den) * o_ref[0]
        mix_ref[:, cw + g * LANES:cw + (g + 1) * LANES] = yb.astype(_BF16)
    mix_ref[:, cw + 3 * LANES:] = yc_ref[0]

    x1 = x_ref[0] + mod[2:3] * jnp.dot(mix_ref[...], wo_ref[...], preferred_element_type=_F32)

    h = _modulated_norm(x1, g2_ref[...], mod[3:4], mod[4:5]).astype(_BF16)
    ffn = None
    for c0, c1 in ff_chunks:
        gate = jnp.dot(h, wg_ref[:, c0:c1], preferred_element_type=_F32)
        up = jnp.dot(h, wu_ref[:, c0:c1], preferred_element_type=_F32)
        act = ((gate * (1.0 / (1.0 + jnp.exp(-gate)))) * up).astype(_BF16)
        part = jnp.dot(act, wd_ref[c0:c1, :], preferred_element_type=_F32)
        ffn = part if ffn is None else ffn + part
    y = x1 + mod[5:6] * ffn
    if final:
        ms = jnp.mean(y * y, axis=-1, keepdims=True)
        y = (y * lax.rsqrt(ms + EPS)) * fg_ref[...]
    out_ref[0] = y


def _ff_chunks(dff):
    mxu_k = 2 * LANES
    cut = (dff // 2 + mxu_k - 1) // mxu_k * mxu_k
    return ((0, cut), (cut, dff)) if 0 < cut < dff else ((0, dff),)


def _token_mixers_out(x, mod, l, u, wpool_bd, pool_scale, outs, lses, yc, w_out_p,
                      g2, wg, wu, wd, final_g, final, tm):
    b, s, d = x.shape
    dff = wg.shape[1]
    cw = 2 * LANES
    hb = tm // POOL_HALO
    nhb = s // POOL_HALO
    tok = lambda width: pl.BlockSpec((1, tm, width), lambda bi, i: (bi, i, 0))
    const = lambda shape: pl.BlockSpec(shape, lambda bi, i: (0,) * len(shape),
                                       pipeline_mode=pl.Buffered(1))
    return pl.pallas_call(
        functools.partial(_token_kernel, s_len=s, final=final, ff_chunks=_ff_chunks(dff)),
        out_shape=jax.ShapeDtypeStruct((b, s, d), _F32),
        grid=(b, s // tm),
        in_specs=[
            tok(d),
            pl.BlockSpec((1, 1, N_MOD, d), lambda bi, i: (l, bi, 0, 0)),
            pl.BlockSpec((1, POOL_HALO, cw), lambda bi, i: (bi, jnp.maximum(i * hb - 1, 0), 0)),
            tok(cw),
            pl.BlockSpec((1, POOL_HALO, cw),
                         lambda bi, i: (bi, jnp.minimum((i + 1) * hb, nhb - 1), 0)),
            const((cw, cw)),
            const((1, cw)),
            tok(LANES), tok(LANES), tok(LANES),
            tok(LANES), tok(LANES), tok(LANES),
            tok(WIN_GROUP * LANES),
            const((d, d)),
            const((1, d)),
            const((d, dff)), const((d, dff)), const((dff, d)),
            const((1, d)),
        ],
        out_specs=tok(d),
        scratch_shapes=[pltpu.VMEM((tm + 2 * POOL_HALO, cw), _F32),
                        pltpu.VMEM((tm, d), _BF16)],
        compiler_params=_cparams(("arbitrary", "arbitrary")),
        name="token_mix_ffn",
    )(x, mod, u, u, u, wpool_bd, pool_scale, *outs, *lses, yc, w_out_p,
      g2, wg, wu, wd, final_g)


def _in_proj_columns(d_model):
    pool_w = d_model // 4
    dil_w = 2 * len(DIL_PAIRS) * HEAD_DIM
    o_qb = pool_w
    o_kb = o_qb + dil_w
    o_vb = o_kb + dil_w
    o_qc = o_vb + dil_w
    o_kc = o_qc + 2 * WIN_GROUP * HEAD_DIM
    o_vc = o_kc + 2 * HEAD_DIM
    cols = list(range(pool_w))
    for g in range(len(DIL_PAIRS)):
        for base in (o_qb, o_kb, o_vb):
            cols += range(base + g * LANES, base + (g + 1) * LANES)
    win_heads = []
    for t in range(WIN_GROUP):
        for j in range(2):
            win_heads.append(j * WIN_GROUP + t)
            cols += range(o_qc + (j * WIN_GROUP + t) * HEAD_DIM,
                          o_qc + (j * WIN_GROUP + t + 1) * HEAD_DIM)
    cols += range(o_kc, o_kc + LANES)
    cols += range(o_vc, o_vc + LANES)
    qscale = np.ones((len(cols),), np.float32)
    for g in range(len(DIL_PAIRS)):
        qscale[pool_w + 3 * g * LANES:pool_w + (3 * g + 1) * LANES] = HEAD_DIM ** -0.5
    q0 = pool_w + 3 * len(DIL_PAIRS) * LANES
    qscale[q0:q0 + WIN_GROUP * LANES] = HEAD_DIM ** -0.5
    return np.asarray(cols), qscale, win_heads


def _take_runs(w, idx, axis):
    idx = np.asarray(idx)
    cuts = [0] + [i for i in range(1, len(idx)) if idx[i] != idx[i - 1] + 1] + [len(idx)]
    parts = [lax.slice_in_dim(w, int(idx[a]), int(idx[b - 1]) + 1, axis=axis)
             for a, b in zip(cuts[:-1], cuts[1:])]
    return jnp.concatenate(parts, axis=axis)


def kernel(x, c, norm1_g, norm2_g, w_ada, b_ada, w_in, w_pool, pool_scale, sink_logit,
           w_out, w_gate, w_up, w_down, final_g):
    b, s, d = x.shape
    depth = w_in.shape[0]
    assert HEAD_DIM ** -0.5 == 0.125
    slopes = _alibi_slopes(2 * WIN_GROUP + 2 * len(DIL_PAIRS))
    slopes_win = tuple(float(v) for v in slopes[:2 * WIN_GROUP])
    slopes_dil = slopes[2 * WIN_GROUP:]
    cols, qscale, win_heads = _in_proj_columns(d)
    pool_w = d // 4
    dil_w = 2 * len(DIL_PAIRS) * HEAD_DIM
    mix_rows = np.concatenate([
        np.arange(pool_w + dil_w),
        np.concatenate([pool_w + dil_w + h * HEAD_DIM + np.arange(HEAD_DIM) for h in win_heads]),
    ])
    n_grp = len(POOL_WINDOWS)
    pg = pool_w // n_grp

    mod = _modulation(c, w_ada, b_ada).reshape(depth, b, N_MOD, d)
    tm = 512
    for l in range(depth):
        w_in_p = (_take_runs(w_in[l], cols, 1) * qscale).astype(_BF16)
        w_out_p = _take_runs(w_out[l], mix_rows, 0).astype(_BF16)
        zpad = lambda n: jnp.zeros((pg, n * pg), _F32)
        wpool_bd = jnp.concatenate(
            [jnp.concatenate([zpad(g), w_pool[l, g], zpad(n_grp - 1 - g)], axis=1)
             for g in range(n_grp)], axis=0).astype(_BF16)
        sink = sink_logit[l]

        (u, q0, k0, v0, q1, k1, v1, q2, k2, v2, qc, kc, vc) = _in_proj(
            x, mod, l, norm1_g[l].reshape(1, d), w_in_p, tm)
        outs, lses = [], []
        for g, ((_, dil), (q, k, v)) in enumerate(zip(
                DIL_PAIRS, ((q0, k0, v0), (q1, k1, v1), (q2, k2, v2)))):
            sl = tuple(float(v_) for v_ in slopes_dil[2 * g:2 * g + 2])
            o, ls = _band_attn(q.reshape(b, s, LANES), k.reshape(b, s, LANES),
                               v.reshape(b, s, LANES), dil, sl)
            outs.append(o)
            lses.append(ls)
        yc = _win_attn(sink, qc, kc, vc, slopes_win, min(s, 4096))
        x = _token_mixers_out(
            x, mod, l, u, wpool_bd, pool_scale[l].reshape(1, pool_w), outs, lses, yc, w_out_p,
            norm2_g[l].reshape(1, d), w_gate[l].astype(_BF16), w_up[l].astype(_BF16),
            w_down[l].astype(_BF16), final_g.reshape(1, d), l == depth - 1, tm)
    return x
```

```python
import functools
import math

import jax
import jax.numpy as jnp
import numpy as np
from jax import lax
from jax.experimental import pallas as pl
from jax.experimental.pallas import tpu as pltpu

LANES = 128
HEAD_DIM = 64
HALF = LANES // 2
POOL_WINDOWS = (2, 4, 8, 16)
POOL_HALO = 8
DIL_PAIRS = ((128, 1), (512, 4), (2048, 16))
DIL_RADIUS = 64
WIN_RADIUS = 128
WIN_GROUP = 3
N_MOD = 6
EPS = 1e-6
NEG = -1e30
QBLK = 128
BAND_UNROLL = 2
VMEM_LIMIT = 56 * 1024 * 1024

_F32 = jnp.float32
_BF16 = jnp.bfloat16


def _alibi_slopes(n):
    i = np.arange(1, n + 1, dtype=np.float32)
    return np.exp2(np.float32(-8.0) * i / np.float32(n)).astype(np.float32)


def _cparams(sem):
    return pltpu.CompilerParams(dimension_semantics=sem, vmem_limit_bytes=VMEM_LIMIT)


def _mod_kernel(c_ref, w_ref, b_ref, o_ref):
    c = c_ref[...]
    act = (c * (1.0 / (1.0 + jnp.exp(-c)))).astype(_BF16)
    w = w_ref[0].astype(_BF16)
    o_ref[0] = jnp.dot(act, w, preferred_element_type=_F32) + b_ref[0]


def _modulation(c, w_ada, b_ada):
    depth, d, nd = w_ada.shape
    b = c.shape[0]
    tn = d
    return pl.pallas_call(
        _mod_kernel,
        out_shape=jax.ShapeDtypeStruct((depth, b, nd), _F32),
        grid=(depth, nd // tn),
        in_specs=[
            pl.BlockSpec((b, d), lambda l, j: (0, 0)),
            pl.BlockSpec((1, d, tn), lambda l, j: (l, 0, j)),
            pl.BlockSpec((1, 1, tn), lambda l, j: (l, 0, j)),
        ],
        out_specs=pl.BlockSpec((1, b, tn), lambda l, j: (l, 0, j)),
        compiler_params=_cparams(("arbitrary", "arbitrary")),
        name="modulation",
    )(c, w_ada, b_ada.reshape(depth, 1, nd))


def _modulated_norm(x, g, shift, scale):
    ms = jnp.mean(x * x, axis=-1, keepdims=True)
    return (x * lax.rsqrt(ms + EPS)) * (g * (1.0 + scale)) + shift


def _pooled_tokens(ubuf_ref, t0, s_len):
    tm = ubuf_ref.shape[0] - 2 * POOL_HALO
    cols = []
    for c in range(2):
        r_lo, r_hi = POOL_WINDOWS[2 * c] // 2, POOL_WINDOWS[2 * c + 1] // 2
        lanes = slice(c * LANES, (c + 1) * LANES)
        lane = lax.broadcasted_iota(jnp.int32, (1, LANES), 1)
        radius = jnp.where(lane < HALF, r_lo, r_hi)
        u = ubuf_ref[POOL_HALO:POOL_HALO + tm, lanes]
        wsum = u
        for k in range(1, r_hi + 1):
            pair = (ubuf_ref[POOL_HALO - k:POOL_HALO - k + tm, lanes]
                    + ubuf_ref[POOL_HALO + k:POOL_HALO + k + tm, lanes])
            wsum = wsum + (pair if k <= r_lo else jnp.where(lane < HALF, 0.0, pair))
        t = t0 + lax.broadcasted_iota(jnp.int32, (tm, LANES), 0)
        cnt = jnp.minimum(t + radius + 1, s_len) - jnp.maximum(t - radius, 0)
        cols.append((wsum / cnt.astype(_F32) - u).astype(_BF16))
    return jnp.concatenate(cols, axis=1)


def _in_proj_kernel(x_ref, xp_ref, xn_ref, mod_ref, g_ref, w_ref, wp_ref, ps_ref,
                    ya_ref, q0_ref, k0_ref, v0_ref, q1_ref, k1_ref, v1_ref,
                    q2_ref, k2_ref, v2_ref, qc_ref, kc_ref, vc_ref, zs_ref, ubuf_ref,
                    *, s_len):
    tm = x_ref.shape[1]
    i = pl.program_id(1)
    cw = 2 * LANES
    mod = mod_ref[0, 0]
    norm = lambda rows: _modulated_norm(rows, g_ref[...], mod[0:1], mod[1:2]).astype(_BF16)
    h = norm(x_ref[0])

    def proj(lhs, t0, t1):
        return jnp.dot(lhs, w_ref[:, t0 * LANES:t1 * LANES], preferred_element_type=_F32)

    halo = proj(norm(jnp.concatenate([xp_ref[0], xn_ref[0]], axis=0)), 0, 2)
    ubuf_ref[0:POOL_HALO] = jnp.where(i > 0, halo[:POOL_HALO], 0.0)
    ubuf_ref[POOL_HALO:POOL_HALO + tm] = proj(h, 0, 2)
    ubuf_ref[POOL_HALO + tm:] = jnp.where(i < pl.num_programs(1) - 1, halo[POOL_HALO:], 0.0)
    pooled = _pooled_tokens(ubuf_ref, i * tm, s_len)

    z_lo, z_hi = proj(h, 2, 8), proj(h, 8, 16)

    def tile(i):
        z, j = (z_lo, i - 2) if i < 8 else (z_hi, i - 8)
        return z[:, j * LANES:(j + 1) * LANES]

    for j, ref in enumerate((q0_ref, k0_ref, v0_ref)):
        ref[0] = tile(2 + j).astype(_BF16)
    for refs, (_, dil), base in (((q1_ref, k1_ref, v1_ref), DIL_PAIRS[1], 5),
                                 ((q2_ref, k2_ref, v2_ref), DIL_PAIRS[2], 8)):
        rows = tm // dil
        for j, ref in enumerate(refs):
            zs_ref[base - 5 + j] = tile(base + j)
            for r in range(dil):
                ref[0, r] = zs_ref[base - 5 + j, pl.ds(r, rows, stride=dil), :].astype(_BF16)
    for t in range(WIN_GROUP):
        qc_ref[0, t] = tile(11 + t).astype(_BF16)
    kc_ref[0] = tile(14).astype(_BF16)
    vc_ref[0] = tile(15).astype(_BF16)
    ya = jnp.dot(pooled, wp_ref[...], preferred_element_type=_F32) * ps_ref[...]
    ya_ref[0] = ya.astype(_BF16)


def _in_proj(x, mod, l, g, w_in_p, wpool_bd, pool_scale, tm):
    b, s, d = x.shape
    n_in = w_in_p.shape[1]
    cw = 2 * LANES
    hb = tm // POOL_HALO
    nhb = s // POOL_HALO
    d1, d2 = DIL_PAIRS[1][1], DIL_PAIRS[2][1]
    nat = lambda: pl.BlockSpec((1, tm, LANES), lambda bi, i: (bi, i, 0))
    res = lambda dil: pl.BlockSpec((1, dil, tm // dil, LANES), lambda bi, i: (bi, 0, i, 0))
    full = lambda shape: pl.BlockSpec(shape, lambda bi, i: (0,) * len(shape))
    sds = jax.ShapeDtypeStruct
    out_shape = (
        [sds((b, s, cw), _BF16)]
        + [sds((b, s, LANES), _BF16)] * 3
        + [sds((b, d1, s // d1, LANES), _BF16)] * 3
        + [sds((b, d2, s // d2, LANES), _BF16)] * 3
        + [sds((b, WIN_GROUP, s, LANES), _BF16)]
        + [sds((b, s, LANES), _BF16)] * 2
    )
    out_specs = (
        [pl.BlockSpec((1, tm, 2 * LANES), lambda bi, i: (bi, i, 0))]
        + [nat() for _ in range(3)]
        + [res(d1) for _ in range(3)]
        + [res(d2) for _ in range(3)]
        + [pl.BlockSpec((1, WIN_GROUP, tm, LANES), lambda bi, i: (bi, 0, i, 0))]
        + [nat() for _ in range(2)]
    )
    return pl.pallas_call(
        functools.partial(_in_proj_kernel, s_len=s),
        out_shape=out_shape,
        grid=(b, s // tm),
        in_specs=[
            pl.BlockSpec((1, tm, d), lambda bi, i: (bi, i, 0)),
            pl.BlockSpec((1, POOL_HALO, d), lambda bi, i: (bi, jnp.maximum(i * hb - 1, 0), 0)),
            pl.BlockSpec((1, POOL_HALO, d),
                         lambda bi, i: (bi, jnp.minimum((i + 1) * hb, nhb - 1), 0)),
            pl.BlockSpec((1, 1, N_MOD, d), lambda bi, i: (l, bi, 0, 0)),
            full((1, d)),
            full((d, n_in)),
            full((cw, cw)),
            full((1, cw)),
        ],
        out_specs=out_specs,
        scratch_shapes=[pltpu.VMEM((6, tm, LANES), _F32),
                        pltpu.VMEM((tm + 2 * POOL_HALO, cw), _F32)],
        compiler_params=_cparams(("arbitrary", "arbitrary")),
        name="in_proj",
    )(x, x, x, mod, g, w_in_p, wpool_bd, pool_scale)


def _band_bias(shape, row_heads, slopes, dist_scale, radius, offset):
    rows, cols = shape
    row = lax.broadcasted_iota(jnp.int32, shape, 0)
    col = lax.broadcasted_iota(jnp.int32, shape, 1)
    dist = jnp.abs(col - offset - (row % QBLK))
    slope = jnp.full(shape, slopes[row_heads[0]] * dist_scale, _F32)
    for i, hd in enumerate(row_heads[1:], start=1):
        slope = jnp.where(row >= i * QBLK, np.float32(slopes[hd] * dist_scale), slope)
    return jnp.where(dist <= radius, -slope * dist.astype(_F32), NEG)


def _window(i, nblk, radius, width):
    seq = nblk * QBLK
    ws = jnp.clip(i * QBLK - radius, 0, seq - width)
    variant = jnp.where(i == 0, 0, jnp.where(i == nblk - 1, 2, 1))
    return ws, variant


def _trans_b_dot(a, b):
    return lax.dot_general(a, b, (((1,), (1,)), ((), ())), preferred_element_type=_F32)


def _band_attn_kernel(q_ref, k_ref, v_ref, o_ref, lse_ref, vext_ref, bias_ref, *scratch,
                      dil, slopes):
    s_len = q_ref.shape[1]
    n = s_len // dil
    nblk = n // QBLK
    width = QBLK + 2 * DIL_RADIUS

    @pl.when(pl.program_id(0) == 0)
    def _():
        for var, off in enumerate((0, DIL_RADIUS, width - QBLK)):
            bias_ref[var] = _band_bias((2 * QBLK, width), (0, 1), slopes, float(dil),
                                       DIL_RADIUS, off)

    vext_ref[:, :LANES] = v_ref[0]
    vext_ref[:, LANES:] = jnp.ones((s_len, LANES), _BF16)

    lane = lax.broadcasted_iota(jnp.int32, (QBLK, LANES), 1)
    lo = lane < HALF
    s_bufs, e_bufs, m_bufs = scratch[0:2], scratch[2:4], scratch[4:6]
    ntot = s_len // QBLK

    def locate(blk):
        r = blk // nblk
        i = blk % nblk
        ws, variant = _window(i, nblk, DIL_RADIUS, width)
        return r, i, pl.multiple_of(r * n + ws, DIL_RADIUS), variant

    def scores(blk, par):
        _, _, kstart, variant = locate(blk)
        qstart = pl.multiple_of(blk * QBLK, QBLK)
        q = q_ref[0, pl.ds(qstart, QBLK), :]
        kw = k_ref[0, pl.ds(kstart, width), :]
        zero = jnp.zeros_like(q)
        qs = jnp.concatenate([jnp.where(lo, q, zero), jnp.where(lo, zero, q)], axis=0)
        s_bufs[par][...] = _trans_b_dot(qs, kw) + bias_ref[variant]

    def numerators(par):
        sc = s_bufs[par][...]
        mx = sc[:, :LANES]
        for c in range(1, width // LANES):
            mx = jnp.maximum(mx, sc[:, c * LANES:(c + 1) * LANES])
        m = jnp.max(mx, axis=-1, keepdims=True)
        e_bufs[par][...] = jnp.exp(sc - m).astype(_BF16)
        m_bufs[par][...] = jnp.broadcast_to(m, (2 * QBLK, LANES))

    def outputs(blk, par):
        r, i, kstart, _ = locate(blk)
        vw = vext_ref[pl.ds(kstart, width), :]
        o = jnp.dot(e_bufs[par][...], vw, preferred_element_type=_F32)
        den = o[:, LANES:]
        res = o[:, :LANES] / den
        lse_all = m_bufs[par][...] + jnp.log(den)
        out = jnp.where(lo, res[:QBLK], res[QBLK:])
        lse = jnp.where(lo, lse_all[:QBLK], lse_all[QBLK:])
        if dil == 1:
            qstart = pl.multiple_of(blk * QBLK, QBLK)
            o_ref[0, pl.ds(qstart, QBLK), :] = out.astype(_BF16)
            lse_ref[0, pl.ds(qstart, QBLK), :] = lse
        else:
            tstart = i * (QBLK * dil) + r
            scratch[6][pl.ds(tstart, QBLK, stride=dil), :] = out
            lse_ref[0, pl.ds(tstart, QBLK, stride=dil), :] = lse

    scores(0, 0)
    scores(1, 1)
    numerators(0)

    def body(p, carry):
        it = 2 * p
        scores(it, 0)
        numerators(1)
        outputs(it - 2, 0)
        scores(it + 1, 1)
        numerators(0)
        outputs(it - 1, 1)
        return carry

    lax.fori_loop(1, ntot // 2, body, 0, unroll=BAND_UNROLL)
    numerators(1)
    outputs(ntot - 2, 0)
    outputs(ntot - 1, 1)
    if dil != 1:
        o_ref[0] = scratch[6][...].astype(_BF16)


def _band_attn(q, k, v, dil, slopes):
    b, s, _ = q.shape
    width = QBLK + 2 * DIL_RADIUS
    spec = lambda: pl.BlockSpec((1, s, LANES), lambda bi: (bi, 0, 0))
    scratch = ([pltpu.VMEM((s, 2 * LANES), _BF16), pltpu.VMEM((3, 2 * QBLK, width), _F32)]
               + [pltpu.VMEM((2 * QBLK, width), _F32)] * 2
               + [pltpu.VMEM((2 * QBLK, width), _BF16)] * 2
               + [pltpu.VMEM((2 * QBLK, LANES), _F32)] * 2)
    if dil != 1:
        scratch.append(pltpu.VMEM((s, LANES), _F32))
    return pl.pallas_call(
        functools.partial(_band_attn_kernel, dil=dil, slopes=slopes),
        out_shape=[jax.ShapeDtypeStruct((b, s, LANES), _BF16),
                   jax.ShapeDtypeStruct((b, s, LANES), _F32)],
        grid=(b,),
        in_specs=[spec(), spec(), spec()],
        out_specs=[spec(), spec()],
        scratch_shapes=scratch,
        compiler_params=_cparams(("arbitrary",)),
        name=f"band_attn_d{dil}",
    )(q, k, v)


def _win_attn_kernel(sink_ref, q_ref, k_ref, v_ref, o_ref, vext_ref, bias_ref,
                     s0_ref, s1_ref, e0_ref, e1_ref, t0_ref, t1_ref, *, slopes):
    s_len = k_ref.shape[1]
    tc = q_ref.shape[2]
    nblk = s_len // QBLK
    nloc = tc // QBLK
    width = QBLK + 2 * WIN_RADIUS
    rows = WIN_GROUP * QBLK
    s_bufs, e_bufs, t_bufs = (s0_ref, s1_ref), (e0_ref, e1_ref), (t0_ref, t1_ref)

    @pl.when((pl.program_id(0) == 0) & (pl.program_id(1) == 0))
    def _():
        for var, off in enumerate((0, WIN_RADIUS, width - QBLK)):
            for j in range(2):
                heads = tuple(j * WIN_GROUP + t for t in range(WIN_GROUP))
                bias_ref[var, j] = _band_bias((rows, width), heads, slopes, 1.0,
                                              WIN_RADIUS, off)

    @pl.when(pl.program_id(1) == 0)
    def _():
        vext_ref[:, :LANES] = v_ref[0]
        vext_ref[:, LANES:] = jnp.ones((s_len, LANES), _BF16)

    lane = lax.broadcasted_iota(jnp.int32, (rows, LANES), 1)
    lo = lane < HALF
    row = lax.broadcasted_iota(jnp.int32, (rows, 1), 0)
    base = pl.program_id(1) * nloc

    def sink_col(j):
        col = jnp.full((rows, 1), sink_ref[j * WIN_GROUP], _F32)
        for t in range(1, WIN_GROUP):
            col = jnp.where(row >= t * QBLK, sink_ref[j * WIN_GROUP + t], col)
        return col

    def kstart_of(bl):
        ws, variant = _window(base + bl, nblk, WIN_RADIUS, width)
        return pl.multiple_of(ws, QBLK), variant

    def scores(bl, par):
        kstart, variant = kstart_of(bl)
        qstart = pl.multiple_of(bl * QBLK, QBLK)
        q = q_ref[0, :, pl.ds(qstart, QBLK), :].reshape(rows, LANES)
        kw = k_ref[0, pl.ds(kstart, width), :]
        zero = jnp.zeros_like(q)
        for j in range(2):
            qh = jnp.where(lo, q, zero) if j == 0 else jnp.where(lo, zero, q)
            s_bufs[par][j] = _trans_b_dot(qh, kw) + bias_ref[variant, j]

    def numerators(par):
        for j in range(2):
            sc = s_bufs[par][j]
            mx = sc[:, :LANES]
            for c in range(1, width // LANES):
                mx = jnp.maximum(mx, sc[:, c * LANES:(c + 1) * LANES])
            sink = sink_col(j)
            m = jnp.maximum(jnp.max(mx, axis=-1, keepdims=True), sink)
            e_bufs[par][j] = jnp.exp(sc - m).astype(_BF16)
            t_bufs[par][j] = jnp.broadcast_to(jnp.exp(sink - m), (rows, LANES))

    def outputs(bl, par):
        kstart, _ = kstart_of(bl)
        qstart = pl.multiple_of(bl * QBLK, QBLK)
        vw = vext_ref[pl.ds(kstart, width), :]
        res = []
        for j in range(2):
            o = jnp.dot(e_bufs[par][j], vw, preferred_element_type=_F32)
            res.append(o[:, :LANES] / (o[:, LANES:] + t_bufs[par][j]))
        out = jnp.where(lo, res[0], res[1]).astype(_BF16)
        for t in range(WIN_GROUP):
            o_ref[0, pl.ds(qstart, QBLK), t * LANES:(t + 1) * LANES] = (
                out[t * QBLK:(t + 1) * QBLK])

    scores(0, 0)
    scores(1, 1)
    numerators(0)

    def body(p, carry):
        it = 2 * p
        scores(it, 0)
        numerators(1)
        outputs(it - 2, 0)
        scores(it + 1, 1)
        numerators(0)
        outputs(it - 1, 1)
        return carry

    lax.fori_loop(1, nloc // 2, body, 0)
    numerators(1)
    outputs(nloc - 2, 0)
    outputs(nloc - 1, 1)


def _win_attn(sink, q, k, v, slopes, tc):
    b, _, s, _ = q.shape
    width = QBLK + 2 * WIN_RADIUS
    rows = WIN_GROUP * QBLK
    assert (tc // QBLK) % 2 == 0 and tc // QBLK >= 4
    return pl.pallas_call(
        functools.partial(_win_attn_kernel, slopes=slopes),
        out_shape=jax.ShapeDtypeStruct((b, s, WIN_GROUP * LANES), _BF16),
        grid=(b, s // tc),
        in_specs=[
            pl.BlockSpec(memory_space=pltpu.SMEM),
            pl.BlockSpec((1, WIN_GROUP, tc, LANES), lambda bi, i: (bi, 0, i, 0)),
            pl.BlockSpec((1, s, LANES), lambda bi, i: (bi, 0, 0)),
            pl.BlockSpec((1, s, LANES), lambda bi, i: (bi, 0, 0)),
        ],
        out_specs=pl.BlockSpec((1, tc, WIN_GROUP * LANES), lambda bi, i: (bi, i, 0)),
        scratch_shapes=[pltpu.VMEM((s, 2 * LANES), _BF16),
                        pltpu.VMEM((3, 2, rows, width), _F32)]
                       + [pltpu.VMEM((2, rows, width), _F32)] * 2
                       + [pltpu.VMEM((2, rows, width), _BF16)] * 2
                       + [pltpu.VMEM((2, rows, LANES), _F32)] * 2,
        compiler_params=_cparams(("arbitrary", "arbitrary")),
        name="win_attn",
    )(sink, q, k, v)


def _token_kernel(x_ref, mod_ref, ya_ref,
                  o0_ref, o1_ref, o2_ref, l0_ref, l1_ref, l2_ref, yc_ref, wo_ref,
                  g2_ref, wg_ref, wu_ref, wd_ref, fg_ref,
                  out_ref, mix_ref, *, final, ff_chunks):
    cw = 2 * LANES
    mod = mod_ref[0, 0]
    mix_ref[:, :cw] = ya_ref[0]

    lses = [l0_ref[0], l1_ref[0], l2_ref[0]]
    mx = jnp.maximum(jnp.maximum(lses[0], lses[1]), lses[2])
    es = [jnp.exp(ls - mx) for ls in lses]
    den = es[0] + es[1] + es[2]
    for g, o_ref in enumerate((o0_ref, o1_ref, o2_ref)):
        yb = (es[g] / den) * o_ref[0]
        mix_ref[:, cw + g * LANES:cw + (g + 1) * LANES] = yb.astype(_BF16)
    mix_ref[:, cw + 3 * LANES:] = yc_ref[0]

    x1 = x_ref[0] + mod[2:3] * jnp.dot(mix_ref[...], wo_ref[...], preferred_element_type=_F32)

    h = _modulated_norm(x1, g2_ref[...], mod[3:4], mod[4:5]).astype(_BF16)
    ffn = None
    for c0, c1 in ff_chunks:
        gate = jnp.dot(h, wg_ref[:, c0:c1], preferred_element_type=_F32)
        up = jnp.dot(h, wu_ref[:, c0:c1], preferred_element_type=_F32)
        act = ((gate * (1.0 / (1.0 + jnp.exp(-gate)))) * up).astype(_BF16)
        part = jnp.dot(act, wd_ref[c0:c1, :], preferred_element_type=_F32)
        ffn = part if ffn is None else ffn + part
    y = x1 + mod[5:6] * ffn
    if final:
        ms = jnp.mean(y * y, axis=-1, keepdims=True)
        y = (y * lax.rsqrt(ms + EPS)) * fg_ref[...]
    out_ref[0] = y


def _ff_chunks(dff):
    mxu_k = 2 * LANES
    cut = (dff // 2 + mxu_k - 1) // mxu_k * mxu_k
    return ((0, cut), (cut, dff)) if 0 < cut < dff else ((0, dff),)


def _token_mixers_out(x, mod, l, ya, outs, lses, yc, w_out_p,
                      g2, wg, wu, wd, final_g, final, tm):
    b, s, d = x.shape
    dff = wg.shape[1]
    cw = 2 * LANES
    tok = lambda width: pl.BlockSpec((1, tm, width), lambda bi, i: (bi, i, 0))
    const = lambda shape: pl.BlockSpec(shape, lambda bi, i: (0,) * len(shape),
                                       pipeline_mode=pl.Buffered(1))
    return pl.pallas_call(
        functools.partial(_token_kernel, final=final, ff_chunks=_ff_chunks(dff)),
        out_shape=jax.ShapeDtypeStruct((b, s, d), _F32),
        grid=(b, s // tm),
        in_specs=[
            tok(d),
            pl.BlockSpec((1, 1, N_MOD, d), lambda bi, i: (l, bi, 0, 0)),
            tok(cw),
            tok(LANES), tok(LANES), tok(LANES),
            tok(LANES), tok(LANES), tok(LANES),
            tok(WIN_GROUP * LANES),
            const((d, d)),
            const((1, d)),
            const((d, dff)), const((d, dff)), const((dff, d)),
            const((1, d)),
        ],
        out_specs=tok(d),
        scratch_shapes=[pltpu.VMEM((tm, d), _BF16)],
        compiler_params=_cparams(("arbitrary", "arbitrary")),
        name="token_mix_ffn",
    )(x, mod, ya, *outs, *lses, yc, w_out_p, g2, wg, wu, wd, final_g)


def _in_proj_columns(d_model):
    pool_w = d_model // 4
    dil_w = 2 * len(DIL_PAIRS) * HEAD_DIM
    o_qb = pool_w
    o_kb = o_qb + dil_w
    o_vb = o_kb + dil_w
    o_qc = o_vb + dil_w
    o_kc = o_qc + 2 * WIN_GROUP * HEAD_DIM
    o_vc = o_kc + 2 * HEAD_DIM
    cols = list(range(pool_w))
    for g in range(len(DIL_PAIRS)):
        for base in (o_qb, o_kb, o_vb):
            cols += range(base + g * LANES, base + (g + 1) * LANES)
    win_heads = []
    for t in range(WIN_GROUP):
        for j in range(2):
            win_heads.append(j * WIN_GROUP + t)
            cols += range(o_qc + (j * WIN_GROUP + t) * HEAD_DIM,
                          o_qc + (j * WIN_GROUP + t + 1) * HEAD_DIM)
    cols += range(o_kc, o_kc + LANES)
    cols += range(o_vc, o_vc + LANES)
    qscale = np.ones((len(cols),), np.float32)
    for g in range(len(DIL_PAIRS)):
        qscale[pool_w + 3 * g * LANES:pool_w + (3 * g + 1) * LANES] = HEAD_DIM ** -0.5
    q0 = pool_w + 3 * len(DIL_PAIRS) * LANES
    qscale[q0:q0 + WIN_GROUP * LANES] = HEAD_DIM ** -0.5
    return np.asarray(cols), qscale, win_heads


def _take_runs(w, idx, axis):
    idx = np.asarray(idx)
    cuts = [0] + [i for i in range(1, len(idx)) if idx[i] != idx[i - 1] + 1] + [len(idx)]
    parts = [lax.slice_in_dim(w, int(idx[a]), int(idx[b - 1]) + 1, axis=axis)
             for a, b in zip(cuts[:-1], cuts[1:])]
    return jnp.concatenate(parts, axis=axis)


def kernel(x, c, norm1_g, norm2_g, w_ada, b_ada, w_in, w_pool, pool_scale, sink_logit,
           w_out, w_gate, w_up, w_down, final_g):
    b, s, d = x.shape
    depth = w_in.shape[0]
    assert HEAD_DIM ** -0.5 == 0.125
    slopes = _alibi_slopes(2 * WIN_GROUP + 2 * len(DIL_PAIRS))
    slopes_win = tuple(float(v) for v in slopes[:2 * WIN_GROUP])
    slopes_dil = slopes[2 * WIN_GROUP:]
    cols, qscale, win_heads = _in_proj_columns(d)
    pool_w = d // 4
    dil_w = 2 * len(DIL_PAIRS) * HEAD_DIM
    mix_rows = np.concatenate([
        np.arange(pool_w + dil_w),
        np.concatenate([pool_w + dil_w + h * HEAD_DIM + np.arange(HEAD_DIM) for h in win_heads]),
    ])
    n_grp = len(POOL_WINDOWS)
    pg = pool_w // n_grp

    mod = _modulation(c, w_ada, b_ada).reshape(depth, b, N_MOD, d)
    tm = 512
    for l in range(depth):
        w_in_p = (_take_runs(w_in[l], cols, 1) * qscale).astype(_BF16)
        w_out_p = _take_runs(w_out[l], mix_rows, 0).astype(_BF16)
        zpad = lambda n: jnp.zeros((pg, n * pg), _F32)
        wpool_bd = jnp.concatenate(
            [jnp.concatenate([zpad(g), w_pool[l, g], zpad(n_grp - 1 - g)], axis=1)
             for g in range(n_grp)], axis=0).astype(_BF16)
        sink = sink_logit[l]

        (ya, q0, k0, v0, q1, k1, v1, q2, k2, v2, qc, kc, vc) = _in_proj(
            x, mod, l, norm1_g[l].reshape(1, d), w_in_p, wpool_bd,
            pool_scale[l].reshape(1, pool_w), tm)
        outs, lses = [], []
        for g, ((_, dil), (q, k, v)) in enumerate(zip(
                DIL_PAIRS, ((q0, k0, v0), (q1, k1, v1), (q2, k2, v2)))):
            sl = tuple(float(v_) for v_ in slopes_dil[2 * g:2 * g + 2])
            o, ls = _band_attn(q.reshape(b, s, LANES), k.reshape(b, s, LANES),
                               v.reshape(b, s, LANES), dil, sl)
            outs.append(o)
            lses.append(ls)
        yc = _win_attn(sink, qc, kc, vc, slopes_win, min(s, 4096))
        x = _token_mixers_out(
            x, mod, l, ya, outs, lses, yc, w_out_p,
            norm2_g[l].reshape(1, d), w_gate[l].astype(_BF16), w_up[l].astype(_BF16),
            w_down[l].astype(_BF16), final_g.reshape(1, d), l == depth - 1, tm)
    return x
```

```python
import functools
import math

import jax
import jax.numpy as jnp
import numpy as np
from jax import lax
from jax.experimental import pallas as pl
from jax.experimental.pallas import tpu as pltpu

LANES = 128
HEAD_DIM = 64
HALF = LANES // 2
POOL_WINDOWS = (2, 4, 8, 16)
POOL_HALO = 8
DIL_PAIRS = ((128, 1), (512, 4), (2048, 16))
DIL_RADIUS = 64
WIN_RADIUS = 128
WIN_GROUP = 3
N_MOD = 6
EPS = 1e-6
NEG = -1e30
QBLK = 128
BAND_UNROLL = 2
VMEM_LIMIT = 56 * 1024 * 1024

_F32 = jnp.float32
_BF16 = jnp.bfloat16


def _alibi_slopes(n):
    i = np.arange(1, n + 1, dtype=np.float32)
    return np.exp2(np.float32(-8.0) * i / np.float32(n)).astype(np.float32)


def _cparams(sem):
    return pltpu.CompilerParams(dimension_semantics=sem, vmem_limit_bytes=VMEM_LIMIT)


def _mod_kernel(c_ref, w_ref, b_ref, o_ref):
    c = c_ref[...]
    act = (c * (1.0 / (1.0 + jnp.exp(-c)))).astype(_BF16)
    w = w_ref[0].astype(_BF16)
    o_ref[0] = jnp.dot(act, w, preferred_element_type=_F32) + b_ref[0]


def _modulation(c, w_ada, b_ada):
    depth, d, nd = w_ada.shape
    b = c.shape[0]
    tn = d
    return pl.pallas_call(
        _mod_kernel,
        out_shape=jax.ShapeDtypeStruct((depth, b, nd), _F32),
        grid=(depth, nd // tn),
        in_specs=[
            pl.BlockSpec((b, d), lambda l, j: (0, 0)),
            pl.BlockSpec((1, d, tn), lambda l, j: (l, 0, j)),
            pl.BlockSpec((1, 1, tn), lambda l, j: (l, 0, j)),
        ],
        out_specs=pl.BlockSpec((1, b, tn), lambda l, j: (l, 0, j)),
        compiler_params=_cparams(("arbitrary", "arbitrary")),
        name="modulation",
    )(c, w_ada, b_ada.reshape(depth, 1, nd))


def _modulated_norm(x, g, shift, scale):
    ms = jnp.mean(x * x, axis=-1, keepdims=True)
    return (x * lax.rsqrt(ms + EPS)) * (g * (1.0 + scale)) + shift


def _pooled_tokens(ubuf_ref, t0, s_len):
    tm = ubuf_ref.shape[0] - 2 * POOL_HALO
    cols = []
    for c in range(2):
        r_lo, r_hi = POOL_WINDOWS[2 * c] // 2, POOL_WINDOWS[2 * c + 1] // 2
        lanes = slice(c * LANES, (c + 1) * LANES)
        lane = lax.broadcasted_iota(jnp.int32, (1, LANES), 1)
        radius = jnp.where(lane < HALF, r_lo, r_hi)
        u = ubuf_ref[POOL_HALO:POOL_HALO + tm, lanes]
        wsum = u
        for k in range(1, r_hi + 1):
            pair = (ubuf_ref[POOL_HALO - k:POOL_HALO - k + tm, lanes]
                    + ubuf_ref[POOL_HALO + k:POOL_HALO + k + tm, lanes])
            wsum = wsum + (pair if k <= r_lo else jnp.where(lane < HALF, 0.0, pair))
        t = t0 + lax.broadcasted_iota(jnp.int32, (tm, LANES), 0)
        cnt = jnp.minimum(t + radius + 1, s_len) - jnp.maximum(t - radius, 0)
        cols.append((wsum / cnt.astype(_F32) - u).astype(_BF16))
    return jnp.concatenate(cols, axis=1)


def _in_proj_kernel(x_ref, xp_ref, xn_ref, mod_ref, g_ref, w_ref, wp_ref, ps_ref,
                    ya_ref, q0_ref, k0_ref, v0_ref, q1_ref, k1_ref, v1_ref,
                    q2_ref, k2_ref, v2_ref, qc_ref, kc_ref, vc_ref, zs_ref, ubuf_ref,
                    *, s_len):
    tm = x_ref.shape[1]
    i = pl.program_id(1)
    cw = 2 * LANES
    mod = mod_ref[0, 0]
    norm = lambda rows: _modulated_norm(rows, g_ref[...], mod[0:1], mod[1:2]).astype(_BF16)
    h = norm(x_ref[0])

    def proj(lhs, t0, t1):
        return jnp.dot(lhs, w_ref[:, t0 * LANES:t1 * LANES], preferred_element_type=_F32)

    halo = proj(norm(jnp.concatenate([xp_ref[0], xn_ref[0]], axis=0)), 0, 2)
    ubuf_ref[0:POOL_HALO] = jnp.where(i > 0, halo[:POOL_HALO], 0.0)
    ubuf_ref[POOL_HALO:POOL_HALO + tm] = proj(h, 0, 2)
    ubuf_ref[POOL_HALO + tm:] = jnp.where(i < pl.num_programs(1) - 1, halo[POOL_HALO:], 0.0)
    pooled = _pooled_tokens(ubuf_ref, i * tm, s_len)

    z_lo, z_hi = proj(h, 2, 8), proj(h, 8, 16)

    def tile(i):
        z, j = (z_lo, i - 2) if i < 8 else (z_hi, i - 8)
        return z[:, j * LANES:(j + 1) * LANES]

    for j, ref in enumerate((q0_ref, k0_ref, v0_ref)):
        ref[0] = tile(2 + j).astype(_BF16)
    for refs, (_, dil), base in (((q1_ref, k1_ref, v1_ref), DIL_PAIRS[1], 5),
                                 ((q2_ref, k2_ref, v2_ref), DIL_PAIRS[2], 8)):
        rows = tm // dil
        for j, ref in enumerate(refs):
            zs_ref[base - 5 + j] = tile(base + j)
            for r in range(dil):
                ref[0, r] = zs_ref[base - 5 + j, pl.ds(r, rows, stride=dil), :].astype(_BF16)
    for t in range(WIN_GROUP):
        qc_ref[0, t] = tile(11 + t).astype(_BF16)
    kc_ref[0] = tile(14).astype(_BF16)
    vc_ref[0] = tile(15).astype(_BF16)
    ya = jnp.dot(pooled, wp_ref[...], preferred_element_type=_F32) * ps_ref[...]
    ya_ref[0] = ya.astype(_BF16)


def _in_proj(x, mod, l, g, w_in_p, wpool_bd, pool_scale, tm):
    b, s, d = x.shape
    n_in = w_in_p.shape[-1]
    cw = 2 * LANES
    hb = tm // POOL_HALO
    nhb = s // POOL_HALO
    d1, d2 = DIL_PAIRS[1][1], DIL_PAIRS[2][1]
    nat = lambda: pl.BlockSpec((1, tm, LANES), lambda bi, i: (bi, i, 0))
    res = lambda dil: pl.BlockSpec((1, dil, tm // dil, LANES), lambda bi, i: (bi, 0, i, 0))
    full = lambda shape: pl.BlockSpec((None,) + shape, lambda bi, i: (l,) + (0,) * len(shape))
    sds = jax.ShapeDtypeStruct
    out_shape = (
        [sds((b, s, cw), _BF16)]
        + [sds((b, s, LANES), _BF16)] * 3
        + [sds((b, d1, s // d1, LANES), _BF16)] * 3
        + [sds((b, d2, s // d2, LANES), _BF16)] * 3
        + [sds((b, WIN_GROUP, s, LANES), _BF16)]
        + [sds((b, s, LANES), _BF16)] * 2
    )
    out_specs = (
        [pl.BlockSpec((1, tm, 2 * LANES), lambda bi, i: (bi, i, 0))]
        + [nat() for _ in range(3)]
        + [res(d1) for _ in range(3)]
        + [res(d2) for _ in range(3)]
        + [pl.BlockSpec((1, WIN_GROUP, tm, LANES), lambda bi, i: (bi, 0, i, 0))]
        + [nat() for _ in range(2)]
    )
    return pl.pallas_call(
        functools.partial(_in_proj_kernel, s_len=s),
        out_shape=out_shape,
        grid=(b, s // tm),
        in_specs=[
            pl.BlockSpec((1, tm, d), lambda bi, i: (bi, i, 0)),
            pl.BlockSpec((1, POOL_HALO, d), lambda bi, i: (bi, jnp.maximum(i * hb - 1, 0), 0)),
            pl.BlockSpec((1, POOL_HALO, d),
                         lambda bi, i: (bi, jnp.minimum((i + 1) * hb, nhb - 1), 0)),
            pl.BlockSpec((1, 1, N_MOD, d), lambda bi, i: (l, bi, 0, 0)),
            full((1, d)),
            full((d, n_in)),
            full((cw, cw)),
            full((1, cw)),
        ],
        out_specs=out_specs,
        scratch_shapes=[pltpu.VMEM((6, tm, LANES), _F32),
                        pltpu.VMEM((tm + 2 * POOL_HALO, cw), _F32)],
        compiler_params=_cparams(("arbitrary", "arbitrary")),
        name="in_proj",
    )(x, x, x, mod, g, w_in_p, wpool_bd, pool_scale)


def _band_bias(shape, row_heads, slopes, dist_scale, radius, offset):
    rows, cols = shape
    row = lax.broadcasted_iota(jnp.int32, shape, 0)
    col = lax.broadcasted_iota(jnp.int32, shape, 1)
    dist = jnp.abs(col - offset - (row % QBLK))
    slope = jnp.full(shape, slopes[row_heads[0]] * dist_scale, _F32)
    for i, hd in enumerate(row_heads[1:], start=1):
        slope = jnp.where(row >= i * QBLK, np.float32(slopes[hd] * dist_scale), slope)
    return jnp.where(dist <= radius, -slope * dist.astype(_F32), NEG)


def _window(i, nblk, radius, width):
    seq = nblk * QBLK
    ws = jnp.clip(i * QBLK - radius, 0, seq - width)
    variant = jnp.where(i == 0, 0, jnp.where(i == nblk - 1, 2, 1))
    return ws, variant


def _trans_b_dot(a, b):
    return lax.dot_general(a, b, (((1,), (1,)), ((), ())), preferred_element_type=_F32)


def _band_attn_kernel(q_ref, k_ref, v_ref, o_ref, lse_ref, vext_ref, bias_ref, *scratch,
                      dil, slopes):
    s_len = q_ref.shape[1]
    n = s_len // dil
    nblk = n // QBLK
    width = QBLK + 2 * DIL_RADIUS

    @pl.when(pl.program_id(0) == 0)
    def _():
        for var, off in enumerate((0, DIL_RADIUS, width - QBLK)):
            bias_ref[var] = _band_bias((2 * QBLK, width), (0, 1), slopes, float(dil),
                                       DIL_RADIUS, off)

    vext_ref[:, :LANES] = v_ref[0]
    vext_ref[:, LANES:] = jnp.ones((s_len, LANES), _BF16)

    lane = lax.broadcasted_iota(jnp.int32, (QBLK, LANES), 1)
    lo = lane < HALF
    s_bufs, e_bufs, m_bufs = scratch[0:2], scratch[2:4], scratch[4:6]
    ntot = s_len // QBLK

    def locate(blk):
        r = blk // nblk
        i = blk % nblk
        ws, variant = _window(i, nblk, DIL_RADIUS, width)
        return r, i, pl.multiple_of(r * n + ws, DIL_RADIUS), variant

    def scores(blk, par):
        _, _, kstart, variant = locate(blk)
        qstart = pl.multiple_of(blk * QBLK, QBLK)
        q = q_ref[0, pl.ds(qstart, QBLK), :]
        kw = k_ref[0, pl.ds(kstart, width), :]
        zero = jnp.zeros_like(q)
        qs = jnp.concatenate([jnp.where(lo, q, zero), jnp.where(lo, zero, q)], axis=0)
        s_bufs[par][...] = _trans_b_dot(qs, kw) + bias_ref[variant]

    def numerators(par):
        sc = s_bufs[par][...]
        mx = sc[:, :LANES]
        for c in range(1, width // LANES):
            mx = jnp.maximum(mx, sc[:, c * LANES:(c + 1) * LANES])
        m = jnp.max(mx, axis=-1, keepdims=True)
        e_bufs[par][...] = jnp.exp(sc - m).astype(_BF16)
        m_bufs[par][...] = jnp.broadcast_to(m, (2 * QBLK, LANES))

    def outputs(blk, par):
        r, i, kstart, _ = locate(blk)
        vw = vext_ref[pl.ds(kstart, width), :]
        o = jnp.dot(e_bufs[par][...], vw, preferred_element_type=_F32)
        den = o[:, LANES:]
        res = o[:, :LANES] / den
        lse_all = m_bufs[par][...] + jnp.log(den)
        out = jnp.where(lo, res[:QBLK], res[QBLK:])
        lse = jnp.where(lo, lse_all[:QBLK], lse_all[QBLK:])
        if dil == 1:
            qstart = pl.multiple_of(blk * QBLK, QBLK)
            o_ref[0, pl.ds(qstart, QBLK), :] = out.astype(_BF16)
            lse_ref[0, pl.ds(qstart, QBLK), :] = lse
        else:
            tstart = i * (QBLK * dil) + r
            scratch[6][pl.ds(tstart, QBLK, stride=dil), :] = out
            lse_ref[0, pl.ds(tstart, QBLK, stride=dil), :] = lse

    scores(0, 0)
    scores(1, 1)
    numerators(0)

    def body(p, carry):
        it = 2 * p
        scores(it, 0)
        numerators(1)
        outputs(it - 2, 0)
        scores(it + 1, 1)
        numerators(0)
        outputs(it - 1, 1)
        return carry

    lax.fori_loop(1, ntot // 2, body, 0, unroll=BAND_UNROLL)
    numerators(1)
    outputs(ntot - 2, 0)
    outputs(ntot - 1, 1)
    if dil != 1:
        o_ref[0] = scratch[6][...].astype(_BF16)


def _band_attn(q, k, v, dil, slopes):
    b, s, _ = q.shape
    width = QBLK + 2 * DIL_RADIUS
    spec = lambda: pl.BlockSpec((1, s, LANES), lambda bi: (bi, 0, 0))
    scratch = ([pltpu.VMEM((s, 2 * LANES), _BF16), pltpu.VMEM((3, 2 * QBLK, width), _F32)]
               + [pltpu.VMEM((2 * QBLK, width), _F32)] * 2
               + [pltpu.VMEM((2 * QBLK, width), _BF16)] * 2
               + [pltpu.VMEM((2 * QBLK, LANES), _F32)] * 2)
    if dil != 1:
        scratch.append(pltpu.VMEM((s, LANES), _F32))
    return pl.pallas_call(
        functools.partial(_band_attn_kernel, dil=dil, slopes=slopes),
        out_shape=[jax.ShapeDtypeStruct((b, s, LANES), _BF16),
                   jax.ShapeDtypeStruct((b, s, LANES), _F32)],
        grid=(b,),
        in_specs=[spec(), spec(), spec()],
        out_specs=[spec(), spec()],
        scratch_shapes=scratch,
        compiler_params=_cparams(("arbitrary",)),
        name=f"band_attn_d{dil}",
    )(q, k, v)


def _win_attn_kernel(sink_ref, q_ref, k_ref, v_ref, o_ref, vext_ref, bias_ref,
                     s0_ref, s1_ref, e0_ref, e1_ref, t0_ref, t1_ref, *, slopes, layer):
    s_len = k_ref.shape[1]
    tc = q_ref.shape[2]
    nblk = s_len // QBLK
    nloc = tc // QBLK
    width = QBLK + 2 * WIN_RADIUS
    rows = WIN_GROUP * QBLK
    s_bufs, e_bufs, t_bufs = (s0_ref, s1_ref), (e0_ref, e1_ref), (t0_ref, t1_ref)

    @pl.when((pl.program_id(0) == 0) & (pl.program_id(1) == 0))
    def _():
        for var, off in enumerate((0, WIN_RADIUS, width - QBLK)):
            for j in range(2):
                heads = tuple(j * WIN_GROUP + t for t in range(WIN_GROUP))
                bias_ref[var, j] = _band_bias((rows, width), heads, slopes, 1.0,
                                              WIN_RADIUS, off)

    @pl.when(pl.program_id(1) == 0)
    def _():
        vext_ref[:, :LANES] = v_ref[0]
        vext_ref[:, LANES:] = jnp.ones((s_len, LANES), _BF16)

    lane = lax.broadcasted_iota(jnp.int32, (rows, LANES), 1)
    lo = lane < HALF
    row = lax.broadcasted_iota(jnp.int32, (rows, 1), 0)
    base = pl.program_id(1) * nloc

    def sink_col(j):
        col = jnp.full((rows, 1), sink_ref[layer, j * WIN_GROUP], _F32)
        for t in range(1, WIN_GROUP):
            col = jnp.where(row >= t * QBLK, sink_ref[layer, j * WIN_GROUP + t], col)
        return col

    def kstart_of(bl):
        ws, variant = _window(base + bl, nblk, WIN_RADIUS, width)
        return pl.multiple_of(ws, QBLK), variant

    def scores(bl, par):
        kstart, variant = kstart_of(bl)
        qstart = pl.multiple_of(bl * QBLK, QBLK)
        q = q_ref[0, :, pl.ds(qstart, QBLK), :].reshape(rows, LANES)
        kw = k_ref[0, pl.ds(kstart, width), :]
        zero = jnp.zeros_like(q)
        for j in range(2):
            qh = jnp.where(lo, q, zero) if j == 0 else jnp.where(lo, zero, q)
            s_bufs[par][j] = _trans_b_dot(qh, kw) + bias_ref[variant, j]

    def numerators(par):
        for j in range(2):
            sc = s_bufs[par][j]
            mx = sc[:, :LANES]
            for c in range(1, width // LANES):
                mx = jnp.maximum(mx, sc[:, c * LANES:(c + 1) * LANES])
            sink = sink_col(j)
            m = jnp.maximum(jnp.max(mx, axis=-1, keepdims=True), sink)
            e_bufs[par][j] = jnp.exp(sc - m).astype(_BF16)
            t_bufs[par][j] = jnp.broadcast_to(jnp.exp(sink - m), (rows, LANES))

    def outputs(bl, par):
        kstart, _ = kstart_of(bl)
        qstart = pl.multiple_of(bl * QBLK, QBLK)
        vw = vext_ref[pl.ds(kstart, width), :]
        res = []
        for j in range(2):
            o = jnp.dot(e_bufs[par][j], vw, preferred_element_type=_F32)
            res.append(o[:, :LANES] / (o[:, LANES:] + t_bufs[par][j]))
        out = jnp.where(lo, res[0], res[1]).astype(_BF16)
        for t in range(WIN_GROUP):
            o_ref[0, pl.ds(qstart, QBLK), t * LANES:(t + 1) * LANES] = (
                out[t * QBLK:(t + 1) * QBLK])

    scores(0, 0)
    scores(1, 1)
    numerators(0)

    def body(p, carry):
        it = 2 * p
        scores(it, 0)
        numerators(1)
        outputs(it - 2, 0)
        scores(it + 1, 1)
        numerators(0)
        outputs(it - 1, 1)
        return carry

    lax.fori_loop(1, nloc // 2, body, 0)
    numerators(1)
    outputs(nloc - 2, 0)
    outputs(nloc - 1, 1)


def _win_attn(sink, layer, q, k, v, slopes, tc):
    b, _, s, _ = q.shape
    width = QBLK + 2 * WIN_RADIUS
    rows = WIN_GROUP * QBLK
    assert (tc // QBLK) % 2 == 0 and tc // QBLK >= 4
    return pl.pallas_call(
        functools.partial(_win_attn_kernel, slopes=slopes, layer=layer),
        out_shape=jax.ShapeDtypeStruct((b, s, WIN_GROUP * LANES), _BF16),
        grid=(b, s // tc),
        in_specs=[
            pl.BlockSpec(memory_space=pltpu.SMEM),
            pl.BlockSpec((1, WIN_GROUP, tc, LANES), lambda bi, i: (bi, 0, i, 0)),
            pl.BlockSpec((1, s, LANES), lambda bi, i: (bi, 0, 0)),
            pl.BlockSpec((1, s, LANES), lambda bi, i: (bi, 0, 0)),
        ],
        out_specs=pl.BlockSpec((1, tc, WIN_GROUP * LANES), lambda bi, i: (bi, i, 0)),
        scratch_shapes=[pltpu.VMEM((s, 2 * LANES), _BF16),
                        pltpu.VMEM((3, 2, rows, width), _F32)]
                       + [pltpu.VMEM((2, rows, width), _F32)] * 2
                       + [pltpu.VMEM((2, rows, width), _BF16)] * 2
                       + [pltpu.VMEM((2, rows, LANES), _F32)] * 2,
        compiler_params=_cparams(("arbitrary", "arbitrary")),
        name="win_attn",
    )(sink, q, k, v)


def _token_kernel(x_ref, mod_ref, ya_ref,
                  o0_ref, o1_ref, o2_ref, l0_ref, l1_ref, l2_ref, yc_ref, wo_ref,
                  g2_ref, wg_ref, wu_ref, wd_ref, fg_ref,
                  out_ref, mix_ref, *, final, ff_chunks):
    cw = 2 * LANES
    mod = mod_ref[0, 0]
    mix_ref[:, :cw] = ya_ref[0]

    lses = [l0_ref[0], l1_ref[0], l2_ref[0]]
    mx = jnp.maximum(jnp.maximum(lses[0], lses[1]), lses[2])
    es = [jnp.exp(ls - mx) for ls in lses]
    den = es[0] + es[1] + es[2]
    for g, o_ref in enumerate((o0_ref, o1_ref, o2_ref)):
        yb = (es[g] / den) * o_ref[0]
        mix_ref[:, cw + g * LANES:cw + (g + 1) * LANES] = yb.astype(_BF16)
    mix_ref[:, cw + 3 * LANES:] = yc_ref[0]

    x1 = x_ref[0] + mod[2:3] * jnp.dot(mix_ref[...], wo_ref[...], preferred_element_type=_F32)

    h = _modulated_norm(x1, g2_ref[...], mod[3:4], mod[4:5]).astype(_BF16)
    ffn = None
    for c0, c1 in ff_chunks:
        gate = jnp.dot(h, wg_ref[:, c0:c1], preferred_element_type=_F32)
        up = jnp.dot(h, wu_ref[:, c0:c1], preferred_element_type=_F32)
        act = ((gate * (1.0 / (1.0 + jnp.exp(-gate)))) * up).astype(_BF16)
        part = jnp.dot(act, wd_ref[c0:c1, :], preferred_element_type=_F32)
        ffn = part if ffn is None else ffn + part
    y = x1 + mod[5:6] * ffn
    if final:
        ms = jnp.mean(y * y, axis=-1, keepdims=True)
        y = (y * lax.rsqrt(ms + EPS)) * fg_ref[...]
    out_ref[0] = y


def _ff_chunks(dff):
    mxu_k = 2 * LANES
    cut = (dff // 2 + mxu_k - 1) // mxu_k * mxu_k
    return ((0, cut), (cut, dff)) if 0 < cut < dff else ((0, dff),)


def _token_mixers_out(x, mod, l, ya, outs, lses, yc, w_out_p,
                      g2, wg, wu, wd, final_g, final, tm):
    b, s, d = x.shape
    dff = wg.shape[-1]
    cw = 2 * LANES
    tok = lambda width: pl.BlockSpec((1, tm, width), lambda bi, i: (bi, i, 0))
    const = lambda shape: pl.BlockSpec((None,) + shape, lambda bi, i: (l,) + (0,) * len(shape),
                                       pipeline_mode=pl.Buffered(1))
    return pl.pallas_call(
        functools.partial(_token_kernel, final=final, ff_chunks=_ff_chunks(dff)),
        out_shape=jax.ShapeDtypeStruct((b, s, d), _F32),
        grid=(b, s // tm),
        in_specs=[
            tok(d),
            pl.BlockSpec((1, 1, N_MOD, d), lambda bi, i: (l, bi, 0, 0)),
            tok(cw),
            tok(LANES), tok(LANES), tok(LANES),
            tok(LANES), tok(LANES), tok(LANES),
            tok(WIN_GROUP * LANES),
            const((d, d)),
            const((1, d)),
            const((d, dff)), const((d, dff)), const((dff, d)),
            pl.BlockSpec((1, d), lambda bi, i: (0, 0)),
        ],
        out_specs=tok(d),
        scratch_shapes=[pltpu.VMEM((tm, d), _BF16)],
        compiler_params=_cparams(("arbitrary", "arbitrary")),
        name="token_mix_ffn",
    )(x, mod, ya, *outs, *lses, yc, w_out_p, g2, wg, wu, wd, final_g)


def _in_proj_columns(d_model):
    pool_w = d_model // 4
    dil_w = 2 * len(DIL_PAIRS) * HEAD_DIM
    o_qb = pool_w
    o_kb = o_qb + dil_w
    o_vb = o_kb + dil_w
    o_qc = o_vb + dil_w
    o_kc = o_qc + 2 * WIN_GROUP * HEAD_DIM
    o_vc = o_kc + 2 * HEAD_DIM
    cols = list(range(pool_w))
    for g in range(len(DIL_PAIRS)):
        for base in (o_qb, o_kb, o_vb):
            cols += range(base + g * LANES, base + (g + 1) * LANES)
    win_heads = []
    for t in range(WIN_GROUP):
        for j in range(2):
            win_heads.append(j * WIN_GROUP + t)
            cols += range(o_qc + (j * WIN_GROUP + t) * HEAD_DIM,
                          o_qc + (j * WIN_GROUP + t + 1) * HEAD_DIM)
    cols += range(o_kc, o_kc + LANES)
    cols += range(o_vc, o_vc + LANES)
    qscale = np.ones((len(cols),), np.float32)
    for g in range(len(DIL_PAIRS)):
        qscale[pool_w + 3 * g * LANES:pool_w + (3 * g + 1) * LANES] = HEAD_DIM ** -0.5
    q0 = pool_w + 3 * len(DIL_PAIRS) * LANES
    qscale[q0:q0 + WIN_GROUP * LANES] = HEAD_DIM ** -0.5
    return np.asarray(cols), qscale, win_heads


def _take_runs(w, idx, axis):
    idx = np.asarray(idx)
    cuts = [0] + [i for i in range(1, len(idx)) if idx[i] != idx[i - 1] + 1] + [len(idx)]
    parts = [lax.slice_in_dim(w, int(idx[a]), int(idx[b - 1]) + 1, axis=axis)
             for a, b in zip(cuts[:-1], cuts[1:])]
    return jnp.concatenate(parts, axis=axis)


def kernel(x, c, norm1_g, norm2_g, w_ada, b_ada, w_in, w_pool, pool_scale, sink_logit,
           w_out, w_gate, w_up, w_down, final_g):
    b, s, d = x.shape
    depth = w_in.shape[0]
    assert HEAD_DIM ** -0.5 == 0.125
    slopes = _alibi_slopes(2 * WIN_GROUP + 2 * len(DIL_PAIRS))
    slopes_win = tuple(float(v) for v in slopes[:2 * WIN_GROUP])
    slopes_dil = slopes[2 * WIN_GROUP:]
    cols, qscale, win_heads = _in_proj_columns(d)
    pool_w = d // 4
    dil_w = 2 * len(DIL_PAIRS) * HEAD_DIM
    mix_rows = np.concatenate([
        np.arange(pool_w + dil_w),
        np.concatenate([pool_w + dil_w + h * HEAD_DIM + np.arange(HEAD_DIM) for h in win_heads]),
    ])
    n_grp = len(POOL_WINDOWS)
    pg = pool_w // n_grp

    mod = _modulation(c, w_ada, b_ada).reshape(depth, b, N_MOD, d)
    w_in_p = (_take_runs(w_in, cols, 2) * qscale).astype(_BF16)
    w_out_p = _take_runs(w_out, mix_rows, 1).astype(_BF16)
    zpad = lambda n: jnp.zeros((depth, pg, n * pg), _F32)
    wpool_bd = jnp.concatenate(
        [jnp.concatenate([zpad(g), w_pool[:, g], zpad(n_grp - 1 - g)], axis=2)
         for g in range(n_grp)], axis=1).astype(_BF16)
    wg, wu, wd = w_gate.astype(_BF16), w_up.astype(_BF16), w_down.astype(_BF16)
    g1, g2 = norm1_g.reshape(depth, 1, d), norm2_g.reshape(depth, 1, d)
    ps = pool_scale.reshape(depth, 1, pool_w)
    tm = 512
    for l in range(depth):
        (ya, q0, k0, v0, q1, k1, v1, q2, k2, v2, qc, kc, vc) = _in_proj(
            x, mod, l, g1, w_in_p, wpool_bd, ps, 2 * tm)
        outs, lses = [], []
        for g, ((_, dil), (q, k, v)) in enumerate(zip(
                DIL_PAIRS, ((q0, k0, v0), (q1, k1, v1), (q2, k2, v2)))):
            sl = tuple(float(v_) for v_ in slopes_dil[2 * g:2 * g + 2])
            o, ls = _band_attn(q.reshape(b, s, LANES), k.reshape(b, s, LANES),
                               v.reshape(b, s, LANES), dil, sl)
            outs.append(o)
            lses.append(ls)
        yc = _win_attn(sink_logit, l, qc, kc, vc, slopes_win, min(s, 4096))
        x = _token_mixers_out(x, mod, l, ya, outs, lses, yc, w_out_p, g2, wg, wu, wd,
                              final_g.reshape(1, d), l == depth - 1, tm)
    return x
```

```python
import functools
import math

import jax
import jax.numpy as jnp
import numpy as np
from jax import lax
from jax.experimental import pallas as pl
from jax.experimental.pallas import tpu as pltpu

LANES = 128
HEAD_DIM = 64
HALF = LANES // 2
POOL_WINDOWS = (2, 4, 8, 16)
POOL_HALO = 8
DIL_PAIRS = ((128, 1), (512, 4), (2048, 16))
DIL_RADIUS = 64
WIN_RADIUS = 128
WIN_GROUP = 3
N_MOD = 6
EPS = 1e-6
NEG = -1e30
QBLK = 128
ONES_ROWS = 16
LOG2E = math.log2(math.e)
BAND_UNROLL = 2
VMEM_LIMIT = 56 * 1024 * 1024

_F32 = jnp.float32
_BF16 = jnp.bfloat16


def _alibi_slopes(n):
    i = np.arange(1, n + 1, dtype=np.float32)
    return np.exp2(np.float32(-8.0) * i / np.float32(n)).astype(np.float32)


def _cparams(sem):
    return pltpu.CompilerParams(dimension_semantics=sem, vmem_limit_bytes=VMEM_LIMIT)


def _mod_kernel(c_ref, w_ref, b_ref, o_ref):
    c = c_ref[...]
    act = (c * (1.0 / (1.0 + jnp.exp(-c)))).astype(_BF16)
    w = w_ref[0].astype(_BF16)
    o_ref[0] = jnp.dot(act, w, preferred_element_type=_F32) + b_ref[0]


def _modulation(c, w_ada, b_ada):
    depth, d, nd = w_ada.shape
    b = c.shape[0]
    tn = d
    return pl.pallas_call(
        _mod_kernel,
        out_shape=jax.ShapeDtypeStruct((depth, b, nd), _F32),
        grid=(depth, nd // tn),
        in_specs=[
            pl.BlockSpec((b, d), lambda l, j: (0, 0)),
            pl.BlockSpec((1, d, tn), lambda l, j: (l, 0, j)),
            pl.BlockSpec((1, 1, tn), lambda l, j: (l, 0, j)),
        ],
        out_specs=pl.BlockSpec((1, b, tn), lambda l, j: (l, 0, j)),
        compiler_params=_cparams(("arbitrary", "arbitrary")),
        name="modulation",
    )(c, w_ada, b_ada.reshape(depth, 1, nd))


def _modulated_norm(x, g, shift, scale):
    ms = jnp.mean(x * x, axis=-1, keepdims=True)
    return (x * lax.rsqrt(ms + EPS)) * (g * (1.0 + scale)) + shift


def _pooled_tokens(ubuf_ref, t0, s_len):
    tm = ubuf_ref.shape[0] - 2 * POOL_HALO
    cols = []
    for c in range(2):
        r_lo, r_hi = POOL_WINDOWS[2 * c] // 2, POOL_WINDOWS[2 * c + 1] // 2
        lanes = slice(c * LANES, (c + 1) * LANES)
        lane = lax.broadcasted_iota(jnp.int32, (1, LANES), 1)
        radius = jnp.where(lane < HALF, r_lo, r_hi)
        u = ubuf_ref[POOL_HALO:POOL_HALO + tm, lanes]
        wsum = u
        for k in range(1, r_hi + 1):
            pair = (ubuf_ref[POOL_HALO - k:POOL_HALO - k + tm, lanes]
                    + ubuf_ref[POOL_HALO + k:POOL_HALO + k + tm, lanes])
            wsum = wsum + (pair if k <= r_lo else jnp.where(lane < HALF, 0.0, pair))
        t = t0 + lax.broadcasted_iota(jnp.int32, (tm, LANES), 0)
        cnt = jnp.minimum(t + radius + 1, s_len) - jnp.maximum(t - radius, 0)
        cols.append((wsum / cnt.astype(_F32) - u).astype(_BF16))
    return jnp.concatenate(cols, axis=1)


def _in_proj_kernel(x_ref, xp_ref, xn_ref, mod_ref, g_ref, w_ref, wp_ref, ps_ref,
                    ya_ref, q0_ref, k0_ref, v0_ref, q1_ref, k1_ref, v1_ref,
                    q2_ref, k2_ref, v2_ref, qc_ref, kc_ref, vct_ref, zs_ref, ubuf_ref,
                    *, s_len):
    tm = x_ref.shape[1]
    i = pl.program_id(1)
    cw = 2 * LANES
    mod = mod_ref[0, 0]
    norm = lambda rows: _modulated_norm(rows, g_ref[...], mod[0:1], mod[1:2]).astype(_BF16)
    h = norm(x_ref[0])

    def proj(lhs, t0, t1):
        return jnp.dot(lhs, w_ref[:, t0 * LANES:t1 * LANES], preferred_element_type=_F32)

    halo = proj(norm(jnp.concatenate([xp_ref[0], xn_ref[0]], axis=0)), 0, 2)
    ubuf_ref[0:POOL_HALO] = jnp.where(i > 0, halo[:POOL_HALO], 0.0)
    ubuf_ref[POOL_HALO:POOL_HALO + tm] = proj(h, 0, 2)
    ubuf_ref[POOL_HALO + tm:] = jnp.where(i < pl.num_programs(1) - 1, halo[POOL_HALO:], 0.0)
    pooled = _pooled_tokens(ubuf_ref, i * tm, s_len)

    z_lo, z_hi = proj(h, 2, 8), proj(h, 8, 16)

    def tile(i):
        z, j = (z_lo, i - 2) if i < 8 else (z_hi, i - 8)
        return z[:, j * LANES:(j + 1) * LANES]

    for j, ref in enumerate((q0_ref, k0_ref, v0_ref)):
        ref[0] = tile(2 + j).astype(_BF16)
    for refs, (_, dil), base in (((q1_ref, k1_ref, v1_ref), DIL_PAIRS[1], 5),
                                 ((q2_ref, k2_ref, v2_ref), DIL_PAIRS[2], 8)):
        rows = tm // dil
        for j, ref in enumerate(refs):
            zs_ref[base - 5 + j] = tile(base + j)
            for r in range(dil):
                ref[0, r] = zs_ref[base - 5 + j, pl.ds(r, rows, stride=dil), :].astype(_BF16)
    for t in range(WIN_GROUP):
        qc_ref[0, t] = tile(11 + t).astype(_BF16)
    kc_ref[0] = tile(14).astype(_BF16)
    vct_ref[0] = tile(15).T.astype(_BF16)
    ya = jnp.dot(pooled, wp_ref[...], preferred_element_type=_F32) * ps_ref[...]
    ya_ref[0] = ya.astype(_BF16)


def _in_proj(x, mod, l, g, w_in_p, wpool_bd, pool_scale, tm):
    b, s, d = x.shape
    n_in = w_in_p.shape[-1]
    cw = 2 * LANES
    hb = tm // POOL_HALO
    nhb = s // POOL_HALO
    d1, d2 = DIL_PAIRS[1][1], DIL_PAIRS[2][1]
    nat = lambda: pl.BlockSpec((1, tm, LANES), lambda bi, i: (bi, i, 0))
    res = lambda dil: pl.BlockSpec((1, dil, tm // dil, LANES), lambda bi, i: (bi, 0, i, 0))
    full = lambda shape: pl.BlockSpec((None,) + shape, lambda bi, i: (l,) + (0,) * len(shape))
    sds = jax.ShapeDtypeStruct
    out_shape = (
        [sds((b, s, cw), _BF16)]
        + [sds((b, s, LANES), _BF16)] * 3
        + [sds((b, d1, s // d1, LANES), _BF16)] * 3
        + [sds((b, d2, s // d2, LANES), _BF16)] * 3
        + [sds((b, WIN_GROUP, s, LANES), _BF16)]
        + [sds((b, s, LANES), _BF16), sds((b, LANES, s), _BF16)]
    )
    out_specs = (
        [pl.BlockSpec((1, tm, 2 * LANES), lambda bi, i: (bi, i, 0))]
        + [nat() for _ in range(3)]
        + [res(d1) for _ in range(3)]
        + [res(d2) for _ in range(3)]
        + [pl.BlockSpec((1, WIN_GROUP, tm, LANES), lambda bi, i: (bi, 0, i, 0))]
        + [nat(), pl.BlockSpec((1, LANES, tm), lambda bi, i: (bi, 0, i))]
    )
    return pl.pallas_call(
        functools.partial(_in_proj_kernel, s_len=s),
        out_shape=out_shape,
        grid=(b, s // tm),
        in_specs=[
            pl.BlockSpec((1, tm, d), lambda bi, i: (bi, i, 0)),
            pl.BlockSpec((1, POOL_HALO, d), lambda bi, i: (bi, jnp.maximum(i * hb - 1, 0), 0)),
            pl.BlockSpec((1, POOL_HALO, d),
                         lambda bi, i: (bi, jnp.minimum((i + 1) * hb, nhb - 1), 0)),
            pl.BlockSpec((1, 1, N_MOD, d), lambda bi, i: (l, bi, 0, 0)),
            full((1, d)),
            full((d, n_in)),
            full((cw, cw)),
            full((1, cw)),
        ],
        out_specs=out_specs,
        scratch_shapes=[pltpu.VMEM((6, tm, LANES), _F32),
                        pltpu.VMEM((tm + 2 * POOL_HALO, cw), _F32)],
        compiler_params=_cparams(("arbitrary", "arbitrary")),
        name="in_proj",
    )(x, x, x, mod, g, w_in_p, wpool_bd, pool_scale)


def _band_bias(shape, row_heads, slopes, dist_scale, radius, offset):
    rows, cols = shape
    row = lax.broadcasted_iota(jnp.int32, shape, 0)
    col = lax.broadcasted_iota(jnp.int32, shape, 1)
    dist = jnp.abs(col - offset - (row % QBLK))
    slope = jnp.full(shape, slopes[row_heads[0]] * dist_scale, _F32)
    for i, hd in enumerate(row_heads[1:], start=1):
        slope = jnp.where(row >= i * QBLK, np.float32(slopes[hd] * dist_scale), slope)
    return jnp.where(dist <= radius, -slope * dist.astype(_F32), NEG)


def _band_bias_t(shape, col_heads, slopes, dist_scale, radius, offset):
    row = lax.broadcasted_iota(jnp.int32, shape, 0)
    col = lax.broadcasted_iota(jnp.int32, shape, 1)
    dist = jnp.abs(row - offset - (col % QBLK))
    slope = jnp.full(shape, slopes[col_heads[0]] * dist_scale, _F32)
    for i, hd in enumerate(col_heads[1:], start=1):
        slope = jnp.where(col >= i * QBLK, np.float32(slopes[hd] * dist_scale), slope)
    return jnp.where(dist <= radius, (-slope * dist.astype(_F32)) * LOG2E, NEG)


def _window(i, nblk, radius, width):
    seq = nblk * QBLK
    ws = jnp.clip(i * QBLK - radius, 0, seq - width)
    variant = jnp.where(i == 0, 0, jnp.where(i == nblk - 1, 2, 1))
    return ws, variant


def _trans_b_dot(a, b):
    return lax.dot_general(a, b, (((1,), (1,)), ((), ())), preferred_element_type=_F32)


def _band_attn_kernel(q_ref, k_ref, v_ref, o_ref, lse_ref, vext_ref, bias_ref, *scratch,
                      dil, slopes):
    s_len = q_ref.shape[1]
    n = s_len // dil
    nblk = n // QBLK
    width = QBLK + 2 * DIL_RADIUS

    @pl.when(pl.program_id(0) == 0)
    def _():
        for var, off in enumerate((0, DIL_RADIUS, width - QBLK)):
            bias_ref[var] = _band_bias((2 * QBLK, width), (0, 1), slopes, float(dil),
                                       DIL_RADIUS, off)

    vext_ref[:, :LANES] = v_ref[0]
    vext_ref[:, LANES:] = jnp.ones((s_len, LANES), _BF16)

    lane = lax.broadcasted_iota(jnp.int32, (QBLK, LANES), 1)
    lo = lane < HALF
    s_bufs, e_bufs, m_bufs = scratch[0:2], scratch[2:4], scratch[4:6]
    ntot = s_len // QBLK

    def locate(blk):
        r = blk // nblk
        i = blk % nblk
        ws, variant = _window(i, nblk, DIL_RADIUS, width)
        return r, i, pl.multiple_of(r * n + ws, DIL_RADIUS), variant

    def scores(blk, par):
        _, _, kstart, variant = locate(blk)
        qstart = pl.multiple_of(blk * QBLK, QBLK)
        q = q_ref[0, pl.ds(qstart, QBLK), :]
        kw = k_ref[0, pl.ds(kstart, width), :]
        zero = jnp.zeros_like(q)
        qs = jnp.concatenate([jnp.where(lo, q, zero), jnp.where(lo, zero, q)], axis=0)
        s_bufs[par][...] = _trans_b_dot(qs, kw) + bias_ref[variant]

    def numerators(par):
        sc = s_bufs[par][...]
        mx = sc[:, :LANES]
        for c in range(1, width // LANES):
            mx = jnp.maximum(mx, sc[:, c * LANES:(c + 1) * LANES])
        m = jnp.max(mx, axis=-1, keepdims=True)
        e_bufs[par][...] = jnp.exp(sc - m).astype(_BF16)
        m_bufs[par][...] = jnp.broadcast_to(m, (2 * QBLK, LANES))

    def outputs(blk, par):
        r, i, kstart, _ = locate(blk)
        vw = vext_ref[pl.ds(kstart, width), :]
        o = jnp.dot(e_bufs[par][...], vw, preferred_element_type=_F32)
        den = o[:, LANES:]
        res = o[:, :LANES] / den
        lse_all = m_bufs[par][...] + jnp.log(den)
        out = jnp.where(lo, res[:QBLK], res[QBLK:])
        lse = jnp.where(lo, lse_all[:QBLK], lse_all[QBLK:])
        if dil == 1:
            qstart = pl.multiple_of(blk * QBLK, QBLK)
            o_ref[0, pl.ds(qstart, QBLK), :] = out.astype(_BF16)
            lse_ref[0, pl.ds(qstart, QBLK), :] = lse
        else:
            tstart = i * (QBLK * dil) + r
            scratch[6][pl.ds(tstart, QBLK, stride=dil), :] = out
            lse_ref[0, pl.ds(tstart, QBLK, stride=dil), :] = lse

    scores(0, 0)
    scores(1, 1)
    numerators(0)

    def body(p, carry):
        it = 2 * p
        scores(it, 0)
        numerators(1)
        outputs(it - 2, 0)
        scores(it + 1, 1)
        numerators(0)
        outputs(it - 1, 1)
        return carry

    lax.fori_loop(1, ntot // 2, body, 0, unroll=BAND_UNROLL)
    numerators(1)
    outputs(ntot - 2, 0)
    outputs(ntot - 1, 1)
    if dil != 1:
        o_ref[0] = scratch[6][...].astype(_BF16)


def _band_attn(q, k, v, dil, slopes):
    b, s, _ = q.shape
    width = QBLK + 2 * DIL_RADIUS
    spec = lambda: pl.BlockSpec((1, s, LANES), lambda bi: (bi, 0, 0))
    scratch = ([pltpu.VMEM((s, 2 * LANES), _BF16), pltpu.VMEM((3, 2 * QBLK, width), _F32)]
               + [pltpu.VMEM((2 * QBLK, width), _F32)] * 2
               + [pltpu.VMEM((2 * QBLK, width), _BF16)] * 2
               + [pltpu.VMEM((2 * QBLK, LANES), _F32)] * 2)
    if dil != 1:
        scratch.append(pltpu.VMEM((s, LANES), _F32))
    return pl.pallas_call(
        functools.partial(_band_attn_kernel, dil=dil, slopes=slopes),
        out_shape=[jax.ShapeDtypeStruct((b, s, LANES), _BF16),
                   jax.ShapeDtypeStruct((b, s, LANES), _F32)],
        grid=(b,),
        in_specs=[spec(), spec(), spec()],
        out_specs=[spec(), spec()],
        scratch_shapes=scratch,
        compiler_params=_cparams(("arbitrary",)),
        name=f"band_attn_d{dil}",
    )(q, k, v)


def _win_attn_kernel(sink_ref, q_ref, k_ref, vt_ref, o_ref, bias_ref,
                     s0_ref, s1_ref, e0_ref, e1_ref, t0_ref, t1_ref, *, slopes, layer):
    s_len = k_ref.shape[1]
    tc = q_ref.shape[2]
    nblk = s_len // QBLK
    nloc = tc // QBLK
    width = QBLK + 2 * WIN_RADIUS
    cols = WIN_GROUP * QBLK
    s_bufs, e_bufs, t_bufs = (s0_ref, s1_ref), (e0_ref, e1_ref), (t0_ref, t1_ref)

    @pl.when((pl.program_id(0) == 0) & (pl.program_id(1) == 0))
    def _():
        for var, off in enumerate((0, WIN_RADIUS, width - QBLK)):
            for j in range(2):
                heads = tuple(j * WIN_GROUP + t for t in range(WIN_GROUP))
                bias_ref[var, j] = _band_bias_t((width, cols), heads, slopes, 1.0,
                                                WIN_RADIUS, off)

    qlane = lax.broadcasted_iota(jnp.int32, (1, cols), 1)
    klo = lax.broadcasted_iota(jnp.int32, (width, LANES), 1) < HALF
    base = pl.program_id(1) * nloc

    def sink_row(j):
        row = jnp.full((1, cols), sink_ref[layer, j * WIN_GROUP] * LOG2E, _F32)
        for t in range(1, WIN_GROUP):
            row = jnp.where(qlane >= t * QBLK, sink_ref[layer, j * WIN_GROUP + t] * LOG2E, row)
        return row

    def kstart_of(bl):
        ws, variant = _window(base + bl, nblk, WIN_RADIUS, width)
        return pl.multiple_of(ws, QBLK), variant

    def scores(bl, par):
        kstart, variant = kstart_of(bl)
        qstart = pl.multiple_of(bl * QBLK, QBLK)
        q = q_ref[0, :, pl.ds(qstart, QBLK), :].reshape(cols, LANES)
        kw = k_ref[0, pl.ds(kstart, width), :]
        zero = jnp.zeros_like(kw)
        for j in range(2):
            kh = jnp.where(klo, kw, zero) if j == 0 else jnp.where(klo, zero, kw)
            s_bufs[par][j] = _trans_b_dot(kh, q) + bias_ref[variant, j]

    def numerators(par):
        for j in range(2):
            sc = s_bufs[par][j]
            sink = sink_row(j)
            m = jnp.maximum(jnp.max(sc, axis=0, keepdims=True), sink)
            e_bufs[par][j] = jnp.exp2(sc - m).astype(_BF16)
            t_bufs[par][j] = jnp.broadcast_to(jnp.exp2(sink - m), (8, cols))

    def outputs(bl, par):
        kstart, _ = kstart_of(bl)
        qstart = pl.multiple_of(bl * QBLK, QBLK)
        vt = vt_ref[0, :, pl.ds(kstart, width)]
        ones = jnp.ones((ONES_ROWS, width), _BF16)
        res = []
        for j in range(2):
            lhs = jnp.concatenate([vt[j * HEAD_DIM:(j + 1) * HEAD_DIM], ones], axis=0)
            o = jnp.dot(lhs, e_bufs[par][j], preferred_element_type=_F32)
            den = o[HEAD_DIM:HEAD_DIM + 1] + t_bufs[par][j][0:1]
            res.append(o[:HEAD_DIM] / den)
        for t in range(WIN_GROUP):
            both = jnp.concatenate([r[:, t * QBLK:(t + 1) * QBLK] for r in res], axis=0)
            o_ref[0, pl.ds(qstart, QBLK), t * LANES:(t + 1) * LANES] = both.T.astype(_BF16)

    scores(0, 0)
    scores(1, 1)
    numerators(0)

    def body(p, carry):
        it = 2 * p
        scores(it, 0)
        numerators(1)
        outputs(it - 2, 0)
        scores(it + 1, 1)
        numerators(0)
        outputs(it - 1, 1)
        return carry

    lax.fori_loop(1, nloc // 2, body, 0, unroll=3)
    numerators(1)
    outputs(nloc - 2, 0)
    outputs(nloc - 1, 1)


def _win_attn(sink, layer, q, k, vt, slopes, tc):
    b, _, s, _ = q.shape
    width = QBLK + 2 * WIN_RADIUS
    cols = WIN_GROUP * QBLK
    assert (tc // QBLK) % 2 == 0 and tc // QBLK >= 4
    return pl.pallas_call(
        functools.partial(_win_attn_kernel, slopes=slopes, layer=layer),
        out_shape=jax.ShapeDtypeStruct((b, s, WIN_GROUP * LANES), _BF16),
        grid=(b, s // tc),
        in_specs=[
            pl.BlockSpec(memory_space=pltpu.SMEM),
            pl.BlockSpec((1, WIN_GROUP, tc, LANES), lambda bi, i: (bi, 0, i, 0)),
            pl.BlockSpec((1, s, LANES), lambda bi, i: (bi, 0, 0)),
            pl.BlockSpec((1, LANES, s), lambda bi, i: (bi, 0, 0)),
        ],
        out_specs=pl.BlockSpec((1, tc, WIN_GROUP * LANES), lambda bi, i: (bi, i, 0)),
        scratch_shapes=[pltpu.VMEM((3, 2, width, cols), _F32)]
                       + [pltpu.VMEM((2, width, cols), _F32)] * 2
                       + [pltpu.VMEM((2, width, cols), _BF16)] * 2
                       + [pltpu.VMEM((2, 8, cols), _F32)] * 2,
        compiler_params=_cparams(("arbitrary", "arbitrary")),
        name="win_attn",
    )(sink, q, k, vt)


def _token_kernel(x_ref, mod_ref, ya_ref,
                  o0_ref, o1_ref, o2_ref, l0_ref, l1_ref, l2_ref, yc_ref, wo_ref,
                  g2_ref, wg_ref, wu_ref, wd_ref, fg_ref,
                  out_ref, mix_ref, *, final, ff_chunks):
    cw = 2 * LANES
    mod = mod_ref[0, 0]
    mix_ref[:, :cw] = ya_ref[0]

    lses = [l0_ref[0], l1_ref[0], l2_ref[0]]
    mx = jnp.maximum(jnp.maximum(lses[0], lses[1]), lses[2])
    es = [jnp.exp(ls - mx) for ls in lses]
    den = es[0] + es[1] + es[2]
    for g, o_ref in enumerate((o0_ref, o1_ref, o2_ref)):
        yb = (es[g] / den) * o_ref[0]
        mix_ref[:, cw + g * LANES:cw + (g + 1) * LANES] = yb.astype(_BF16)
    mix_ref[:, cw + 3 * LANES:] = yc_ref[0]

    x1 = x_ref[0] + mod[2:3] * jnp.dot(mix_ref[...], wo_ref[...], preferred_element_type=_F32)

    h = _modulated_norm(x1, g2_ref[...], mod[3:4], mod[4:5]).astype(_BF16)
    ffn = None
    for c0, c1 in ff_chunks:
        gate = jnp.dot(h, wg_ref[:, c0:c1], preferred_element_type=_F32)
        up = jnp.dot(h, wu_ref[:, c0:c1], preferred_element_type=_F32)
        act = ((gate * (1.0 / (1.0 + jnp.exp(-gate)))) * up).astype(_BF16)
        part = jnp.dot(act, wd_ref[c0:c1, :], preferred_element_type=_F32)
        ffn = part if ffn is None else ffn + part
    y = x1 + mod[5:6] * ffn
    if final:
        ms = jnp.mean(y * y, axis=-1, keepdims=True)
        y = (y * lax.rsqrt(ms + EPS)) * fg_ref[...]
    out_ref[0] = y


def _ff_chunks(dff):
    mxu_k = 2 * LANES
    cut = (dff // 2 + mxu_k - 1) // mxu_k * mxu_k
    return ((0, cut), (cut, dff)) if 0 < cut < dff else ((0, dff),)


def _token_mixers_out(x, mod, l, ya, outs, lses, yc, w_out_p,
                      g2, wg, wu, wd, final_g, final, tm):
    b, s, d = x.shape
    dff = wg.shape[-1]
    cw = 2 * LANES
    tok = lambda width: pl.BlockSpec((1, tm, width), lambda bi, i: (bi, i, 0))
    const = lambda shape: pl.BlockSpec((None,) + shape, lambda bi, i: (l,) + (0,) * len(shape),
                                       pipeline_mode=pl.Buffered(1))
    return pl.pallas_call(
        functools.partial(_token_kernel, final=final, ff_chunks=_ff_chunks(dff)),
        out_shape=jax.ShapeDtypeStruct((b, s, d), _F32),
        grid=(b, s // tm),
        in_specs=[
            tok(d),
            pl.BlockSpec((1, 1, N_MOD, d), lambda bi, i: (l, bi, 0, 0)),
            tok(cw),
            tok(LANES), tok(LANES), tok(LANES),
            tok(LANES), tok(LANES), tok(LANES),
            tok(WIN_GROUP * LANES),
            const((d, d)),
            const((1, d)),
            const((d, dff)), const((d, dff)), const((dff, d)),
            pl.BlockSpec((1, d), lambda bi, i: (0, 0)),
        ],
        out_specs=tok(d),
        scratch_shapes=[pltpu.VMEM((tm, d), _BF16)],
        compiler_params=_cparams(("arbitrary", "arbitrary")),
        name="token_mix_ffn",
    )(x, mod, ya, *outs, *lses, yc, w_out_p, g2, wg, wu, wd, final_g)


def _in_proj_columns(d_model):
    pool_w = d_model // 4
    dil_w = 2 * len(DIL_PAIRS) * HEAD_DIM
    o_qb = pool_w
    o_kb = o_qb + dil_w
    o_vb = o_kb + dil_w
    o_qc = o_vb + dil_w
    o_kc = o_qc + 2 * WIN_GROUP * HEAD_DIM
    o_vc = o_kc + 2 * HEAD_DIM
    cols = list(range(pool_w))
    for g in range(len(DIL_PAIRS)):
        for base in (o_qb, o_kb, o_vb):
            cols += range(base + g * LANES, base + (g + 1) * LANES)
    win_heads = []
    for t in range(WIN_GROUP):
        for j in range(2):
            win_heads.append(j * WIN_GROUP + t)
            cols += range(o_qc + (j * WIN_GROUP + t) * HEAD_DIM,
                          o_qc + (j * WIN_GROUP + t + 1) * HEAD_DIM)
    cols += range(o_kc, o_kc + LANES)
    cols += range(o_vc, o_vc + LANES)
    qscale = np.ones((len(cols),), np.float32)
    for g in range(len(DIL_PAIRS)):
        qscale[pool_w + 3 * g * LANES:pool_w + (3 * g + 1) * LANES] = HEAD_DIM ** -0.5
    q0 = pool_w + 3 * len(DIL_PAIRS) * LANES
    qscale[q0:q0 + WIN_GROUP * LANES] = HEAD_DIM ** -0.5 * LOG2E
    return np.asarray(cols), qscale, win_heads


def _take_runs(w, idx, axis):
    idx = np.asarray(idx)
    cuts = [0] + [i for i in range(1, len(idx)) if idx[i] != idx[i - 1] + 1] + [len(idx)]
    parts = [lax.slice_in_dim(w, int(idx[a]), int(idx[b - 1]) + 1, axis=axis)
             for a, b in zip(cuts[:-1], cuts[1:])]
    return jnp.concatenate(parts, axis=axis)


def kernel(x, c, norm1_g, norm2_g, w_ada, b_ada, w_in, w_pool, pool_scale, sink_logit,
           w_out, w_gate, w_up, w_down, final_g):
    b, s, d = x.shape
    depth = w_in.shape[0]
    assert HEAD_DIM ** -0.5 == 0.125
    slopes = _alibi_slopes(2 * WIN_GROUP + 2 * len(DIL_PAIRS))
    slopes_win = tuple(float(v) for v in slopes[:2 * WIN_GROUP])
    slopes_dil = slopes[2 * WIN_GROUP:]
    cols, qscale, win_heads = _in_proj_columns(d)
    pool_w = d // 4
    dil_w = 2 * len(DIL_PAIRS) * HEAD_DIM
    mix_rows = np.concatenate([
        np.arange(pool_w + dil_w),
        np.concatenate([pool_w + dil_w + h * HEAD_DIM + np.arange(HEAD_DIM) for h in win_heads]),
    ])
    n_grp = len(POOL_WINDOWS)
    pg = pool_w // n_grp

    mod = _modulation(c, w_ada, b_ada).reshape(depth, b, N_MOD, d)
    w_in_p = (_take_runs(w_in, cols, 2) * qscale).astype(_BF16)
    w_out_p = _take_runs(w_out, mix_rows, 1).astype(_BF16)
    zpad = lambda n: jnp.zeros((depth, pg, n * pg), _F32)
    wpool_bd = jnp.concatenate(
        [jnp.concatenate([zpad(g), w_pool[:, g], zpad(n_grp - 1 - g)], axis=2)
         for g in range(n_grp)], axis=1).astype(_BF16)
    wg, wu, wd = w_gate.astype(_BF16), w_up.astype(_BF16), w_down.astype(_BF16)
    g1, g2 = norm1_g.reshape(depth, 1, d), norm2_g.reshape(depth, 1, d)
    ps = pool_scale.reshape(depth, 1, pool_w)
    tm = 512
    for l in range(depth):
        (ya, q0, k0, v0, q1, k1, v1, q2, k2, v2, qc, kc, vc) = _in_proj(
            x, mod, l, g1, w_in_p, wpool_bd, ps, 2 * tm)
        outs, lses = [], []
        for g, ((_, dil), (q, k, v)) in enumerate(zip(
                DIL_PAIRS, ((q0, k0, v0), (q1, k1, v1), (q2, k2, v2)))):
            sl = tuple(float(v_) for v_ in slopes_dil[2 * g:2 * g + 2])
            o, ls = _band_attn(q.reshape(b, s, LANES), k.reshape(b, s, LANES),
                               v.reshape(b, s, LANES), dil, sl)
            outs.append(o)
            lses.append(ls)
        yc = _win_attn(sink_logit, l, qc, kc, vc, slopes_win, min(s, 4096))
        x = _token_mixers_out(x, mod, l, ya, outs, lses, yc, w_out_p, g2, wg, wu, wd,
                              final_g.reshape(1, d), l == depth - 1, tm)
    return x
```

```python
import functools
import math

import jax
import jax.numpy as jnp
import numpy as np
from jax import lax
from jax.experimental import pallas as pl
from jax.experimental.pallas import tpu as pltpu

LANES = 128
HEAD_DIM = 64
HALF = LANES // 2
POOL_WINDOWS = (2, 4, 8, 16)
POOL_HALO = 8
DIL_PAIRS = ((128, 1), (512, 4), (2048, 16))
DIL_RADIUS = 64
WIN_RADIUS = 128
WIN_GROUP = 3
N_MOD = 6
EPS = 1e-6
NEG = -1e30
QBLK = 128
ONES_ROWS = 16
LOG2E = math.log2(math.e)
BAND_UNROLL = 5
VMEM_LIMIT = 56 * 1024 * 1024

_F32 = jnp.float32
_BF16 = jnp.bfloat16


def _alibi_slopes(n):
    i = np.arange(1, n + 1, dtype=np.float32)
    return np.exp2(np.float32(-8.0) * i / np.float32(n)).astype(np.float32)


def _cparams(sem):
    return pltpu.CompilerParams(dimension_semantics=sem, vmem_limit_bytes=VMEM_LIMIT)


def _mod_kernel(c_ref, w_ref, b_ref, o_ref):
    c = c_ref[...]
    act = (c * (1.0 / (1.0 + jnp.exp(-c)))).astype(_BF16)
    w = w_ref[0].astype(_BF16)
    o_ref[0] = jnp.dot(act, w, preferred_element_type=_F32) + b_ref[0]


def _modulation(c, w_ada, b_ada):
    depth, d, nd = w_ada.shape
    b = c.shape[0]
    tn = d
    return pl.pallas_call(
        _mod_kernel,
        out_shape=jax.ShapeDtypeStruct((depth, b, nd), _F32),
        grid=(depth, nd // tn),
        in_specs=[
            pl.BlockSpec((b, d), lambda l, j: (0, 0)),
            pl.BlockSpec((1, d, tn), lambda l, j: (l, 0, j)),
            pl.BlockSpec((1, 1, tn), lambda l, j: (l, 0, j)),
        ],
        out_specs=pl.BlockSpec((1, b, tn), lambda l, j: (l, 0, j)),
        compiler_params=_cparams(("arbitrary", "arbitrary")),
        name="modulation",
    )(c, w_ada, b_ada.reshape(depth, 1, nd))


def _modulated_norm(x, g, shift, scale):
    ms = jnp.mean(x * x, axis=-1, keepdims=True)
    return (x * lax.rsqrt(ms + EPS)) * (g * (1.0 + scale)) + shift


def _pooled_tokens(ubuf_ref, t0, s_len):
    tm = ubuf_ref.shape[0] - 2 * POOL_HALO
    cols = []
    for c in range(2):
        r_lo, r_hi = POOL_WINDOWS[2 * c] // 2, POOL_WINDOWS[2 * c + 1] // 2
        lanes = slice(c * LANES, (c + 1) * LANES)
        lane = lax.broadcasted_iota(jnp.int32, (1, LANES), 1)
        radius = jnp.where(lane < HALF, r_lo, r_hi)
        u = ubuf_ref[POOL_HALO:POOL_HALO + tm, lanes]
        wsum = u
        for k in range(1, r_hi + 1):
            pair = (ubuf_ref[POOL_HALO - k:POOL_HALO - k + tm, lanes]
                    + ubuf_ref[POOL_HALO + k:POOL_HALO + k + tm, lanes])
            wsum = wsum + (pair if k <= r_lo else jnp.where(lane < HALF, 0.0, pair))
        t = t0 + lax.broadcasted_iota(jnp.int32, (tm, LANES), 0)
        cnt = jnp.minimum(t + radius + 1, s_len) - jnp.maximum(t - radius, 0)
        cols.append((wsum / cnt.astype(_F32) - u).astype(_BF16))
    return jnp.concatenate(cols, axis=1)


def _in_proj_kernel(x_ref, xp_ref, xn_ref, mod_ref, g_ref, w_ref, wp_ref, ps_ref,
                    ya_ref, q0_ref, k0_ref, v0_ref, q1_ref, k1_ref, v1_ref,
                    q2_ref, k2_ref, v2_ref, qc_ref, kc_ref, vct_ref, zs_ref, ubuf_ref,
                    *, s_len):
    tm = x_ref.shape[1]
    i = pl.program_id(1)
    cw = 2 * LANES
    mod = mod_ref[0, 0]
    norm = lambda rows: _modulated_norm(rows, g_ref[...], mod[0:1], mod[1:2]).astype(_BF16)
    h = norm(x_ref[0])

    def proj(lhs, t0, t1):
        return jnp.dot(lhs, w_ref[:, t0 * LANES:t1 * LANES], preferred_element_type=_F32)

    halo = proj(norm(jnp.concatenate([xp_ref[0], xn_ref[0]], axis=0)), 0, 2)
    ubuf_ref[0:POOL_HALO] = jnp.where(i > 0, halo[:POOL_HALO], 0.0)
    ubuf_ref[POOL_HALO:POOL_HALO + tm] = proj(h, 0, 2)
    ubuf_ref[POOL_HALO + tm:] = jnp.where(i < pl.num_programs(1) - 1, halo[POOL_HALO:], 0.0)
    pooled = _pooled_tokens(ubuf_ref, i * tm, s_len)

    z_lo, z_hi = proj(h, 2, 8), proj(h, 8, 16)

    def tile(i):
        z, j = (z_lo, i - 2) if i < 8 else (z_hi, i - 8)
        return z[:, j * LANES:(j + 1) * LANES]

    for j, ref in enumerate((q0_ref, k0_ref, v0_ref)):
        ref[0] = tile(2 + j).astype(_BF16)
    for refs, (_, dil), base in (((q1_ref, k1_ref, v1_ref), DIL_PAIRS[1], 5),
                                 ((q2_ref, k2_ref, v2_ref), DIL_PAIRS[2], 8)):
        rows = tm // dil
        for j, ref in enumerate(refs):
            zs_ref[base - 5 + j] = tile(base + j)
            for r in range(dil):
                ref[0, r] = zs_ref[base - 5 + j, pl.ds(r, rows, stride=dil), :].astype(_BF16)
    for t in range(WIN_GROUP):
        qc_ref[0, t] = tile(11 + t).astype(_BF16)
    kc_ref[0] = tile(14).astype(_BF16)
    vct_ref[0] = tile(15).T.astype(_BF16)
    ya = jnp.dot(pooled, wp_ref[...], preferred_element_type=_F32) * ps_ref[...]
    ya_ref[0] = ya.astype(_BF16)


def _in_proj(x, mod, l, g, w_in_p, wpool_bd, pool_scale, tm):
    b, s, d = x.shape
    n_in = w_in_p.shape[-1]
    cw = 2 * LANES
    hb = tm // POOL_HALO
    nhb = s // POOL_HALO
    d1, d2 = DIL_PAIRS[1][1], DIL_PAIRS[2][1]
    nat = lambda: pl.BlockSpec((1, tm, LANES), lambda bi, i: (bi, i, 0))
    res = lambda dil: pl.BlockSpec((1, dil, tm // dil, LANES), lambda bi, i: (bi, 0, i, 0))
    full = lambda shape: pl.BlockSpec((None,) + shape, lambda bi, i: (l,) + (0,) * len(shape))
    sds = jax.ShapeDtypeStruct
    out_shape = (
        [sds((b, s, cw), _BF16)]
        + [sds((b, s, LANES), _BF16)] * 3
        + [sds((b, d1, s // d1, LANES), _BF16)] * 3
        + [sds((b, d2, s // d2, LANES), _BF16)] * 3
        + [sds((b, WIN_GROUP, s, LANES), _BF16)]
        + [sds((b, s, LANES), _BF16), sds((b, LANES, s), _BF16)]
    )
    out_specs = (
        [pl.BlockSpec((1, tm, 2 * LANES), lambda bi, i: (bi, i, 0))]
        + [nat() for _ in range(3)]
        + [res(d1) for _ in range(3)]
        + [res(d2) for _ in range(3)]
        + [pl.BlockSpec((1, WIN_GROUP, tm, LANES), lambda bi, i: (bi, 0, i, 0))]
        + [nat(), pl.BlockSpec((1, LANES, tm), lambda bi, i: (bi, 0, i))]
    )
    return pl.pallas_call(
        functools.partial(_in_proj_kernel, s_len=s),
        out_shape=out_shape,
        grid=(b, s // tm),
        in_specs=[
            pl.BlockSpec((1, tm, d), lambda bi, i: (bi, i, 0)),
            pl.BlockSpec((1, POOL_HALO, d), lambda bi, i: (bi, jnp.maximum(i * hb - 1, 0), 0)),
            pl.BlockSpec((1, POOL_HALO, d),
                         lambda bi, i: (bi, jnp.minimum((i + 1) * hb, nhb - 1), 0)),
            pl.BlockSpec((1, 1, N_MOD, d), lambda bi, i: (l, bi, 0, 0)),
            full((1, d)),
            full((d, n_in)),
            full((cw, cw)),
            full((1, cw)),
        ],
        out_specs=out_specs,
        scratch_shapes=[pltpu.VMEM((6, tm, LANES), _F32),
                        pltpu.VMEM((tm + 2 * POOL_HALO, cw), _F32)],
        compiler_params=_cparams(("arbitrary", "arbitrary")),
        name="in_proj",
    )(x, x, x, mod, g, w_in_p, wpool_bd, pool_scale)


def _band_bias_t(shape, col_heads, slopes, dist_scale, radius, offset):
    row = lax.broadcasted_iota(jnp.int32, shape, 0)
    col = lax.broadcasted_iota(jnp.int32, shape, 1)
    dist = jnp.abs(row - offset - (col % QBLK))
    slope = jnp.full(shape, slopes[col_heads[0]] * dist_scale, _F32)
    for i, hd in enumerate(col_heads[1:], start=1):
        slope = jnp.where(col >= i * QBLK, np.float32(slopes[hd] * dist_scale), slope)
    return jnp.where(dist <= radius, (-slope * dist.astype(_F32)) * LOG2E, NEG)


def _window(i, nblk, radius, width):
    seq = nblk * QBLK
    ws = jnp.clip(i * QBLK - radius, 0, seq - width)
    variant = jnp.where(i == 0, 0, jnp.where(i == nblk - 1, 2, 1))
    return ws, variant


def _trans_b_dot(a, b):
    return lax.dot_general(a, b, (((1,), (1,)), ((), ())), preferred_element_type=_F32)


def _band_attn_kernel(q_ref, k_ref, v_ref, o_ref, lse_ref, bias_ref, *scratch, dil, slopes):
    s_len = q_ref.shape[1]
    n = s_len // dil
    nblk = n // QBLK
    width = QBLK + 2 * DIL_RADIUS
    cols = 2 * QBLK

    @pl.when(pl.program_id(0) == 0)
    def _():
        for var, off in enumerate((0, DIL_RADIUS, width - QBLK)):
            bias_ref[var] = _band_bias_t((width, cols), (0, 1), slopes, float(dil),
                                         DIL_RADIUS, off)

    lo = lax.broadcasted_iota(jnp.int32, (QBLK, LANES), 1) < HALF
    s_bufs, e_bufs, m_bufs = scratch[0:2], scratch[2:4], scratch[4:6]
    ntot = s_len // QBLK

    def locate(blk):
        r = blk // nblk
        i = blk % nblk
        ws, variant = _window(i, nblk, DIL_RADIUS, width)
        return r, i, pl.multiple_of(r * n + ws, DIL_RADIUS), variant

    def scores(blk, par):
        _, _, kstart, variant = locate(blk)
        qstart = pl.multiple_of(blk * QBLK, QBLK)
        q = q_ref[0, pl.ds(qstart, QBLK), :]
        kw = k_ref[0, pl.ds(kstart, width), :]
        zero = jnp.zeros_like(q)
        qs = jnp.concatenate([jnp.where(lo, q, zero), jnp.where(lo, zero, q)], axis=0)
        s_bufs[par][...] = _trans_b_dot(kw, qs) + bias_ref[variant]

    def numerators(par):
        sc = s_bufs[par][...]
        m = jnp.max(sc, axis=0, keepdims=True)
        e_bufs[par][...] = jnp.exp2(sc - m).astype(_BF16)
        m_bufs[par][...] = jnp.broadcast_to(m, (8, cols))

    def outputs(blk, par):
        r, i, kstart, _ = locate(blk)
        vt = v_ref[0, pl.ds(kstart, width), :].T
        lhs = jnp.concatenate([vt, jnp.ones((ONES_ROWS, width), _BF16)], axis=0)
        o = jnp.dot(lhs, e_bufs[par][...], preferred_element_type=_F32)
        den = o[LANES:LANES + 1]
        lse_row = m_bufs[par][0:1] * (1.0 / LOG2E) + jnp.log(den)
        res = jnp.concatenate(
            [o[h * HEAD_DIM:(h + 1) * HEAD_DIM, h * QBLK:(h + 1) * QBLK]
             / den[:, h * QBLK:(h + 1) * QBLK] for h in range(2)], axis=0)
        lse2 = jnp.concatenate(
            [jnp.broadcast_to(lse_row[:, h * QBLK:(h + 1) * QBLK], (HEAD_DIM, QBLK))
             for h in range(2)], axis=0)
        out, lse = res.T, lse2.T
        if dil == 1:
            qstart = pl.multiple_of(blk * QBLK, QBLK)
            o_ref[0, pl.ds(qstart, QBLK), :] = out.astype(_BF16)
            lse_ref[0, pl.ds(qstart, QBLK), :] = lse
        else:
            tstart = i * (QBLK * dil) + r
            scratch[6][pl.ds(tstart, QBLK, stride=dil), :] = out
            lse_ref[0, pl.ds(tstart, QBLK, stride=dil), :] = lse

    scores(0, 0)
    scores(1, 1)
    numerators(0)

    def body(p, carry):
        it = 2 * p
        scores(it, 0)
        numerators(1)
        outputs(it - 2, 0)
        scores(it + 1, 1)
        numerators(0)
        outputs(it - 1, 1)
        return carry

    lax.fori_loop(1, ntot // 2, body, 0, unroll=BAND_UNROLL)
    numerators(1)
    outputs(ntot - 2, 0)
    outputs(ntot - 1, 1)
    if dil != 1:
        o_ref[0] = scratch[6][...].astype(_BF16)


def _band_attn(q, k, v, dil, slopes):
    b, s, _ = q.shape
    width = QBLK + 2 * DIL_RADIUS
    spec = lambda: pl.BlockSpec((1, s, LANES), lambda bi: (bi, 0, 0))
    cols = 2 * QBLK
    scratch = ([pltpu.VMEM((3, width, cols), _F32)]
               + [pltpu.VMEM((width, cols), _F32)] * 2
               + [pltpu.VMEM((width, cols), _BF16)] * 2
               + [pltpu.VMEM((8, cols), _F32)] * 2)
    if dil != 1:
        scratch.append(pltpu.VMEM((s, LANES), _F32))
    return pl.pallas_call(
        functools.partial(_band_attn_kernel, dil=dil, slopes=slopes),
        out_shape=[jax.ShapeDtypeStruct((b, s, LANES), _BF16),
                   jax.ShapeDtypeStruct((b, s, LANES), _F32)],
        grid=(b,),
        in_specs=[spec(), spec(), spec()],
        out_specs=[spec(), spec()],
        scratch_shapes=scratch,
        compiler_params=_cparams(("arbitrary",)),
        name=f"band_attn_d{dil}",
    )(q, k, v)


def _win_attn_kernel(sink_ref, q_ref, k_ref, vt_ref, o_ref, bias_ref,
                     s0_ref, s1_ref, e0_ref, e1_ref, t0_ref, t1_ref, *, slopes, layer):
    s_len = k_ref.shape[1]
    tc = q_ref.shape[2]
    nblk = s_len // QBLK
    nloc = tc // QBLK
    width = QBLK + 2 * WIN_RADIUS
    cols = WIN_GROUP * QBLK
    s_bufs, e_bufs, t_bufs = (s0_ref, s1_ref), (e0_ref, e1_ref), (t0_ref, t1_ref)

    @pl.when((pl.program_id(0) == 0) & (pl.program_id(1) == 0))
    def _():
        for var, off in enumerate((0, WIN_RADIUS, width - QBLK)):
            for j in range(2):
                heads = tuple(j * WIN_GROUP + t for t in range(WIN_GROUP))
                bias_ref[var, j] = _band_bias_t((width, cols), heads, slopes, 1.0,
                                                WIN_RADIUS, off)

    qlane = lax.broadcasted_iota(jnp.int32, (1, cols), 1)
    klo = lax.broadcasted_iota(jnp.int32, (width, LANES), 1) < HALF
    base = pl.program_id(1) * nloc

    def sink_row(j):
        row = jnp.full((1, cols), sink_ref[layer, j * WIN_GROUP] * LOG2E, _F32)
        for t in range(1, WIN_GROUP):
            row = jnp.where(qlane >= t * QBLK, sink_ref[layer, j * WIN_GROUP + t] * LOG2E, row)
        return row

    def kstart_of(bl):
        ws, variant = _window(base + bl, nblk, WIN_RADIUS, width)
        return pl.multiple_of(ws, QBLK), variant

    def scores(bl, par):
        kstart, variant = kstart_of(bl)
        qstart = pl.multiple_of(bl * QBLK, QBLK)
        q = q_ref[0, :, pl.ds(qstart, QBLK), :].reshape(cols, LANES)
        kw = k_ref[0, pl.ds(kstart, width), :]
        zero = jnp.zeros_like(kw)
        for j in range(2):
            kh = jnp.where(klo, kw, zero) if j == 0 else jnp.where(klo, zero, kw)
            s_bufs[par][j] = _trans_b_dot(kh, q) + bias_ref[variant, j]

    def numerators(par):
        for j in range(2):
            sc = s_bufs[par][j]
            sink = sink_row(j)
            m = jnp.maximum(jnp.max(sc, axis=0, keepdims=True), sink)
            e_bufs[par][j] = jnp.exp2(sc - m).astype(_BF16)
            t_bufs[par][j] = jnp.broadcast_to(jnp.exp2(sink - m), (8, cols))

    def outputs(bl, par):
        kstart, _ = kstart_of(bl)
        qstart = pl.multiple_of(bl * QBLK, QBLK)
        vt = vt_ref[0, :, pl.ds(kstart, width)]
        ones = jnp.ones((ONES_ROWS, width), _BF16)
        res = []
        for j in range(2):
            lhs = jnp.concatenate([vt[j * HEAD_DIM:(j + 1) * HEAD_DIM], ones], axis=0)
            o = jnp.dot(lhs, e_bufs[par][j], preferred_element_type=_F32)
            den = o[HEAD_DIM:HEAD_DIM + 1] + t_bufs[par][j][0:1]
            res.append(o[:HEAD_DIM] / den)
        for t in range(WIN_GROUP):
            both = jnp.concatenate([r[:, t * QBLK:(t + 1) * QBLK] for r in res], axis=0)
            o_ref[0, pl.ds(qstart, QBLK), t * LANES:(t + 1) * LANES] = both.T.astype(_BF16)

    scores(0, 0)
    scores(1, 1)
    numerators(0)

    def body(p, carry):
        it = 2 * p
        scores(it, 0)
        numerators(1)
        outputs(it - 2, 0)
        scores(it + 1, 1)
        numerators(0)
        outputs(it - 1, 1)
        return carry

    lax.fori_loop(1, nloc // 2, body, 0, unroll=3)
    numerators(1)
    outputs(nloc - 2, 0)
    outputs(nloc - 1, 1)


def _win_attn(sink, layer, q, k, vt, slopes, tc):
    b, _, s, _ = q.shape
    width = QBLK + 2 * WIN_RADIUS
    cols = WIN_GROUP * QBLK
    assert (tc // QBLK) % 2 == 0 and tc // QBLK >= 4
    return pl.pallas_call(
        functools.partial(_win_attn_kernel, slopes=slopes, layer=layer),
        out_shape=jax.ShapeDtypeStruct((b, s, WIN_GROUP * LANES), _BF16),
        grid=(b, s // tc),
        in_specs=[
            pl.BlockSpec(memory_space=pltpu.SMEM),
            pl.BlockSpec((1, WIN_GROUP, tc, LANES), lambda bi, i: (bi, 0, i, 0)),
            pl.BlockSpec((1, s, LANES), lambda bi, i: (bi, 0, 0)),
            pl.BlockSpec((1, LANES, s), lambda bi, i: (bi, 0, 0)),
        ],
        out_specs=pl.BlockSpec((1, tc, WIN_GROUP * LANES), lambda bi, i: (bi, i, 0)),
        scratch_shapes=[pltpu.VMEM((3, 2, width, cols), _F32)]
                       + [pltpu.VMEM((2, width, cols), _F32)] * 2
                       + [pltpu.VMEM((2, width, cols), _BF16)] * 2
                       + [pltpu.VMEM((2, 8, cols), _F32)] * 2,
        compiler_params=_cparams(("arbitrary", "arbitrary")),
        name="win_attn",
    )(sink, q, k, vt)


def _token_kernel(x_ref, mod_ref, ya_ref,
                  o0_ref, o1_ref, o2_ref, l0_ref, l1_ref, l2_ref, yc_ref, wo_ref,
                  g2_ref, wg_ref, wu_ref, wd_ref, fg_ref,
                  out_ref, mix_ref, *, final, ff_chunks):
    cw = 2 * LANES
    mod = mod_ref[0, 0]
    mix_ref[:, :cw] = ya_ref[0]

    lses = [l0_ref[0], l1_ref[0], l2_ref[0]]
    mx = jnp.maximum(jnp.maximum(lses[0], lses[1]), lses[2])
    es = [jnp.exp(ls - mx) for ls in lses]
    den = es[0] + es[1] + es[2]
    for g, o_ref in enumerate((o0_ref, o1_ref, o2_ref)):
        yb = (es[g] / den) * o_ref[0]
        mix_ref[:, cw + g * LANES:cw + (g + 1) * LANES] = yb.astype(_BF16)
    mix_ref[:, cw + 3 * LANES:] = yc_ref[0]

    x1 = x_ref[0] + mod[2:3] * jnp.dot(mix_ref[...], wo_ref[...], preferred_element_type=_F32)

    h = _modulated_norm(x1, g2_ref[...], mod[3:4], mod[4:5]).astype(_BF16)
    ffn = None
    for c0, c1 in ff_chunks:
        gate = jnp.dot(h, wg_ref[:, c0:c1], preferred_element_type=_F32)
        up = jnp.dot(h, wu_ref[:, c0:c1], preferred_element_type=_F32)
        act = ((gate * (1.0 / (1.0 + jnp.exp(-gate)))) * up).astype(_BF16)
        part = jnp.dot(act, wd_ref[c0:c1, :], preferred_element_type=_F32)
        ffn = part if ffn is None else ffn + part
    y = x1 + mod[5:6] * ffn
    if final:
        ms = jnp.mean(y * y, axis=-1, keepdims=True)
        y = (y * lax.rsqrt(ms + EPS)) * fg_ref[...]
    out_ref[0] = y


def _ff_chunks(dff):
    mxu_k = 2 * LANES
    cut = (dff // 2 + mxu_k - 1) // mxu_k * mxu_k
    return ((0, cut), (cut, dff)) if 0 < cut < dff else ((0, dff),)


def _token_mixers_out(x, mod, l, ya, outs, lses, yc, w_out_p,
                      g2, wg, wu, wd, final_g, final, tm):
    b, s, d = x.shape
    dff = wg.shape[-1]
    cw = 2 * LANES
    tok = lambda width: pl.BlockSpec((1, tm, width), lambda bi, i: (bi, i, 0))
    const = lambda shape: pl.BlockSpec((None,) + shape, lambda bi, i: (l,) + (0,) * len(shape),
                                       pipeline_mode=pl.Buffered(1))
    return pl.pallas_call(
        functools.partial(_token_kernel, final=final, ff_chunks=_ff_chunks(dff)),
        out_shape=jax.ShapeDtypeStruct((b, s, d), _F32),
        grid=(b, s // tm),
        in_specs=[
            tok(d),
            pl.BlockSpec((1, 1, N_MOD, d), lambda bi, i: (l, bi, 0, 0)),
            tok(cw),
            tok(LANES), tok(LANES), tok(LANES),
            tok(LANES), tok(LANES), tok(LANES),
            tok(WIN_GROUP * LANES),
            const((d, d)),
            const((1, d)),
            const((d, dff)), const((d, dff)), const((dff, d)),
            pl.BlockSpec((1, d), lambda bi, i: (0, 0)),
        ],
        out_specs=tok(d),
        scratch_shapes=[pltpu.VMEM((tm, d), _BF16)],
        compiler_params=_cparams(("arbitrary", "arbitrary")),
        name="token_mix_ffn",
    )(x, mod, ya, *outs, *lses, yc, w_out_p, g2, wg, wu, wd, final_g)


def _in_proj_columns(d_model):
    pool_w = d_model // 4
    dil_w = 2 * len(DIL_PAIRS) * HEAD_DIM
    o_qb = pool_w
    o_kb = o_qb + dil_w
    o_vb = o_kb + dil_w
    o_qc = o_vb + dil_w
    o_kc = o_qc + 2 * WIN_GROUP * HEAD_DIM
    o_vc = o_kc + 2 * HEAD_DIM
    cols = list(range(pool_w))
    for g in range(len(DIL_PAIRS)):
        for base in (o_qb, o_kb, o_vb):
            cols += range(base + g * LANES, base + (g + 1) * LANES)
    win_heads = []
    for t in range(WIN_GROUP):
        for j in range(2):
            win_heads.append(j * WIN_GROUP + t)
            cols += range(o_qc + (j * WIN_GROUP + t) * HEAD_DIM,
                          o_qc + (j * WIN_GROUP + t + 1) * HEAD_DIM)
    cols += range(o_kc, o_kc + LANES)
    cols += range(o_vc, o_vc + LANES)
    qscale = np.ones((len(cols),), np.float32)
    for g in range(len(DIL_PAIRS)):
        qscale[pool_w + 3 * g * LANES:pool_w + (3 * g + 1) * LANES] = HEAD_DIM ** -0.5 * LOG2E
    q0 = pool_w + 3 * len(DIL_PAIRS) * LANES
    qscale[q0:q0 + WIN_GROUP * LANES] = HEAD_DIM ** -0.5 * LOG2E
    return np.asarray(cols), qscale, win_heads


def _take_runs(w, idx, axis):
    idx = np.asarray(idx)
    cuts = [0] + [i for i in range(1, len(idx)) if idx[i] != idx[i - 1] + 1] + [len(idx)]
    parts = [lax.slice_in_dim(w, int(idx[a]), int(idx[b - 1]) + 1, axis=axis)
             for a, b in zip(cuts[:-1], cuts[1:])]
    return jnp.concatenate(parts, axis=axis)


def kernel(x, c, norm1_g, norm2_g, w_ada, b_ada, w_in, w_pool, pool_scale, sink_logit,
           w_out, w_gate, w_up, w_down, final_g):
    b, s, d = x.shape
    depth = w_in.shape[0]
    slopes = _alibi_slopes(2 * WIN_GROUP + 2 * len(DIL_PAIRS))
    slopes_win = tuple(float(v) for v in slopes[:2 * WIN_GROUP])
    slopes_dil = slopes[2 * WIN_GROUP:]
    cols, qscale, win_heads = _in_proj_columns(d)
    pool_w = d // 4
    dil_w = 2 * len(DIL_PAIRS) * HEAD_DIM
    mix_rows = np.concatenate([
        np.arange(pool_w + dil_w),
        np.concatenate([pool_w + dil_w + h * HEAD_DIM + np.arange(HEAD_DIM) for h in win_heads]),
    ])
    n_grp = len(POOL_WINDOWS)
    pg = pool_w // n_grp

    mod = _modulation(c, w_ada, b_ada).reshape(depth, b, N_MOD, d)
    w_in_p = (_take_runs(w_in, cols, 2) * qscale).astype(_BF16)
    w_out_p = _take_runs(w_out, mix_rows, 1).astype(_BF16)
    zpad = lambda n: jnp.zeros((depth, pg, n * pg), _F32)
    wpool_bd = jnp.concatenate(
        [jnp.concatenate([zpad(g), w_pool[:, g], zpad(n_grp - 1 - g)], axis=2)
         for g in range(n_grp)], axis=1).astype(_BF16)
    wg, wu, wd = w_gate.astype(_BF16), w_up.astype(_BF16), w_down.astype(_BF16)
    g1, g2 = norm1_g.reshape(depth, 1, d), norm2_g.reshape(depth, 1, d)
    ps = pool_scale.reshape(depth, 1, pool_w)
    tm = 512
    for l in range(depth):
        (ya, q0, k0, v0, q1, k1, v1, q2, k2, v2, qc, kc, vc) = _in_proj(
            x, mod, l, g1, w_in_p, wpool_bd, ps, 2 * tm)
        outs, lses = [], []
        for g, ((_, dil), (q, k, v)) in enumerate(zip(
                DIL_PAIRS, ((q0, k0, v0), (q1, k1, v1), (q2, k2, v2)))):
            sl = tuple(float(v_) for v_ in slopes_dil[2 * g:2 * g + 2])
            o, ls = _band_attn(q.reshape(b, s, LANES), k.reshape(b, s, LANES),
                               v.reshape(b, s, LANES), dil, sl)
            outs.append(o)
            lses.append(ls)
        yc = _win_attn(sink_logit, l, qc, kc, vc, slopes_win, min(s, 4096))
        x = _token_mixers_out(x, mod, l, ya, outs, lses, yc, w_out_p, g2, wg, wu, wd,
                              final_g.reshape(1, d), l == depth - 1, tm)
    return x
```

```python
import functools
import math

import jax
import jax.numpy as jnp
import numpy as np
from jax import lax
from jax.experimental import pallas as pl
from jax.experimental.pallas import tpu as pltpu

LANES = 128
HEAD_DIM = 64
HALF = LANES // 2
POOL_WINDOWS = (2, 4, 8, 16)
POOL_HALO = 8
DIL_PAIRS = ((128, 1), (512, 4), (2048, 16))
DIL_RADIUS = 64
WIN_RADIUS = 128
WIN_GROUP = 3
N_MOD = 6
EPS = 1e-6
NEG = -1e30
QBLK = 128
ONES_ROWS = 16
LOG2E = math.log2(math.e)
BAND_UNROLL = 5
VMEM_LIMIT = 56 * 1024 * 1024

_F32 = jnp.float32
_BF16 = jnp.bfloat16


def _alibi_slopes(n):
    i = np.arange(1, n + 1, dtype=np.float32)
    return np.exp2(np.float32(-8.0) * i / np.float32(n)).astype(np.float32)


def _cparams(sem):
    return pltpu.CompilerParams(dimension_semantics=sem, vmem_limit_bytes=VMEM_LIMIT)


def _mod_kernel(c_ref, w_ref, b_ref, o_ref):
    c = c_ref[...]
    act = (c * (1.0 / (1.0 + jnp.exp(-c)))).astype(_BF16)
    w = w_ref[0].astype(_BF16)
    o_ref[0] = jnp.dot(act, w, preferred_element_type=_F32) + b_ref[0]


def _modulation(c, w_ada, b_ada):
    depth, d, nd = w_ada.shape
    b = c.shape[0]
    tn = d
    return pl.pallas_call(
        _mod_kernel,
        out_shape=jax.ShapeDtypeStruct((depth, b, nd), _F32),
        grid=(depth, nd // tn),
        in_specs=[
            pl.BlockSpec((b, d), lambda l, j: (0, 0)),
            pl.BlockSpec((1, d, tn), lambda l, j: (l, 0, j)),
            pl.BlockSpec((1, 1, tn), lambda l, j: (l, 0, j)),
        ],
        out_specs=pl.BlockSpec((1, b, tn), lambda l, j: (l, 0, j)),
        compiler_params=_cparams(("arbitrary", "arbitrary")),
        name="modulation",
    )(c, w_ada, b_ada.reshape(depth, 1, nd))


def _modulated_norm(x, g, shift, scale):
    ms = jnp.mean(x * x, axis=-1, keepdims=True)
    return (x * lax.rsqrt(ms + EPS)) * (g * (1.0 + scale)) + shift


def _pooled_tokens(ubuf_ref, t0, s_len):
    tm = ubuf_ref.shape[0] - 2 * POOL_HALO
    cols = []
    for c in range(2):
        r_lo, r_hi = POOL_WINDOWS[2 * c] // 2, POOL_WINDOWS[2 * c + 1] // 2
        lanes = slice(c * LANES, (c + 1) * LANES)
        lane = lax.broadcasted_iota(jnp.int32, (1, LANES), 1)
        radius = jnp.where(lane < HALF, r_lo, r_hi)
        u = ubuf_ref[POOL_HALO:POOL_HALO + tm, lanes]
        wsum = u
        for k in range(1, r_hi + 1):
            pair = (ubuf_ref[POOL_HALO - k:POOL_HALO - k + tm, lanes]
                    + ubuf_ref[POOL_HALO + k:POOL_HALO + k + tm, lanes])
            wsum = wsum + (pair if k <= r_lo else jnp.where(lane < HALF, 0.0, pair))
        t = t0 + lax.broadcasted_iota(jnp.int32, (tm, LANES), 0)
        cnt = jnp.minimum(t + radius + 1, s_len) - jnp.maximum(t - radius, 0)
        cols.append((wsum / cnt.astype(_F32) - u).astype(_BF16))
    return jnp.concatenate(cols, axis=1)


def _in_proj_kernel(x_ref, xp_ref, xn_ref, mod_ref, g_ref, w_ref, wp_ref, ps_ref,
                    ya_ref, q0_ref, k0_ref, v0_ref, q1_ref, k1_ref, v1_ref,
                    q2_ref, k2_ref, v2_ref, qc_ref, kc_ref, vct_ref, zs_ref, ubuf_ref,
                    *, s_len):
    tm = x_ref.shape[1]
    i = pl.program_id(1)
    cw = 2 * LANES
    mod = mod_ref[0, 0]
    norm = lambda rows: _modulated_norm(rows, g_ref[...], mod[0:1], mod[1:2]).astype(_BF16)
    h = norm(x_ref[0])

    def proj(lhs, t0, t1):
        return jnp.dot(lhs, w_ref[:, t0 * LANES:t1 * LANES], preferred_element_type=_F32)

    halo = proj(norm(jnp.concatenate([xp_ref[0], xn_ref[0]], axis=0)), 0, 2)
    ubuf_ref[0:POOL_HALO] = jnp.where(i > 0, halo[:POOL_HALO], 0.0)
    ubuf_ref[POOL_HALO:POOL_HALO + tm] = proj(h, 0, 2)
    ubuf_ref[POOL_HALO + tm:] = jnp.where(i < pl.num_programs(1) - 1, halo[POOL_HALO:], 0.0)
    pooled = _pooled_tokens(ubuf_ref, i * tm, s_len)

    z_lo, z_hi = proj(h, 2, 8), proj(h, 8, 16)

    def tile(i):
        z, j = (z_lo, i - 2) if i < 8 else (z_hi, i - 8)
        return z[:, j * LANES:(j + 1) * LANES]

    for j, ref in enumerate((q0_ref, k0_ref, v0_ref)):
        ref[0] = tile(2 + j).astype(_BF16)
    for refs, (_, dil), base in (((q1_ref, k1_ref, v1_ref), DIL_PAIRS[1], 5),
                                 ((q2_ref, k2_ref, v2_ref), DIL_PAIRS[2], 8)):
        rows = tm // dil
        for j, ref in enumerate(refs):
            zs_ref[base - 5 + j] = tile(base + j)
            for r in range(dil):
                ref[0, r] = zs_ref[base - 5 + j, pl.ds(r, rows, stride=dil), :].astype(_BF16)
    for t in range(WIN_GROUP):
        qc_ref[0, t] = tile(11 + t).astype(_BF16)
    kc_ref[0] = tile(14).astype(_BF16)
    vct_ref[0] = tile(15).T.astype(_BF16)
    ya = jnp.dot(pooled, wp_ref[...], preferred_element_type=_F32) * ps_ref[...]
    ya_ref[0] = ya.astype(_BF16)


def _in_proj(x, mod, l, g, w_in_p, wpool_bd, pool_scale, tm):
    b, s, d = x.shape
    n_in = w_in_p.shape[-1]
    cw = 2 * LANES
    hb = tm // POOL_HALO
    nhb = s // POOL_HALO
    d1, d2 = DIL_PAIRS[1][1], DIL_PAIRS[2][1]
    nat = lambda: pl.BlockSpec((1, tm, LANES), lambda bi, i: (bi, i, 0))
    res = lambda dil: pl.BlockSpec((1, dil, tm // dil, LANES), lambda bi, i: (bi, 0, i, 0))
    full = lambda shape: pl.BlockSpec((None,) + shape, lambda bi, i: (l,) + (0,) * len(shape))
    sds = jax.ShapeDtypeStruct
    out_shape = (
        [sds((b, s, cw), _BF16)]
        + [sds((b, s, LANES), _BF16)] * 3
        + [sds((b, d1, s // d1, LANES), _BF16)] * 3
        + [sds((b, d2, s // d2, LANES), _BF16)] * 3
        + [sds((b, WIN_GROUP, s, LANES), _BF16)]
        + [sds((b, s, LANES), _BF16), sds((b, LANES, s), _BF16)]
    )
    out_specs = (
        [pl.BlockSpec((1, tm, 2 * LANES), lambda bi, i: (bi, i, 0))]
        + [nat() for _ in range(3)]
        + [res(d1) for _ in range(3)]
        + [res(d2) for _ in range(3)]
        + [pl.BlockSpec((1, WIN_GROUP, tm, LANES), lambda bi, i: (bi, 0, i, 0))]
        + [nat(), pl.BlockSpec((1, LANES, tm), lambda bi, i: (bi, 0, i))]
    )
    return pl.pallas_call(
        functools.partial(_in_proj_kernel, s_len=s),
        out_shape=out_shape,
        grid=(b, s // tm),
        in_specs=[
            pl.BlockSpec((1, tm, d), lambda bi, i: (bi, i, 0)),
            pl.BlockSpec((1, POOL_HALO, d), lambda bi, i: (bi, jnp.maximum(i * hb - 1, 0), 0)),
            pl.BlockSpec((1, POOL_HALO, d),
                         lambda bi, i: (bi, jnp.minimum((i + 1) * hb, nhb - 1), 0)),
            pl.BlockSpec((1, 1, N_MOD, d), lambda bi, i: (l, bi, 0, 0)),
            full((1, d)),
            full((d, n_in)),
            full((cw, cw)),
            full((1, cw)),
        ],
        out_specs=out_specs,
        scratch_shapes=[pltpu.VMEM((6, tm, LANES), _F32),
                        pltpu.VMEM((tm + 2 * POOL_HALO, cw), _F32)],
        compiler_params=_cparams(("arbitrary", "arbitrary")),
        name="in_proj",
    )(x, x, x, mod, g, w_in_p, wpool_bd, pool_scale)


def _band_bias_t(shape, col_heads, slopes, dist_scale, radius, offset):
    row = lax.broadcasted_iota(jnp.int32, shape, 0)
    col = lax.broadcasted_iota(jnp.int32, shape, 1)
    dist = jnp.abs(row - offset - (col % QBLK))
    slope = jnp.full(shape, slopes[col_heads[0]] * dist_scale, _F32)
    for i, hd in enumerate(col_heads[1:], start=1):
        slope = jnp.where(col >= i * QBLK, np.float32(slopes[hd] * dist_scale), slope)
    return jnp.where(dist <= radius, (-slope * dist.astype(_F32)) * LOG2E, NEG)


def _window(i, nblk, radius, width):
    seq = nblk * QBLK
    ws = jnp.clip(i * QBLK - radius, 0, seq - width)
    variant = jnp.where(i == 0, 0, jnp.where(i == nblk - 1, 2, 1))
    return ws, variant


def _trans_b_dot(a, b):
    return lax.dot_general(a, b, (((1,), (1,)), ((), ())), preferred_element_type=_F32)


def _band_attn_kernel(q_ref, k_ref, v_ref, o_ref, lse_ref, bias_ref, *scratch, dil, slopes):
    s_len = q_ref.shape[1]
    n = s_len // dil
    nblk = n // QBLK
    width = QBLK + 2 * DIL_RADIUS
    cols = 2 * QBLK

    @pl.when(pl.program_id(0) == 0)
    def _():
        for var, off in enumerate((0, DIL_RADIUS, width - QBLK)):
            bias_ref[var] = _band_bias_t((width, cols), (0, 1), slopes, float(dil),
                                         DIL_RADIUS, off)

    lo = lax.broadcasted_iota(jnp.int32, (QBLK, LANES), 1) < HALF
    s_bufs, e_bufs, m_bufs = scratch[0:2], scratch[2:4], scratch[4:6]
    ntot = s_len // QBLK

    def locate(blk):
        r = blk // nblk
        i = blk % nblk
        ws, variant = _window(i, nblk, DIL_RADIUS, width)
        return r, i, pl.multiple_of(r * n + ws, DIL_RADIUS), variant

    def scores(blk, par):
        _, _, kstart, variant = locate(blk)
        qstart = pl.multiple_of(blk * QBLK, QBLK)
        q = q_ref[0, pl.ds(qstart, QBLK), :]
        kw = k_ref[0, pl.ds(kstart, width), :]
        zero = jnp.zeros_like(q)
        qs = jnp.concatenate([jnp.where(lo, q, zero), jnp.where(lo, zero, q)], axis=0)
        s_bufs[par][...] = _trans_b_dot(kw, qs) + bias_ref[variant]

    def numerators(par):
        sc = s_bufs[par][...]
        m = jnp.max(sc, axis=0, keepdims=True)
        e_bufs[par][...] = jnp.exp2(sc - m).astype(_BF16)
        m_bufs[par][...] = jnp.broadcast_to(m, (8, cols))

    def outputs(blk, par):
        r, i, kstart, _ = locate(blk)
        vt = v_ref[0, pl.ds(kstart, width), :].T
        lhs = jnp.concatenate([vt, jnp.ones((ONES_ROWS, width), _BF16)], axis=0)
        o = jnp.dot(lhs, e_bufs[par][...], preferred_element_type=_F32)
        den = o[LANES:LANES + 1]
        lse_row = m_bufs[par][0:1] * (1.0 / LOG2E) + jnp.log(den)
        res = jnp.concatenate(
            [o[h * HEAD_DIM:(h + 1) * HEAD_DIM, h * QBLK:(h + 1) * QBLK]
             / den[:, h * QBLK:(h + 1) * QBLK] for h in range(2)], axis=0)
        lse2 = jnp.concatenate(
            [jnp.broadcast_to(lse_row[:, h * QBLK:(h + 1) * QBLK], (HEAD_DIM, QBLK))
             for h in range(2)], axis=0)
        out, lse = res.T, lse2.T
        if dil == 1:
            qstart = pl.multiple_of(blk * QBLK, QBLK)
            o_ref[0, pl.ds(qstart, QBLK), :] = out.astype(_BF16)
            lse_ref[0, pl.ds(qstart, QBLK), :] = lse
        else:
            tstart = i * (QBLK * dil) + r
            scratch[6][pl.ds(tstart, QBLK, stride=dil), :] = out
            lse_ref[0, pl.ds(tstart, QBLK, stride=dil), :] = lse

    scores(0, 0)
    scores(1, 1)
    numerators(0)

    def body(p, carry):
        it = 2 * p
        scores(it, 0)
        numerators(1)
        outputs(it - 2, 0)
        scores(it + 1, 1)
        numerators(0)
        outputs(it - 1, 1)
        return carry

    lax.fori_loop(1, ntot // 2, body, 0, unroll=BAND_UNROLL)
    numerators(1)
    outputs(ntot - 2, 0)
    outputs(ntot - 1, 1)
    if dil != 1:
        o_ref[0] = scratch[6][...].astype(_BF16)


def _band_attn(q, k, v, dil, slopes):
    b, s, _ = q.shape
    width = QBLK + 2 * DIL_RADIUS
    spec = lambda: pl.BlockSpec((1, s, LANES), lambda bi: (bi, 0, 0))
    cols = 2 * QBLK
    scratch = ([pltpu.VMEM((3, width, cols), _F32)]
               + [pltpu.VMEM((width, cols), _F32)] * 2
               + [pltpu.VMEM((width, cols), _BF16)] * 2
               + [pltpu.VMEM((8, cols), _F32)] * 2)
    if dil != 1:
        scratch.append(pltpu.VMEM((s, LANES), _F32))
    return pl.pallas_call(
        functools.partial(_band_attn_kernel, dil=dil, slopes=slopes),
        out_shape=[jax.ShapeDtypeStruct((b, s, LANES), _BF16),
                   jax.ShapeDtypeStruct((b, s, LANES), _F32)],
        grid=(b,),
        in_specs=[spec(), spec(), spec()],
        out_specs=[spec(), spec()],
        scratch_shapes=scratch,
        compiler_params=_cparams(("arbitrary",)),
        name=f"band_attn_d{dil}",
    )(q, k, v)


def _win_attn_kernel(sink_ref, q_ref, k_ref, vt_ref, o_ref, bias_ref,
                     s0_ref, s1_ref, e0_ref, e1_ref, t0_ref, t1_ref, *, slopes, layer):
    s_len = k_ref.shape[1]
    tc = q_ref.shape[2]
    nblk = s_len // QBLK
    nloc = tc // QBLK
    width = QBLK + 2 * WIN_RADIUS
    cols = WIN_GROUP * QBLK
    s_bufs, e_bufs, t_bufs = (s0_ref, s1_ref), (e0_ref, e1_ref), (t0_ref, t1_ref)

    @pl.when((pl.program_id(0) == 0) & (pl.program_id(1) == 0))
    def _():
        for var, off in enumerate((0, WIN_RADIUS, width - QBLK)):
            for j in range(2):
                heads = tuple(j * WIN_GROUP + t for t in range(WIN_GROUP))
                bias_ref[var, j] = _band_bias_t((width, cols), heads, slopes, 1.0,
                                                WIN_RADIUS, off)

    qlane = lax.broadcasted_iota(jnp.int32, (1, cols), 1)
    klo = lax.broadcasted_iota(jnp.int32, (width, LANES), 1) < HALF
    base = pl.program_id(1) * nloc

    def sink_row(j):
        row = jnp.full((1, cols), sink_ref[layer, j * WIN_GROUP] * LOG2E, _F32)
        for t in range(1, WIN_GROUP):
            row = jnp.where(qlane >= t * QBLK, sink_ref[layer, j * WIN_GROUP + t] * LOG2E, row)
        return row

    def kstart_of(bl):
        ws, variant = _window(base + bl, nblk, WIN_RADIUS, width)
        return pl.multiple_of(ws, QBLK), variant

    def scores(bl, par):
        kstart, variant = kstart_of(bl)
        qstart = pl.multiple_of(bl * QBLK, QBLK)
        q = q_ref[0, :, pl.ds(qstart, QBLK), :].reshape(cols, LANES)
        kw = k_ref[0, pl.ds(kstart, width), :]
        zero = jnp.zeros_like(kw)
        for j in range(2):
            kh = jnp.where(klo, kw, zero) if j == 0 else jnp.where(klo, zero, kw)
            s_bufs[par][j] = _trans_b_dot(kh, q) + bias_ref[variant, j]

    def numerators(par):
        for j in range(2):
            sc = s_bufs[par][j]
            sink = sink_row(j)
            m = jnp.maximum(jnp.max(sc, axis=0, keepdims=True), sink)
            e_bufs[par][j] = jnp.exp2(sc - m).astype(_BF16)
            t_bufs[par][j] = jnp.broadcast_to(jnp.exp2(sink - m), (8, cols))

    def outputs(bl, par):
        kstart, _ = kstart_of(bl)
        qstart = pl.multiple_of(bl * QBLK, QBLK)
        vt = vt_ref[0, :, pl.ds(kstart, width)]
        ones = jnp.ones((ONES_ROWS, width), _BF16)
        res = []
        for j in range(2):
            lhs = jnp.concatenate([vt[j * HEAD_DIM:(j + 1) * HEAD_DIM], ones], axis=0)
            o = jnp.dot(lhs, e_bufs[par][j], preferred_element_type=_F32)
            den = o[HEAD_DIM:HEAD_DIM + 1] + t_bufs[par][j][0:1]
            res.append(o[:HEAD_DIM] / den)
        for t in range(WIN_GROUP):
            both = jnp.concatenate([r[:, t * QBLK:(t + 1) * QBLK] for r in res], axis=0)
            o_ref[0, pl.ds(qstart, QBLK), t * LANES:(t + 1) * LANES] = both.T.astype(_BF16)

    scores(0, 0)
    scores(1, 1)
    numerators(0)

    def body(p, carry):
        it = 2 * p
        scores(it, 0)
        numerators(1)
        outputs(it - 2, 0)
        scores(it + 1, 1)
        numerators(0)
        outputs(it - 1, 1)
        return carry

    lax.fori_loop(1, nloc // 2, body, 0, unroll=3)
    numerators(1)
    outputs(nloc - 2, 0)
    outputs(nloc - 1, 1)


def _win_attn(sink, layer, q, k, vt, slopes, tc):
    b, _, s, _ = q.shape
    width = QBLK + 2 * WIN_RADIUS
    cols = WIN_GROUP * QBLK
    assert (tc // QBLK) % 2 == 0 and tc // QBLK >= 4
    return pl.pallas_call(
        functools.partial(_win_attn_kernel, slopes=slopes, layer=layer),
        out_shape=jax.ShapeDtypeStruct((b, s, WIN_GROUP * LANES), _BF16),
        grid=(b, s // tc),
        in_specs=[
            pl.BlockSpec(memory_space=pltpu.SMEM),
            pl.BlockSpec((1, WIN_GROUP, tc, LANES), lambda bi, i: (bi, 0, i, 0)),
            pl.BlockSpec((1, s, LANES), lambda bi, i: (bi, 0, 0)),
            pl.BlockSpec((1, LANES, s), lambda bi, i: (bi, 0, 0)),
        ],
        out_specs=pl.BlockSpec((1, tc, WIN_GROUP * LANES), lambda bi, i: (bi, i, 0)),
        scratch_shapes=[pltpu.VMEM((3, 2, width, cols), _F32)]
                       + [pltpu.VMEM((2, width, cols), _F32)] * 2
                       + [pltpu.VMEM((2, width, cols), _BF16)] * 2
                       + [pltpu.VMEM((2, 8, cols), _F32)] * 2,
        compiler_params=_cparams(("arbitrary", "arbitrary")),
        name="win_attn",
    )(sink, q, k, vt)


def _token_kernel(x_ref, mod_ref, ya_ref,
                  o0_ref, o1_ref, o2_ref, l0_ref, l1_ref, l2_ref, yc_ref, wo_ref,
                  g2_ref, wg_ref, wu_ref, wd_ref, fg_ref,
                  out_ref, mix_ref, *, final, ff_chunks):
    cw = 2 * LANES
    mod = mod_ref[0, 0]
    mix_ref[:, :cw] = ya_ref[0]

    lses = [l0_ref[0], l1_ref[0], l2_ref[0]]
    mx = jnp.maximum(jnp.maximum(lses[0], lses[1]), lses[2])
    es = [jnp.exp(ls - mx) for ls in lses]
    den = es[0] + es[1] + es[2]
    for g, o_ref in enumerate((o0_ref, o1_ref, o2_ref)):
        yb = (es[g] / den) * o_ref[0]
        mix_ref[:, cw + g * LANES:cw + (g + 1) * LANES] = yb.astype(_BF16)
    mix_ref[:, cw + 3 * LANES:] = yc_ref[0]

    x1 = x_ref[0] + mod[2:3] * jnp.dot(mix_ref[...], wo_ref[...], preferred_element_type=_F32)

    h = _modulated_norm(x1, g2_ref[...], mod[3:4], mod[4:5]).astype(_BF16)
    ffn = None
    for c0, c1 in ff_chunks:
        gate = jnp.dot(h, wg_ref[:, c0:c1], preferred_element_type=_F32)
        up = jnp.dot(h, wu_ref[:, c0:c1], preferred_element_type=_F32)
        act = ((gate * (1.0 / (1.0 + jnp.exp(-gate)))) * up).astype(_BF16)
        part = jnp.dot(act, wd_ref[c0:c1, :], preferred_element_type=_F32)
        ffn = part if ffn is None else ffn + part
    y = x1 + mod[5:6] * ffn
    if final:
        ms = jnp.mean(y * y, axis=-1, keepdims=True)
        y = (y * lax.rsqrt(ms + EPS)) * fg_ref[...]
    out_ref[0] = y


def _ff_chunks(dff):
    mxu_k = 2 * LANES
    cut = (dff // 2 + mxu_k - 1) // mxu_k * mxu_k
    return ((0, cut), (cut, dff)) if 0 < cut < dff else ((0, dff),)


def _token_mixers_out(x, mod, l, ya, outs, lses, yc, w_out_p,
                      g2, wg, wu, wd, final_g, final, tm):
    b, s, d = x.shape
    dff = wg.shape[-1]
    cw = 2 * LANES
    tok = lambda width: pl.BlockSpec((1, tm, width), lambda bi, i: (bi, i, 0))
    const = lambda shape: pl.BlockSpec((None,) + shape, lambda bi, i: (l,) + (0,) * len(shape),
                                       pipeline_mode=pl.Buffered(1))
    return pl.pallas_call(
        functools.partial(_token_kernel, final=final, ff_chunks=_ff_chunks(dff)),
        out_shape=jax.ShapeDtypeStruct((b, s, d), _F32),
        grid=(b, s // tm),
        in_specs=[
            tok(d),
            pl.BlockSpec((1, 1, N_MOD, d), lambda bi, i: (l, bi, 0, 0)),
            tok(cw),
            tok(LANES), tok(LANES), tok(LANES),
            tok(LANES), tok(LANES), tok(LANES),
            tok(WIN_GROUP * LANES),
            const((d, d)),
            const((1, d)),
            const((d, dff)), const((d, dff)), const((dff, d)),
            pl.BlockSpec((1, d), lambda bi, i: (0, 0)),
        ],
        out_specs=tok(d),
        scratch_shapes=[pltpu.VMEM((tm, d), _BF16)],
        compiler_params=_cparams(("arbitrary", "arbitrary")),
        name="token_mix_ffn",
    )(x, mod, ya, *outs, *lses, yc, w_out_p, g2, wg, wu, wd, final_g)


def _in_proj_columns(d_model):
    pool_w = d_model // 4
    dil_w = 2 * len(DIL_PAIRS) * HEAD_DIM
    o_qb = pool_w
    o_kb = o_qb + dil_w
    o_vb = o_kb + dil_w
    o_qc = o_vb + dil_w
    o_kc = o_qc + 2 * WIN_GROUP * HEAD_DIM
    o_vc = o_kc + 2 * HEAD_DIM
    cols = list(range(pool_w))
    for g in range(len(DIL_PAIRS)):
        for base in (o_qb, o_kb, o_vb):
            cols += range(base + g * LANES, base + (g + 1) * LANES)
    win_heads = []
    for t in range(WIN_GROUP):
        for j in range(2):
            win_heads.append(j * WIN_GROUP + t)
            cols += range(o_qc + (j * WIN_GROUP + t) * HEAD_DIM,
                          o_qc + (j * WIN_GROUP + t + 1) * HEAD_DIM)
    cols += range(o_kc, o_kc + LANES)
    cols += range(o_vc, o_vc + LANES)
    qscale = np.ones((len(cols),), np.float32)
    for g in range(len(DIL_PAIRS)):
        qscale[pool_w + 3 * g * LANES:pool_w + (3 * g + 1) * LANES] = HEAD_DIM ** -0.5 * LOG2E
    q0 = pool_w + 3 * len(DIL_PAIRS) * LANES
    qscale[q0:q0 + WIN_GROUP * LANES] = HEAD_DIM ** -0.5 * LOG2E
    return np.asarray(cols), qscale, win_heads


def _runs(idx, key=None):
    idx = np.asarray(idx)
    key = np.zeros(len(idx)) if key is None else np.asarray(key)
    cuts = ([0] + [i for i in range(1, len(idx))
                   if idx[i] != idx[i - 1] + 1 or key[i] != key[i - 1]] + [len(idx)])
    return [(int(idx[a]), b - a, a) for a, b in zip(cuts[:-1], cuts[1:])]


def _take_runs(w, idx, axis):
    parts = [lax.slice_in_dim(w, src, src + n, axis=axis) for src, n, _ in _runs(idx)]
    return jnp.concatenate(parts, axis=axis)


def _permute_cols_kernel(w_ref, o_ref, *, runs):
    for src, n, dst, scale in runs:
        o_ref[0, :, dst:dst + n] = (w_ref[0, :, src:src + n] * scale).astype(o_ref.dtype)


def _permute_cols_bf16(w, cols, scale, tr):
    depth, rows, n = w.shape
    runs = [(src, length, dst, float(scale[dst])) for src, length, dst in _runs(cols, scale)]
    spec = pl.BlockSpec((1, tr, n), lambda l, i: (l, i, 0))
    return pl.pallas_call(
        functools.partial(_permute_cols_kernel, runs=tuple(runs)),
        out_shape=jax.ShapeDtypeStruct(w.shape, _BF16),
        grid=(depth, rows // tr),
        in_specs=[spec],
        out_specs=spec,
        compiler_params=_cparams(("arbitrary", "arbitrary")),
        name="permute_cols",
    )(w)


def kernel(x, c, norm1_g, norm2_g, w_ada, b_ada, w_in, w_pool, pool_scale, sink_logit,
           w_out, w_gate, w_up, w_down, final_g):
    b, s, d = x.shape
    depth = w_in.shape[0]
    slopes = _alibi_slopes(2 * WIN_GROUP + 2 * len(DIL_PAIRS))
    slopes_win = tuple(float(v) for v in slopes[:2 * WIN_GROUP])
    slopes_dil = slopes[2 * WIN_GROUP:]
    cols, qscale, win_heads = _in_proj_columns(d)
    pool_w = d // 4
    dil_w = 2 * len(DIL_PAIRS) * HEAD_DIM
    mix_rows = np.concatenate([
        np.arange(pool_w + dil_w),
        np.concatenate([pool_w + dil_w + h * HEAD_DIM + np.arange(HEAD_DIM) for h in win_heads]),
    ])
    n_grp = len(POOL_WINDOWS)
    pg = pool_w // n_grp

    mod = _modulation(c, w_ada, b_ada).reshape(depth, b, N_MOD, d)
    w_in_p = _permute_cols_bf16(w_in, cols, qscale, 256)
    w_out_p = _take_runs(w_out, mix_rows, 1).astype(_BF16)
    zpad = lambda n: jnp.zeros((depth, pg, n * pg), _F32)
    wpool_bd = jnp.concatenate(
        [jnp.concatenate([zpad(g), w_pool[:, g], zpad(n_grp - 1 - g)], axis=2)
         for g in range(n_grp)], axis=1).astype(_BF16)
    wg, wu, wd = w_gate.astype(_BF16), w_up.astype(_BF16), w_down.astype(_BF16)
    g1, g2 = norm1_g.reshape(depth, 1, d), norm2_g.reshape(depth, 1, d)
    ps = pool_scale.reshape(depth, 1, pool_w)
    tm = 512
    for l in range(depth):
        (ya, q0, k0, v0, q1, k1, v1, q2, k2, v2, qc, kc, vc) = _in_proj(
            x, mod, l, g1, w_in_p, wpool_bd, ps, 2 * tm)
        outs, lses = [], []
        for g, ((_, dil), (q, k, v)) in enumerate(zip(
                DIL_PAIRS, ((q0, k0, v0), (q1, k1, v1), (q2, k2, v2)))):
            sl = tuple(float(v_) for v_ in slopes_dil[2 * g:2 * g + 2])
            o, ls = _band_attn(q.reshape(b, s, LANES), k.reshape(b, s, LANES),
                               v.reshape(b, s, LANES), dil, sl)
            outs.append(o)
            lses.append(ls)
        yc = _win_attn(sink_logit, l, qc, kc, vc, slopes_win, s)
        x = _token_mixers_out(x, mod, l, ya, outs, lses, yc, w_out_p, g2, wg, wu, wd,
                              final_g.reshape(1, d), l == depth - 1, tm)
    return x
```

```python
import functools
import math

import jax
import jax.numpy as jnp
import numpy as np
from jax import lax
from jax.experimental import pallas as pl
from jax.experimental.pallas import tpu as pltpu

LANES = 128
HEAD_DIM = 64
HALF = LANES // 2
POOL_WINDOWS = (2, 4, 8, 16)
POOL_HALO = 8
DIL_PAIRS = ((128, 1), (512, 4), (2048, 16))
DIL_RADIUS = 64
WIN_RADIUS = 128
WIN_GROUP = 3
N_MOD = 6
EPS = 1e-6
NEG = -1e30
QBLK = 128
ONES_ROWS = 16
LOG2E = math.log2(math.e)
BAND_UNROLL = 5
VMEM_LIMIT = 56 * 1024 * 1024

_F32 = jnp.float32
_BF16 = jnp.bfloat16


def _alibi_slopes(n):
    i = np.arange(1, n + 1, dtype=np.float32)
    return np.exp2(np.float32(-8.0) * i / np.float32(n)).astype(np.float32)


def _cparams(sem):
    return pltpu.CompilerParams(dimension_semantics=sem, vmem_limit_bytes=VMEM_LIMIT)


def _mod_kernel(c_ref, w_ref, b_ref, o_ref):
    c = c_ref[...]
    act = (c * (1.0 / (1.0 + jnp.exp(-c)))).astype(_BF16)
    w = w_ref[0].astype(_BF16)
    o_ref[0] = jnp.dot(act, w, preferred_element_type=_F32) + b_ref[0]


def _modulation(c, w_ada, b_ada):
    depth, d, nd = w_ada.shape
    b = c.shape[0]
    tn = d
    return pl.pallas_call(
        _mod_kernel,
        out_shape=jax.ShapeDtypeStruct((depth, b, nd), _F32),
        grid=(depth, nd // tn),
        in_specs=[
            pl.BlockSpec((b, d), lambda l, j: (0, 0)),
            pl.BlockSpec((1, d, tn), lambda l, j: (l, 0, j)),
            pl.BlockSpec((1, 1, tn), lambda l, j: (l, 0, j)),
        ],
        out_specs=pl.BlockSpec((1, b, tn), lambda l, j: (l, 0, j)),
        compiler_params=_cparams(("arbitrary", "arbitrary")),
        name="modulation",
    )(c, w_ada, b_ada.reshape(depth, 1, nd))


def _modulated_norm(x, g, shift, scale):
    ms = jnp.mean(x * x, axis=-1, keepdims=True)
    return (x * lax.rsqrt(ms + EPS)) * (g * (1.0 + scale)) + shift


def _pooled_tokens(ubuf_ref, t0, s_len):
    tm = ubuf_ref.shape[0] - 2 * POOL_HALO
    cols = []
    for c in range(2):
        r_lo, r_hi = POOL_WINDOWS[2 * c] // 2, POOL_WINDOWS[2 * c + 1] // 2
        lanes = slice(c * LANES, (c + 1) * LANES)
        lane = lax.broadcasted_iota(jnp.int32, (1, LANES), 1)
        radius = jnp.where(lane < HALF, r_lo, r_hi)
        u = ubuf_ref[POOL_HALO:POOL_HALO + tm, lanes]
        wsum = u
        for k in range(1, r_hi + 1):
            pair = (ubuf_ref[POOL_HALO - k:POOL_HALO - k + tm, lanes]
                    + ubuf_ref[POOL_HALO + k:POOL_HALO + k + tm, lanes])
            wsum = wsum + (pair if k <= r_lo else jnp.where(lane < HALF, 0.0, pair))
        t = t0 + lax.broadcasted_iota(jnp.int32, (tm, LANES), 0)
        cnt = jnp.minimum(t + radius + 1, s_len) - jnp.maximum(t - radius, 0)
        cols.append((wsum / cnt.astype(_F32) - u).astype(_BF16))
    return jnp.concatenate(cols, axis=1)


def _in_proj_kernel(x_ref, xp_ref, xn_ref, mod_ref, g_ref, w_ref, wp_ref, ps_ref,
                    ya_ref, q0_ref, k0_ref, v0_ref, q1_ref, k1_ref, v1_ref,
                    q2_ref, k2_ref, v2_ref, qc_ref, kc_ref, vct_ref, zs_ref, ubuf_ref,
                    *, s_len):
    tm = x_ref.shape[1]
    i = pl.program_id(1)
    cw = 2 * LANES
    mod = mod_ref[0, 0]
    norm = lambda rows: _modulated_norm(rows, g_ref[...], mod[0:1], mod[1:2]).astype(_BF16)
    h = norm(x_ref[0])

    def proj(lhs, t0, t1):
        return jnp.dot(lhs, w_ref[:, t0 * LANES:t1 * LANES], preferred_element_type=_F32)

    halo = proj(norm(jnp.concatenate([xp_ref[0], xn_ref[0]], axis=0)), 0, 2)
    ubuf_ref[0:POOL_HALO] = jnp.where(i > 0, halo[:POOL_HALO], 0.0)
    ubuf_ref[POOL_HALO:POOL_HALO + tm] = proj(h, 0, 2)
    ubuf_ref[POOL_HALO + tm:] = jnp.where(i < pl.num_programs(1) - 1, halo[POOL_HALO:], 0.0)
    pooled = _pooled_tokens(ubuf_ref, i * tm, s_len)

    z_lo, z_hi = proj(h, 2, 8), proj(h, 8, 16)

    def tile(i):
        z, j = (z_lo, i - 2) if i < 8 else (z_hi, i - 8)
        return z[:, j * LANES:(j + 1) * LANES]

    for j, ref in enumerate((q0_ref, k0_ref, v0_ref)):
        ref[0] = tile(2 + j).astype(_BF16)
    for refs, (_, dil), base in (((q1_ref, k1_ref, v1_ref), DIL_PAIRS[1], 5),
                                 ((q2_ref, k2_ref, v2_ref), DIL_PAIRS[2], 8)):
        rows = tm // dil
        for j, ref in enumerate(refs):
            zs_ref[base - 5 + j] = tile(base + j)
            for r in range(dil):
                ref[0, r] = zs_ref[base - 5 + j, pl.ds(r, rows, stride=dil), :].astype(_BF16)
    for t in range(WIN_GROUP):
        qc_ref[0, t] = tile(11 + t).astype(_BF16)
    kc_ref[0] = tile(14).astype(_BF16)
    vct_ref[0] = tile(15).T.astype(_BF16)
    ya = jnp.dot(pooled, wp_ref[...], preferred_element_type=_F32) * ps_ref[...]
    ya_ref[0] = ya.astype(_BF16)


def _in_proj(x, mod, l, g, w_in_p, wpool_bd, pool_scale, tm):
    b, s, d = x.shape
    n_in = w_in_p.shape[-1]
    cw = 2 * LANES
    hb = tm // POOL_HALO
    nhb = s // POOL_HALO
    d1, d2 = DIL_PAIRS[1][1], DIL_PAIRS[2][1]
    nat = lambda: pl.BlockSpec((1, tm, LANES), lambda bi, i: (bi, i, 0))
    res = lambda dil: pl.BlockSpec((1, dil, tm // dil, LANES), lambda bi, i: (bi, 0, i, 0))
    full = lambda shape: pl.BlockSpec((None,) + shape, lambda bi, i: (l,) + (0,) * len(shape))
    sds = jax.ShapeDtypeStruct
    out_shape = (
        [sds((b, s, cw), _BF16)]
        + [sds((b, s, LANES), _BF16)] * 3
        + [sds((b, d1, s // d1, LANES), _BF16)] * 3
        + [sds((b, d2, s // d2, LANES), _BF16)] * 3
        + [sds((b, WIN_GROUP, s, LANES), _BF16)]
        + [sds((b, s, LANES), _BF16), sds((b, LANES, s), _BF16)]
    )
    out_specs = (
        [pl.BlockSpec((1, tm, 2 * LANES), lambda bi, i: (bi, i, 0))]
        + [nat() for _ in range(3)]
        + [res(d1) for _ in range(3)]
        + [res(d2) for _ in range(3)]
        + [pl.BlockSpec((1, WIN_GROUP, tm, LANES), lambda bi, i: (bi, 0, i, 0))]
        + [nat(), pl.BlockSpec((1, LANES, tm), lambda bi, i: (bi, 0, i))]
    )
    return pl.pallas_call(
        functools.partial(_in_proj_kernel, s_len=s),
        out_shape=out_shape,
        grid=(b, s // tm),
        in_specs=[
            pl.BlockSpec((1, tm, d), lambda bi, i: (bi, i, 0)),
            pl.BlockSpec((1, POOL_HALO, d), lambda bi, i: (bi, jnp.maximum(i * hb - 1, 0), 0)),
            pl.BlockSpec((1, POOL_HALO, d),
                         lambda bi, i: (bi, jnp.minimum((i + 1) * hb, nhb - 1), 0)),
            pl.BlockSpec((1, 1, N_MOD, d), lambda bi, i: (l, bi, 0, 0)),
            full((1, d)),
            full((d, n_in)),
            full((cw, cw)),
            full((1, cw)),
        ],
        out_specs=out_specs,
        scratch_shapes=[pltpu.VMEM((6, tm, LANES), _F32),
                        pltpu.VMEM((tm + 2 * POOL_HALO, cw), _F32)],
        compiler_params=_cparams(("arbitrary", "arbitrary")),
        name="in_proj",
    )(x, x, x, mod, g, w_in_p, wpool_bd, pool_scale)


def _band_bias_t(shape, col_heads, slopes, dist_scale, radius, offset):
    row = lax.broadcasted_iota(jnp.int32, shape, 0)
    col = lax.broadcasted_iota(jnp.int32, shape, 1)
    dist = jnp.abs(row - offset - (col % QBLK))
    slope = jnp.full(shape, slopes[col_heads[0]] * dist_scale, _F32)
    for i, hd in enumerate(col_heads[1:], start=1):
        slope = jnp.where(col >= i * QBLK, np.float32(slopes[hd] * dist_scale), slope)
    return jnp.where(dist <= radius, (-slope * dist.astype(_F32)) * LOG2E, NEG)


def _window(i, nblk, radius, width):
    seq = nblk * QBLK
    ws = jnp.clip(i * QBLK - radius, 0, seq - width)
    variant = jnp.where(i == 0, 0, jnp.where(i == nblk - 1, 2, 1))
    return ws, variant


def _trans_b_dot(a, b):
    return lax.dot_general(a, b, (((1,), (1,)), ((), ())), preferred_element_type=_F32)


def _band_attn_kernel(q_ref, k_ref, v_ref, o_ref, lse_ref, bias_ref, *scratch, dil, slopes):
    s_len = q_ref.shape[1]
    n = s_len // dil
    nblk = n // QBLK
    width = QBLK + 2 * DIL_RADIUS
    cols = 2 * QBLK

    @pl.when(pl.program_id(0) == 0)
    def _():
        for var, off in enumerate((0, DIL_RADIUS, width - QBLK)):
            bias_ref[var] = _band_bias_t((width, cols), (0, 1), slopes, float(dil),
                                         DIL_RADIUS, off)

    lo = lax.broadcasted_iota(jnp.int32, (QBLK, LANES), 1) < HALF
    s_bufs, e_bufs, m_bufs = scratch[0:2], scratch[2:4], scratch[4:6]
    ntot = s_len // QBLK

    def locate(blk):
        r = blk // nblk
        i = blk % nblk
        ws, variant = _window(i, nblk, DIL_RADIUS, width)
        return r, i, pl.multiple_of(r * n + ws, DIL_RADIUS), variant

    def scores(blk, par):
        _, _, kstart, variant = locate(blk)
        qstart = pl.multiple_of(blk * QBLK, QBLK)
        q = q_ref[0, pl.ds(qstart, QBLK), :]
        kw = k_ref[0, pl.ds(kstart, width), :]
        zero = jnp.zeros_like(q)
        qs = jnp.concatenate([jnp.where(lo, q, zero), jnp.where(lo, zero, q)], axis=0)
        s_bufs[par][...] = _trans_b_dot(kw, qs) + bias_ref[variant]

    def numerators(par):
        sc = s_bufs[par][...]
        m = jnp.max(sc, axis=0, keepdims=True)
        e_bufs[par][...] = jnp.exp2(sc - m).astype(_BF16)
        m_bufs[par][...] = jnp.broadcast_to(m, (8, cols))

    def outputs(blk, par):
        r, i, kstart, _ = locate(blk)
        vt = v_ref[0, pl.ds(kstart, width), :].T
        lhs = jnp.concatenate([vt, jnp.ones((ONES_ROWS, width), _BF16)], axis=0)
        o = jnp.dot(lhs, e_bufs[par][...], preferred_element_type=_F32)
        den = o[LANES:LANES + 1]
        lse_row = m_bufs[par][0:1] * (1.0 / LOG2E) + jnp.log(den)
        res = jnp.concatenate(
            [o[h * HEAD_DIM:(h + 1) * HEAD_DIM, h * QBLK:(h + 1) * QBLK]
             / den[:, h * QBLK:(h + 1) * QBLK] for h in range(2)], axis=0)
        lse2 = jnp.concatenate(
            [jnp.broadcast_to(lse_row[:, h * QBLK:(h + 1) * QBLK], (HEAD_DIM, QBLK))
             for h in range(2)], axis=0)
        out, lse = res.T, lse2.T
        if dil == 1:
            qstart = pl.multiple_of(blk * QBLK, QBLK)
            o_ref[0, pl.ds(qstart, QBLK), :] = out.astype(_BF16)
            lse_ref[0, pl.ds(qstart, QBLK), :] = lse
        else:
            tstart = i * (QBLK * dil) + r
            scratch[6][pl.ds(tstart, QBLK, stride=dil), :] = out
            lse_ref[0, pl.ds(tstart, QBLK, stride=dil), :] = lse

    scores(0, 0)
    scores(1, 1)
    numerators(0)

    def body(p, carry):
        it = 2 * p
        scores(it, 0)
        numerators(1)
        outputs(it - 2, 0)
        scores(it + 1, 1)
        numerators(0)
        outputs(it - 1, 1)
        return carry

    lax.fori_loop(1, ntot // 2, body, 0, unroll=BAND_UNROLL)
    numerators(1)
    outputs(ntot - 2, 0)
    outputs(ntot - 1, 1)
    if dil != 1:
        o_ref[0] = scratch[6][...].astype(_BF16)


def _band_attn(q, k, v, dil, slopes):
    b, s, _ = q.shape
    width = QBLK + 2 * DIL_RADIUS
    spec = lambda: pl.BlockSpec((1, s, LANES), lambda bi: (bi, 0, 0))
    cols = 2 * QBLK
    scratch = ([pltpu.VMEM((3, width, cols), _F32)]
               + [pltpu.VMEM((width, cols), _F32)] * 2
               + [pltpu.VMEM((width, cols), _BF16)] * 2
               + [pltpu.VMEM((8, cols), _F32)] * 2)
    if dil != 1:
        scratch.append(pltpu.VMEM((s, LANES), _F32))
    return pl.pallas_call(
        functools.partial(_band_attn_kernel, dil=dil, slopes=slopes),
        out_shape=[jax.ShapeDtypeStruct((b, s, LANES), _BF16),
                   jax.ShapeDtypeStruct((b, s, LANES), _F32)],
        grid=(b,),
        in_specs=[spec(), spec(), spec()],
        out_specs=[spec(), spec()],
        scratch_shapes=scratch,
        compiler_params=_cparams(("arbitrary",)),
        name=f"band_attn_d{dil}",
    )(q, k, v)


def _win_attn_kernel(sink_ref, q_ref, k_ref, vt_ref, o_ref, bias_ref,
                     s0_ref, s1_ref, e0_ref, e1_ref, t0_ref, t1_ref, *, slopes, layer):
    s_len = k_ref.shape[1]
    tc = q_ref.shape[2]
    nblk = s_len // QBLK
    nloc = tc // QBLK
    width = QBLK + 2 * WIN_RADIUS
    cols = WIN_GROUP * QBLK
    s_bufs, e_bufs, t_bufs = (s0_ref, s1_ref), (e0_ref, e1_ref), (t0_ref, t1_ref)

    @pl.when((pl.program_id(0) == 0) & (pl.program_id(1) == 0))
    def _():
        for var, off in enumerate((0, WIN_RADIUS, width - QBLK)):
            for j in range(2):
                heads = tuple(j * WIN_GROUP + t for t in range(WIN_GROUP))
                bias_ref[var, j] = _band_bias_t((width, cols), heads, slopes, 1.0,
                                                WIN_RADIUS, off)

    qlane = lax.broadcasted_iota(jnp.int32, (1, cols), 1)
    klo = lax.broadcasted_iota(jnp.int32, (width, LANES), 1) < HALF
    base = pl.program_id(1) * nloc

    def sink_row(j):
        row = jnp.full((1, cols), sink_ref[layer, j * WIN_GROUP] * LOG2E, _F32)
        for t in range(1, WIN_GROUP):
            row = jnp.where(qlane >= t * QBLK, sink_ref[layer, j * WIN_GROUP + t] * LOG2E, row)
        return row

    def kstart_of(bl):
        ws, variant = _window(base + bl, nblk, WIN_RADIUS, width)
        return pl.multiple_of(ws, QBLK), variant

    def scores(bl, par):
        kstart, variant = kstart_of(bl)
        qstart = pl.multiple_of(bl * QBLK, QBLK)
        q = q_ref[0, :, pl.ds(qstart, QBLK), :].reshape(cols, LANES)
        kw = k_ref[0, pl.ds(kstart, width), :]
        zero = jnp.zeros_like(kw)
        for j in range(2):
            kh = jnp.where(klo, kw, zero) if j == 0 else jnp.where(klo, zero, kw)
            s_bufs[par][j] = _trans_b_dot(kh, q) + bias_ref[variant, j]

    def numerators(par):
        for j in range(2):
            sc = s_bufs[par][j]
            sink = sink_row(j)
            m = jnp.maximum(jnp.max(sc, axis=0, keepdims=True), sink)
            e_bufs[par][j] = jnp.exp2(sc - m).astype(_BF16)
            t_bufs[par][j] = jnp.broadcast_to(jnp.exp2(sink - m), (8, cols))

    def outputs(bl, par):
        kstart, _ = kstart_of(bl)
        qstart = pl.multiple_of(bl * QBLK, QBLK)
        vt = vt_ref[0, :, pl.ds(kstart, width)]
        ones = jnp.ones((ONES_ROWS, width), _BF16)
        res = []
        for j in range(2):
            lhs = jnp.concatenate([vt[j * HEAD_DIM:(j + 1) * HEAD_DIM], ones], axis=0)
            o = jnp.dot(lhs, e_bufs[par][j], preferred_element_type=_F32)
            den = o[HEAD_DIM:HEAD_DIM + 1] + t_bufs[par][j][0:1]
            res.append(o[:HEAD_DIM] / den)
        for t in range(WIN_GROUP):
            both = jnp.concatenate([r[:, t * QBLK:(t + 1) * QBLK] for r in res], axis=0)
            o_ref[0, pl.ds(qstart, QBLK), t * LANES:(t + 1) * LANES] = both.T.astype(_BF16)

    scores(0, 0)
    scores(1, 1)
    numerators(0)

    def body(p, carry):
        it = 2 * p
        scores(it, 0)
        numerators(1)
        outputs(it - 2, 0)
        scores(it + 1, 1)
        numerators(0)
        outputs(it - 1, 1)
        return carry

    lax.fori_loop(1, nloc // 2, body, 0, unroll=3)
    numerators(1)
    outputs(nloc - 2, 0)
    outputs(nloc - 1, 1)


def _win_attn(sink, layer, q, k, vt, slopes, tc):
    b, _, s, _ = q.shape
    width = QBLK + 2 * WIN_RADIUS
    cols = WIN_GROUP * QBLK
    assert (tc // QBLK) % 2 == 0 and tc // QBLK >= 4
    return pl.pallas_call(
        functools.partial(_win_attn_kernel, slopes=slopes, layer=layer),
        out_shape=jax.ShapeDtypeStruct((b, s, WIN_GROUP * LANES), _BF16),
        grid=(b, s // tc),
        in_specs=[
            pl.BlockSpec(memory_space=pltpu.SMEM),
            pl.BlockSpec((1, WIN_GROUP, tc, LANES), lambda bi, i: (bi, 0, i, 0)),
            pl.BlockSpec((1, s, LANES), lambda bi, i: (bi, 0, 0)),
            pl.BlockSpec((1, LANES, s), lambda bi, i: (bi, 0, 0)),
        ],
        out_specs=pl.BlockSpec((1, tc, WIN_GROUP * LANES), lambda bi, i: (bi, i, 0)),
        scratch_shapes=[pltpu.VMEM((3, 2, width, cols), _F32)]
                       + [pltpu.VMEM((2, width, cols), _F32)] * 2
                       + [pltpu.VMEM((2, width, cols), _BF16)] * 2
                       + [pltpu.VMEM((2, 8, cols), _F32)] * 2,
        compiler_params=_cparams(("arbitrary", "arbitrary")),
        name="win_attn",
    )(sink, q, k, vt)


def _token_kernel(x_ref, mod_ref, ya_ref,
                  o0_ref, o1_ref, o2_ref, l0_ref, l1_ref, l2_ref, yc_ref, wo_ref,
                  g2_ref, wg_ref, wu_ref, wd_ref, fg_ref,
                  out_ref, mix_ref, *, final, ff_chunks):
    cw = 2 * LANES
    mod = mod_ref[0, 0]
    mix_ref[:, :cw] = ya_ref[0]

    lses = [l0_ref[0], l1_ref[0], l2_ref[0]]
    mx = jnp.maximum(jnp.maximum(lses[0], lses[1]), lses[2])
    es = [jnp.exp(ls - mx) for ls in lses]
    den = es[0] + es[1] + es[2]
    for g, o_ref in enumerate((o0_ref, o1_ref, o2_ref)):
        yb = (es[g] / den) * o_ref[0]
        mix_ref[:, cw + g * LANES:cw + (g + 1) * LANES] = yb.astype(_BF16)
    mix_ref[:, cw + 3 * LANES:] = yc_ref[0]

    out_ref[0] = x_ref[0] + mod[2:3] * jnp.dot(mix_ref[...], wo_ref[...],
                                               preferred_element_type=_F32)

    h = _modulated_norm(out_ref[0], g2_ref[...], mod[3:4], mod[4:5]).astype(_BF16)
    for c0, c1 in ff_chunks:
        gate = jnp.dot(h, wg_ref[:, c0:c1], preferred_element_type=_F32)
        up = jnp.dot(h, wu_ref[:, c0:c1], preferred_element_type=_F32)
        act = ((gate * (1.0 / (1.0 + jnp.exp(-gate)))) * up).astype(_BF16)
        out_ref[0] += mod[5:6] * jnp.dot(act, wd_ref[c0:c1, :], preferred_element_type=_F32)
    if final:
        y = out_ref[0]
        ms = jnp.mean(y * y, axis=-1, keepdims=True)
        out_ref[0] = (y * lax.rsqrt(ms + EPS)) * fg_ref[...]


def _ff_chunks(dff, n_chunks=4):
    mxu_k = 2 * LANES
    tiles = -(-dff // mxu_k)
    cuts = sorted({min(dff, -(-tiles * c // n_chunks) * mxu_k) for c in range(n_chunks + 1)})
    return tuple(zip(cuts[:-1], cuts[1:]))


def _token_mixers_out(x, mod, l, ya, outs, lses, yc, w_out_p,
                      g2, wg, wu, wd, final_g, final, tm):
    b, s, d = x.shape
    dff = wg.shape[-1]
    cw = 2 * LANES
    tok = lambda width: pl.BlockSpec((1, tm, width), lambda bi, i: (bi, i, 0))
    const = lambda shape: pl.BlockSpec((None,) + shape, lambda bi, i: (l,) + (0,) * len(shape),
                                       pipeline_mode=pl.Buffered(1))
    return pl.pallas_call(
        functools.partial(_token_kernel, final=final, ff_chunks=_ff_chunks(dff)),
        out_shape=jax.ShapeDtypeStruct((b, s, d), _F32),
        grid=(b, s // tm),
        in_specs=[
            tok(d),
            pl.BlockSpec((1, 1, N_MOD, d), lambda bi, i: (l, bi, 0, 0)),
            tok(cw),
            tok(LANES), tok(LANES), tok(LANES),
            tok(LANES), tok(LANES), tok(LANES),
            tok(WIN_GROUP * LANES),
            const((d, d)),
            const((1, d)),
            const((d, dff)), const((d, dff)), const((dff, d)),
            pl.BlockSpec((1, d), lambda bi, i: (0, 0)),
        ],
        out_specs=tok(d),
        scratch_shapes=[pltpu.VMEM((tm, d), _BF16)],
        compiler_params=_cparams(("arbitrary", "arbitrary")),
        name="token_mix_ffn",
    )(x, mod, ya, *outs, *lses, yc, w_out_p, g2, wg, wu, wd, final_g)


def _in_proj_columns(d_model):
    pool_w = d_model // 4
    dil_w = 2 * len(DIL_PAIRS) * HEAD_DIM
    o_qb = pool_w
    o_kb = o_qb + dil_w
    o_vb = o_kb + dil_w
    o_qc = o_vb + dil_w
    o_kc = o_qc + 2 * WIN_GROUP * HEAD_DIM
    o_vc = o_kc + 2 * HEAD_DIM
    cols = list(range(pool_w))
    for g in range(len(DIL_PAIRS)):
        for base in (o_qb, o_kb, o_vb):
            cols += range(base + g * LANES, base + (g + 1) * LANES)
    win_heads = []
    for t in range(WIN_GROUP):
        for j in range(2):
            win_heads.append(j * WIN_GROUP + t)
            cols += range(o_qc + (j * WIN_GROUP + t) * HEAD_DIM,
                          o_qc + (j * WIN_GROUP + t + 1) * HEAD_DIM)
    cols += range(o_kc, o_kc + LANES)
    cols += range(o_vc, o_vc + LANES)
    qscale = np.ones((len(cols),), np.float32)
    for g in range(len(DIL_PAIRS)):
        qscale[pool_w + 3 * g * LANES:pool_w + (3 * g + 1) * LANES] = HEAD_DIM ** -0.5 * LOG2E
    q0 = pool_w + 3 * len(DIL_PAIRS) * LANES
    qscale[q0:q0 + WIN_GROUP * LANES] = HEAD_DIM ** -0.5 * LOG2E
    return np.asarray(cols), qscale, win_heads


def _runs(idx, key=None):
    idx = np.asarray(idx)
    key = np.zeros(len(idx)) if key is None else np.asarray(key)
    cuts = ([0] + [i for i in range(1, len(idx))
                   if idx[i] != idx[i - 1] + 1 or key[i] != key[i - 1]] + [len(idx)])
    return [(int(idx[a]), b - a, a) for a, b in zip(cuts[:-1], cuts[1:])]


def _take_runs(w, idx, axis):
    parts = [lax.slice_in_dim(w, src, src + n, axis=axis) for src, n, _ in _runs(idx)]
    return jnp.concatenate(parts, axis=axis)


def _permute_cols_kernel(w_ref, o_ref, *, runs):
    for src, n, dst, scale in runs:
        o_ref[0, :, dst:dst + n] = (w_ref[0, :, src:src + n] * scale).astype(o_ref.dtype)


def _permute_cols_bf16(w, cols, scale, tr):
    depth, rows, n = w.shape
    runs = [(src, length, dst, float(scale[dst])) for src, length, dst in _runs(cols, scale)]
    spec = pl.BlockSpec((1, tr, n), lambda l, i: (l, i, 0))
    return pl.pallas_call(
        functools.partial(_permute_cols_kernel, runs=tuple(runs)),
        out_shape=jax.ShapeDtypeStruct(w.shape, _BF16),
        grid=(depth, rows // tr),
        in_specs=[spec],
        out_specs=spec,
        compiler_params=_cparams(("arbitrary", "arbitrary")),
        name="permute_cols",
    )(w)


def kernel(x, c, norm1_g, norm2_g, w_ada, b_ada, w_in, w_pool, pool_scale, sink_logit,
           w_out, w_gate, w_up, w_down, final_g):
    b, s, d = x.shape
    depth = w_in.shape[0]
    slopes = _alibi_slopes(2 * WIN_GROUP + 2 * len(DIL_PAIRS))
    slopes_win = tuple(float(v) for v in slopes[:2 * WIN_GROUP])
    slopes_dil = slopes[2 * WIN_GROUP:]
    cols, qscale, win_heads = _in_proj_columns(d)
    pool_w = d // 4
    dil_w = 2 * len(DIL_PAIRS) * HEAD_DIM
    mix_rows = np.concatenate([
        np.arange(pool_w + dil_w),
        np.concatenate([pool_w + dil_w + h * HEAD_DIM + np.arange(HEAD_DIM) for h in win_heads]),
    ])
    n_grp = len(POOL_WINDOWS)
    pg = pool_w // n_grp

    mod = _modulation(c, w_ada, b_ada).reshape(depth, b, N_MOD, d)
    w_in_p = _permute_cols_bf16(w_in, cols, qscale, 256)
    w_out_p = _take_runs(w_out, mix_rows, 1).astype(_BF16)
    zpad = lambda n: jnp.zeros((depth, pg, n * pg), _F32)
    wpool_bd = jnp.concatenate(
        [jnp.concatenate([zpad(g), w_pool[:, g], zpad(n_grp - 1 - g)], axis=2)
         for g in range(n_grp)], axis=1).astype(_BF16)
    wg, wu, wd = w_gate.astype(_BF16), w_up.astype(_BF16), w_down.astype(_BF16)
    g1, g2 = norm1_g.reshape(depth, 1, d), norm2_g.reshape(depth, 1, d)
    ps = pool_scale.reshape(depth, 1, pool_w)
    tm = 512
    for l in range(depth):
        (ya, q0, k0, v0, q1, k1, v1, q2, k2, v2, qc, kc, vc) = _in_proj(
            x, mod, l, g1, w_in_p, wpool_bd, ps, 2 * tm)
        outs, lses = [], []
        for g, ((_, dil), (q, k, v)) in enumerate(zip(
                DIL_PAIRS, ((q0, k0, v0), (q1, k1, v1), (q2, k2, v2)))):
            sl = tuple(float(v_) for v_ in slopes_dil[2 * g:2 * g + 2])
            o, ls = _band_attn(q.reshape(b, s, LANES), k.reshape(b, s, LANES),
                               v.reshape(b, s, LANES), dil, sl)
            outs.append(o)
            lses.append(ls)
        yc = _win_attn(sink_logit, l, qc, kc, vc, slopes_win, s)
        x = _token_mixers_out(x, mod, l, ya, outs, lses, yc, w_out_p, g2, wg, wu, wd,
                              final_g.reshape(1, d), l == depth - 1, 2 * tm)
    return x
```

```python
import functools
import math

import jax
import jax.numpy as jnp
import numpy as np
from jax import lax
from jax.experimental import pallas as pl
from jax.experimental.pallas import tpu as pltpu

LANES = 128
HEAD_DIM = 64
HALF = LANES // 2
POOL_WINDOWS = (2, 4, 8, 16)
POOL_HALO = 8
DIL_PAIRS = ((128, 1), (512, 4), (2048, 16))
DIL_RADIUS = 64
WIN_RADIUS = 128
WIN_GROUP = 3
N_MOD = 6
EPS = 1e-6
NEG = -1e30
QBLK = 128
ONES_ROWS = 16
LOG2E = math.log2(math.e)
BAND_UNROLL = 5
VMEM_LIMIT = 56 * 1024 * 1024

_F32 = jnp.float32
_BF16 = jnp.bfloat16


def _alibi_slopes(n):
    i = np.arange(1, n + 1, dtype=np.float32)
    return np.exp2(np.float32(-8.0) * i / np.float32(n)).astype(np.float32)


def _cparams(sem):
    return pltpu.CompilerParams(dimension_semantics=sem, vmem_limit_bytes=VMEM_LIMIT)


def _mod_kernel(c_ref, w_ref, b_ref, o_ref):
    c = c_ref[...]
    act = (c * (1.0 / (1.0 + jnp.exp(-c)))).astype(_BF16)
    w = w_ref[0].astype(_BF16)
    o_ref[0] = jnp.dot(act, w, preferred_element_type=_F32) + b_ref[0]


def _modulation(c, w_ada, b_ada):
    depth, d, nd = w_ada.shape
    b = c.shape[0]
    tn = d
    return pl.pallas_call(
        _mod_kernel,
        out_shape=jax.ShapeDtypeStruct((depth, b, nd), _F32),
        grid=(depth, nd // tn),
        in_specs=[
            pl.BlockSpec((b, d), lambda l, j: (0, 0)),
            pl.BlockSpec((1, d, tn), lambda l, j: (l, 0, j)),
            pl.BlockSpec((1, 1, tn), lambda l, j: (l, 0, j)),
        ],
        out_specs=pl.BlockSpec((1, b, tn), lambda l, j: (l, 0, j)),
        compiler_params=_cparams(("arbitrary", "arbitrary")),
        name="modulation",
    )(c, w_ada, b_ada.reshape(depth, 1, nd))


def _modulated_norm(x, g, shift, scale):
    ms = jnp.mean(x * x, axis=-1, keepdims=True)
    return (x * lax.rsqrt(ms + EPS)) * (g * (1.0 + scale)) + shift


def _pooled_tokens(ubuf_ref, t0, s_len):
    tm = ubuf_ref.shape[0] - 2 * POOL_HALO
    cols = []
    for c in range(2):
        r_lo, r_hi = POOL_WINDOWS[2 * c] // 2, POOL_WINDOWS[2 * c + 1] // 2
        lanes = slice(c * LANES, (c + 1) * LANES)
        lane = lax.broadcasted_iota(jnp.int32, (1, LANES), 1)
        radius = jnp.where(lane < HALF, r_lo, r_hi)
        u = ubuf_ref[POOL_HALO:POOL_HALO + tm, lanes]
        wsum = u
        for k in range(1, r_hi + 1):
            pair = (ubuf_ref[POOL_HALO - k:POOL_HALO - k + tm, lanes]
                    + ubuf_ref[POOL_HALO + k:POOL_HALO + k + tm, lanes])
            wsum = wsum + (pair if k <= r_lo else jnp.where(lane < HALF, 0.0, pair))
        t = t0 + lax.broadcasted_iota(jnp.int32, (tm, LANES), 0)
        cnt = jnp.minimum(t + radius + 1, s_len) - jnp.maximum(t - radius, 0)
        cols.append((wsum / cnt.astype(_F32) - u).astype(_BF16))
    return jnp.concatenate(cols, axis=1)


def _in_proj_kernel(*refs, s_len, n_cast):
    for src, dst in zip(refs[:n_cast], refs[len(refs) - 2 - n_cast:len(refs) - 2]):
        dst[...] = src[...].astype(dst.dtype)
    (x_ref, xp_ref, xn_ref, mod_ref, g_ref, w_ref, wp_ref, ps_ref,
     ya_ref, q0_ref, k0_ref, v0_ref, q1_ref, k1_ref, v1_ref,
     q2_ref, k2_ref, v2_ref, qc_ref, kc_ref, vct_ref) = refs[n_cast:len(refs) - 2 - n_cast]
    zs_ref, ubuf_ref = refs[-2:]
    tm = x_ref.shape[1]
    i = pl.program_id(1)
    cw = 2 * LANES
    mod = mod_ref[0, 0]
    norm = lambda rows: _modulated_norm(rows, g_ref[...], mod[0:1], mod[1:2]).astype(_BF16)
    h = norm(x_ref[0])

    def proj(lhs, t0, t1):
        return jnp.dot(lhs, w_ref[:, t0 * LANES:t1 * LANES], preferred_element_type=_F32)

    halo = proj(norm(jnp.concatenate([xp_ref[0], xn_ref[0]], axis=0)), 0, 2)
    ubuf_ref[0:POOL_HALO] = jnp.where(i > 0, halo[:POOL_HALO], 0.0)
    ubuf_ref[POOL_HALO:POOL_HALO + tm] = proj(h, 0, 2)
    ubuf_ref[POOL_HALO + tm:] = jnp.where(i < pl.num_programs(1) - 1, halo[POOL_HALO:], 0.0)
    pooled = _pooled_tokens(ubuf_ref, i * tm, s_len)

    z_lo, z_hi = proj(h, 2, 8), proj(h, 8, 16)

    def tile(i):
        z, j = (z_lo, i - 2) if i < 8 else (z_hi, i - 8)
        return z[:, j * LANES:(j + 1) * LANES]

    for j, ref in enumerate((q0_ref, k0_ref, v0_ref)):
        ref[0] = tile(2 + j).astype(_BF16)
    for refs, (_, dil), base in (((q1_ref, k1_ref, v1_ref), DIL_PAIRS[1], 5),
                                 ((q2_ref, k2_ref, v2_ref), DIL_PAIRS[2], 8)):
        rows = tm // dil
        for j, ref in enumerate(refs):
            zs_ref[base - 5 + j] = tile(base + j)
            for r in range(dil):
                ref[0, r] = zs_ref[base - 5 + j, pl.ds(r, rows, stride=dil), :].astype(_BF16)
    for t in range(WIN_GROUP):
        qc_ref[0, t] = tile(11 + t).astype(_BF16)
    kc_ref[0] = tile(14).astype(_BF16)
    vct_ref[0] = tile(15).T.astype(_BF16)
    ya = jnp.dot(pooled, wp_ref[...], preferred_element_type=_F32) * ps_ref[...]
    ya_ref[0] = ya.astype(_BF16)


def _in_proj(x, mod, l, g, w_in_p, wpool_bd, pool_scale, tm, to_cast=()):
    b, s, d = x.shape
    n_in = w_in_p.shape[-1]
    cw = 2 * LANES
    hb = tm // POOL_HALO
    nhb = s // POOL_HALO
    d1, d2 = DIL_PAIRS[1][1], DIL_PAIRS[2][1]
    nat = lambda: pl.BlockSpec((1, tm, LANES), lambda bi, i: (bi, i, 0))
    res = lambda dil: pl.BlockSpec((1, dil, tm // dil, LANES), lambda bi, i: (bi, 0, i, 0))
    full = lambda shape: pl.BlockSpec((None,) + shape, lambda bi, i: (l,) + (0,) * len(shape))
    sds = jax.ShapeDtypeStruct
    out_shape = (
        [sds((b, s, cw), _BF16)]
        + [sds((b, s, LANES), _BF16)] * 3
        + [sds((b, d1, s // d1, LANES), _BF16)] * 3
        + [sds((b, d2, s // d2, LANES), _BF16)] * 3
        + [sds((b, WIN_GROUP, s, LANES), _BF16)]
        + [sds((b, s, LANES), _BF16), sds((b, LANES, s), _BF16)]
    )
    out_specs = (
        [pl.BlockSpec((1, tm, 2 * LANES), lambda bi, i: (bi, i, 0))]
        + [nat() for _ in range(3)]
        + [res(d1) for _ in range(3)]
        + [res(d2) for _ in range(3)]
        + [pl.BlockSpec((1, WIN_GROUP, tm, LANES), lambda bi, i: (bi, 0, i, 0))]
        + [nat(), pl.BlockSpec((1, LANES, tm), lambda bi, i: (bi, 0, i))]
    )
    nt = s // tm
    cast_in, cast_specs = [], []
    for w in to_cast:
        w2 = w.reshape(-1, w.shape[-1])
        slab = w2.shape[0] // (b * nt)
        assert slab * b * nt == w2.shape[0] and slab % 16 == 0
        cast_in.append(w2)
        cast_specs.append(pl.BlockSpec((slab, w2.shape[1]), lambda bi, i: (bi * nt + i, 0)))
    outs = pl.pallas_call(
        functools.partial(_in_proj_kernel, s_len=s, n_cast=len(to_cast)),
        out_shape=out_shape + [sds(w2.shape, _BF16) for w2 in cast_in],
        grid=(b, nt),
        in_specs=cast_specs + [
            pl.BlockSpec((1, tm, d), lambda bi, i: (bi, i, 0)),
            pl.BlockSpec((1, POOL_HALO, d), lambda bi, i: (bi, jnp.maximum(i * hb - 1, 0), 0)),
            pl.BlockSpec((1, POOL_HALO, d),
                         lambda bi, i: (bi, jnp.minimum((i + 1) * hb, nhb - 1), 0)),
            pl.BlockSpec((1, 1, N_MOD, d), lambda bi, i: (l, bi, 0, 0)),
            full((1, d)),
            full((d, n_in)),
            full((cw, cw)),
            full((1, cw)),
        ],
        out_specs=out_specs + cast_specs,
        scratch_shapes=[pltpu.VMEM((6, tm, LANES), _F32),
                        pltpu.VMEM((tm + 2 * POOL_HALO, cw), _F32)],
        compiler_params=_cparams(("arbitrary", "arbitrary")),
        name="in_proj",
    )(*cast_in, x, x, x, mod, g, w_in_p, wpool_bd, pool_scale)
    n_main = len(out_shape)
    return outs[:n_main], [o.reshape(w.shape) for o, w in zip(outs[n_main:], to_cast)]


def _band_bias_t(shape, col_heads, slopes, dist_scale, radius, offset):
    row = lax.broadcasted_iota(jnp.int32, shape, 0)
    col = lax.broadcasted_iota(jnp.int32, shape, 1)
    dist = jnp.abs(row - offset - (col % QBLK))
    slope = jnp.full(shape, slopes[col_heads[0]] * dist_scale, _F32)
    for i, hd in enumerate(col_heads[1:], start=1):
        slope = jnp.where(col >= i * QBLK, np.float32(slopes[hd] * dist_scale), slope)
    return jnp.where(dist <= radius, (-slope * dist.astype(_F32)) * LOG2E, NEG)


def _window(i, nblk, radius, width):
    seq = nblk * QBLK
    ws = jnp.clip(i * QBLK - radius, 0, seq - width)
    variant = jnp.where(i == 0, 0, jnp.where(i == nblk - 1, 2, 1))
    return ws, variant


def _trans_b_dot(a, b):
    return lax.dot_general(a, b, (((1,), (1,)), ((), ())), preferred_element_type=_F32)


def _band_attn_kernel(q_ref, k_ref, v_ref, o_ref, lse_ref, bias_ref, *scratch, dil, slopes):
    s_len = q_ref.shape[1]
    n = s_len // dil
    nblk = n // QBLK
    width = QBLK + 2 * DIL_RADIUS
    cols = 2 * QBLK

    @pl.when(pl.program_id(0) == 0)
    def _():
        for var, off in enumerate((0, DIL_RADIUS, width - QBLK)):
            bias_ref[var] = _band_bias_t((width, cols), (0, 1), slopes, float(dil),
                                         DIL_RADIUS, off)

    lo = lax.broadcasted_iota(jnp.int32, (QBLK, LANES), 1) < HALF
    s_bufs, e_bufs, m_bufs = scratch[0:2], scratch[2:4], scratch[4:6]
    ntot = s_len // QBLK

    def locate(blk):
        r = blk // nblk
        i = blk % nblk
        ws, variant = _window(i, nblk, DIL_RADIUS, width)
        return r, i, pl.multiple_of(r * n + ws, DIL_RADIUS), variant

    def scores(blk, par):
        _, _, kstart, variant = locate(blk)
        qstart = pl.multiple_of(blk * QBLK, QBLK)
        q = q_ref[0, pl.ds(qstart, QBLK), :]
        kw = k_ref[0, pl.ds(kstart, width), :]
        zero = jnp.zeros_like(q)
        qs = jnp.concatenate([jnp.where(lo, q, zero), jnp.where(lo, zero, q)], axis=0)
        s_bufs[par][...] = _trans_b_dot(kw, qs) + bias_ref[variant]

    def numerators(par):
        sc = s_bufs[par][...]
        m = jnp.max(sc, axis=0, keepdims=True)
        e_bufs[par][...] = jnp.exp2(sc - m).astype(_BF16)
        m_bufs[par][...] = jnp.broadcast_to(m, (8, cols))

    def outputs(blk, par):
        r, i, kstart, _ = locate(blk)
        vt = v_ref[0, pl.ds(kstart, width), :].T
        lhs = jnp.concatenate([vt, jnp.ones((ONES_ROWS, width), _BF16)], axis=0)
        o = jnp.dot(lhs, e_bufs[par][...], preferred_element_type=_F32)
        den = o[LANES:LANES + 1]
        lse_row = m_bufs[par][0:1] * (1.0 / LOG2E) + jnp.log(den)
        res = jnp.concatenate(
            [o[h * HEAD_DIM:(h + 1) * HEAD_DIM, h * QBLK:(h + 1) * QBLK]
             / den[:, h * QBLK:(h + 1) * QBLK] for h in range(2)], axis=0)
        lse2 = jnp.concatenate(
            [jnp.broadcast_to(lse_row[:, h * QBLK:(h + 1) * QBLK], (HEAD_DIM, QBLK))
             for h in range(2)], axis=0)
        out, lse = res.T, lse2.T
        if dil == 1:
            qstart = pl.multiple_of(blk * QBLK, QBLK)
            o_ref[0, pl.ds(qstart, QBLK), :] = out.astype(_BF16)
            lse_ref[0, pl.ds(qstart, QBLK), :] = lse
        else:
            tstart = i * (QBLK * dil) + r
            scratch[6][pl.ds(tstart, QBLK, stride=dil), :] = out
            lse_ref[0, pl.ds(tstart, QBLK, stride=dil), :] = lse

    scores(0, 0)
    scores(1, 1)
    numerators(0)

    def body(p, carry):
        it = 2 * p
        scores(it, 0)
        numerators(1)
        outputs(it - 2, 0)
        scores(it + 1, 1)
        numerators(0)
        outputs(it - 1, 1)
        return carry

    lax.fori_loop(1, ntot // 2, body, 0, unroll=BAND_UNROLL)
    numerators(1)
    outputs(ntot - 2, 0)
    outputs(ntot - 1, 1)
    if dil != 1:
        o_ref[0] = scratch[6][...].astype(_BF16)


def _band_attn(q, k, v, dil, slopes):
    b, s, _ = q.shape
    width = QBLK + 2 * DIL_RADIUS
    spec = lambda: pl.BlockSpec((1, s, LANES), lambda bi: (bi, 0, 0))
    cols = 2 * QBLK
    scratch = ([pltpu.VMEM((3, width, cols), _F32)]
               + [pltpu.VMEM((width, cols), _F32)] * 2
               + [pltpu.VMEM((width, cols), _BF16)] * 2
               + [pltpu.VMEM((8, cols), _F32)] * 2)
    if dil != 1:
        scratch.append(pltpu.VMEM((s, LANES), _F32))
    return pl.pallas_call(
        functools.partial(_band_attn_kernel, dil=dil, slopes=slopes),
        out_shape=[jax.ShapeDtypeStruct((b, s, LANES), _BF16),
                   jax.ShapeDtypeStruct((b, s, LANES), _F32)],
        grid=(b,),
        in_specs=[spec(), spec(), spec()],
        out_specs=[spec(), spec()],
        scratch_shapes=scratch,
        compiler_params=_cparams(("arbitrary",)),
        name=f"band_attn_d{dil}",
    )(q, k, v)


def _win_attn_kernel(sink_ref, q_ref, k_ref, vt_ref, o_ref, bias_ref,
                     s0_ref, s1_ref, e0_ref, e1_ref, t0_ref, t1_ref, *, slopes, layer):
    s_len = k_ref.shape[1]
    tc = q_ref.shape[2]
    nblk = s_len // QBLK
    nloc = tc // QBLK
    width = QBLK + 2 * WIN_RADIUS
    cols = WIN_GROUP * QBLK
    s_bufs, e_bufs, t_bufs = (s0_ref, s1_ref), (e0_ref, e1_ref), (t0_ref, t1_ref)

    @pl.when((pl.program_id(0) == 0) & (pl.program_id(1) == 0))
    def _():
        for var, off in enumerate((0, WIN_RADIUS, width - QBLK)):
            for j in range(2):
                heads = tuple(j * WIN_GROUP + t for t in range(WIN_GROUP))
                bias_ref[var, j] = _band_bias_t((width, cols), heads, slopes, 1.0,
                                                WIN_RADIUS, off)

    qlane = lax.broadcasted_iota(jnp.int32, (1, cols), 1)
    klo = lax.broadcasted_iota(jnp.int32, (width, LANES), 1) < HALF
    base = pl.program_id(1) * nloc

    def sink_row(j):
        row = jnp.full((1, cols), sink_ref[layer, j * WIN_GROUP] * LOG2E, _F32)
        for t in range(1, WIN_GROUP):
            row = jnp.where(qlane >= t * QBLK, sink_ref[layer, j * WIN_GROUP + t] * LOG2E, row)
        return row

    def kstart_of(bl):
        ws, variant = _window(base + bl, nblk, WIN_RADIUS, width)
        return pl.multiple_of(ws, QBLK), variant

    def scores(bl, par):
        kstart, variant = kstart_of(bl)
        qstart = pl.multiple_of(bl * QBLK, QBLK)
        q = q_ref[0, :, pl.ds(qstart, QBLK), :].reshape(cols, LANES)
        kw = k_ref[0, pl.ds(kstart, width), :]
        zero = jnp.zeros_like(kw)
        for j in range(2):
            kh = jnp.where(klo, kw, zero) if j == 0 else jnp.where(klo, zero, kw)
            s_bufs[par][j] = _trans_b_dot(kh, q) + bias_ref[variant, j]

    def numerators(par):
        for j in range(2):
            sc = s_bufs[par][j]
            sink = sink_row(j)
            m = jnp.maximum(jnp.max(sc, axis=0, keepdims=True), sink)
            e_bufs[par][j] = jnp.exp2(sc - m).astype(_BF16)
            t_bufs[par][j] = jnp.broadcast_to(jnp.exp2(sink - m), (8, cols))

    def outputs(bl, par):
        kstart, _ = kstart_of(bl)
        qstart = pl.multiple_of(bl * QBLK, QBLK)
        vt = vt_ref[0, :, pl.ds(kstart, width)]
        ones = jnp.ones((ONES_ROWS, width), _BF16)
        res = []
        for j in range(2):
            lhs = jnp.concatenate([vt[j * HEAD_DIM:(j + 1) * HEAD_DIM], ones], axis=0)
            o = jnp.dot(lhs, e_bufs[par][j], preferred_element_type=_F32)
            den = o[HEAD_DIM:HEAD_DIM + 1] + t_bufs[par][j][0:1]
            res.append(o[:HEAD_DIM] / den)
        for t in range(WIN_GROUP):
            both = jnp.concatenate([r[:, t * QBLK:(t + 1) * QBLK] for r in res], axis=0)
            o_ref[0, pl.ds(qstart, QBLK), t * LANES:(t + 1) * LANES] = both.T.astype(_BF16)

    scores(0, 0)
    scores(1, 1)
    numerators(0)

    def body(p, carry):
        it = 2 * p
        scores(it, 0)
        numerators(1)
        outputs(it - 2, 0)
        scores(it + 1, 1)
        numerators(0)
        outputs(it - 1, 1)
        return carry

    lax.fori_loop(1, nloc // 2, body, 0, unroll=3)
    numerators(1)
    outputs(nloc - 2, 0)
    outputs(nloc - 1, 1)


def _win_attn(sink, layer, q, k, vt, slopes, tc):
    b, _, s, _ = q.shape
    width = QBLK + 2 * WIN_RADIUS
    cols = WIN_GROUP * QBLK
    assert (tc // QBLK) % 2 == 0 and tc // QBLK >= 4
    return pl.pallas_call(
        functools.partial(_win_attn_kernel, slopes=slopes, layer=layer),
        out_shape=jax.ShapeDtypeStruct((b, s, WIN_GROUP * LANES), _BF16),
        grid=(b, s // tc),
        in_specs=[
            pl.BlockSpec(memory_space=pltpu.SMEM),
            pl.BlockSpec((1, WIN_GROUP, tc, LANES), lambda bi, i: (bi, 0, i, 0)),
            pl.BlockSpec((1, s, LANES), lambda bi, i: (bi, 0, 0)),
            pl.BlockSpec((1, LANES, s), lambda bi, i: (bi, 0, 0)),
        ],
        out_specs=pl.BlockSpec((1, tc, WIN_GROUP * LANES), lambda bi, i: (bi, i, 0)),
        scratch_shapes=[pltpu.VMEM((3, 2, width, cols), _F32)]
                       + [pltpu.VMEM((2, width, cols), _F32)] * 2
                       + [pltpu.VMEM((2, width, cols), _BF16)] * 2
                       + [pltpu.VMEM((2, 8, cols), _F32)] * 2,
        compiler_params=_cparams(("arbitrary", "arbitrary")),
        name="win_attn",
    )(sink, q, k, vt)


def _token_kernel(x_ref, mod_ref, ya_ref,
                  o0_ref, o1_ref, o2_ref, l0_ref, l1_ref, l2_ref, yc_ref, wo_ref,
                  g2_ref, wg_ref, wu_ref, wd_ref, fg_ref,
                  out_ref, mix_ref, *, final, ff_chunks):
    cw = 2 * LANES
    mod = mod_ref[0, 0]
    mix_ref[:, :cw] = ya_ref[0]

    lses = [l0_ref[0], l1_ref[0], l2_ref[0]]
    mx = jnp.maximum(jnp.maximum(lses[0], lses[1]), lses[2])
    es = [jnp.exp(ls - mx) for ls in lses]
    den = es[0] + es[1] + es[2]
    for g, o_ref in enumerate((o0_ref, o1_ref, o2_ref)):
        yb = (es[g] / den) * o_ref[0]
        mix_ref[:, cw + g * LANES:cw + (g + 1) * LANES] = yb.astype(_BF16)
    mix_ref[:, cw + 3 * LANES:] = yc_ref[0]

    out_ref[0] = x_ref[0] + mod[2:3] * jnp.dot(mix_ref[...], wo_ref[...],
                                               preferred_element_type=_F32)

    h = _modulated_norm(out_ref[0], g2_ref[...], mod[3:4], mod[4:5]).astype(_BF16)
    for c0, c1 in ff_chunks:
        gate = jnp.dot(h, wg_ref[:, c0:c1], preferred_element_type=_F32)
        up = jnp.dot(h, wu_ref[:, c0:c1], preferred_element_type=_F32)
        act = ((gate * (1.0 / (1.0 + jnp.exp(-gate)))) * up).astype(_BF16)
        out_ref[0] += mod[5:6] * jnp.dot(act, wd_ref[c0:c1, :], preferred_element_type=_F32)
    if final:
        y = out_ref[0]
        ms = jnp.mean(y * y, axis=-1, keepdims=True)
        out_ref[0] = (y * lax.rsqrt(ms + EPS)) * fg_ref[...]


def _ff_chunks(dff, n_chunks=4):
    mxu_k = 2 * LANES
    tiles = -(-dff // mxu_k)
    cuts = sorted({min(dff, -(-tiles * c // n_chunks) * mxu_k) for c in range(n_chunks + 1)})
    return tuple(zip(cuts[:-1], cuts[1:]))


def _token_mixers_out(x, mod, l, ya, outs, lses, yc, w_out_p,
                      g2, wg, wu, wd, final_g, final, tm):
    b, s, d = x.shape
    dff = wg.shape[-1]
    cw = 2 * LANES
    tok = lambda width: pl.BlockSpec((1, tm, width), lambda bi, i: (bi, i, 0))
    const = lambda shape: pl.BlockSpec((None,) + shape, lambda bi, i: (l,) + (0,) * len(shape),
                                       pipeline_mode=pl.Buffered(1))
    return pl.pallas_call(
        functools.partial(_token_kernel, final=final, ff_chunks=_ff_chunks(dff)),
        out_shape=jax.ShapeDtypeStruct((b, s, d), _F32),
        grid=(b, s // tm),
        in_specs=[
            tok(d),
            pl.BlockSpec((1, 1, N_MOD, d), lambda bi, i: (l, bi, 0, 0)),
            tok(cw),
            tok(LANES), tok(LANES), tok(LANES),
            tok(LANES), tok(LANES), tok(LANES),
            tok(WIN_GROUP * LANES),
            const((d, d)),
            const((1, d)),
            const((d, dff)), const((d, dff)), const((dff, d)),
            pl.BlockSpec((1, d), lambda bi, i: (0, 0)),
        ],
        out_specs=tok(d),
        scratch_shapes=[pltpu.VMEM((tm, d), _BF16)],
        compiler_params=_cparams(("arbitrary", "arbitrary")),
        name="token_mix_ffn",
    )(x, mod, ya, *outs, *lses, yc, w_out_p, g2, wg, wu, wd, final_g)


def _in_proj_columns(d_model):
    pool_w = d_model // 4
    dil_w = 2 * len(DIL_PAIRS) * HEAD_DIM
    o_qb = pool_w
    o_kb = o_qb + dil_w
    o_vb = o_kb + dil_w
    o_qc = o_vb + dil_w
    o_kc = o_qc + 2 * WIN_GROUP * HEAD_DIM
    o_vc = o_kc + 2 * HEAD_DIM
    cols = list(range(pool_w))
    for g in range(len(DIL_PAIRS)):
        for base in (o_qb, o_kb, o_vb):
            cols += range(base + g * LANES, base + (g + 1) * LANES)
    win_heads = []
    for t in range(WIN_GROUP):
        for j in range(2):
            win_heads.append(j * WIN_GROUP + t)
            cols += range(o_qc + (j * WIN_GROUP + t) * HEAD_DIM,
                          o_qc + (j * WIN_GROUP + t + 1) * HEAD_DIM)
    cols += range(o_kc, o_kc + LANES)
    cols += range(o_vc, o_vc + LANES)
    qscale = np.ones((len(cols),), np.float32)
    for g in range(len(DIL_PAIRS)):
        qscale[pool_w + 3 * g * LANES:pool_w + (3 * g + 1) * LANES] = HEAD_DIM ** -0.5 * LOG2E
    q0 = pool_w + 3 * len(DIL_PAIRS) * LANES
    qscale[q0:q0 + WIN_GROUP * LANES] = HEAD_DIM ** -0.5 * LOG2E
    return np.asarray(cols), qscale, win_heads


def _runs(idx, key=None):
    idx = np.asarray(idx)
    key = np.zeros(len(idx)) if key is None else np.asarray(key)
    cuts = ([0] + [i for i in range(1, len(idx))
                   if idx[i] != idx[i - 1] + 1 or key[i] != key[i - 1]] + [len(idx)])
    return [(int(idx[a]), b - a, a) for a, b in zip(cuts[:-1], cuts[1:])]


def _take_runs(w, idx, axis):
    parts = [lax.slice_in_dim(w, src, src + n, axis=axis) for src, n, _ in _runs(idx)]
    return jnp.concatenate(parts, axis=axis)


def _permute_cols_kernel(w_ref, o_ref, *, runs):
    for src, n, dst, scale in runs:
        o_ref[0, :, dst:dst + n] = (w_ref[0, :, src:src + n] * scale).astype(o_ref.dtype)


def _permute_cols_bf16(w, cols, scale, tr):
    depth, rows, n = w.shape
    runs = [(src, length, dst, float(scale[dst])) for src, length, dst in _runs(cols, scale)]
    spec = pl.BlockSpec((1, tr, n), lambda l, i: (l, i, 0))
    return pl.pallas_call(
        functools.partial(_permute_cols_kernel, runs=tuple(runs)),
        out_shape=jax.ShapeDtypeStruct(w.shape, _BF16),
        grid=(depth, rows // tr),
        in_specs=[spec],
        out_specs=spec,
        compiler_params=_cparams(("arbitrary", "arbitrary")),
        name="permute_cols",
    )(w)


def kernel(x, c, norm1_g, norm2_g, w_ada, b_ada, w_in, w_pool, pool_scale, sink_logit,
           w_out, w_gate, w_up, w_down, final_g):
    b, s, d = x.shape
    depth = w_in.shape[0]
    slopes = _alibi_slopes(2 * WIN_GROUP + 2 * len(DIL_PAIRS))
    slopes_win = tuple(float(v) for v in slopes[:2 * WIN_GROUP])
    slopes_dil = slopes[2 * WIN_GROUP:]
    cols, qscale, win_heads = _in_proj_columns(d)
    pool_w = d // 4
    dil_w = 2 * len(DIL_PAIRS) * HEAD_DIM
    mix_rows = np.concatenate([
        np.arange(pool_w + dil_w),
        np.concatenate([pool_w + dil_w + h * HEAD_DIM + np.arange(HEAD_DIM) for h in win_heads]),
    ])
    n_grp = len(POOL_WINDOWS)
    pg = pool_w // n_grp

    mod = _modulation(c, w_ada, b_ada).reshape(depth, b, N_MOD, d)
    w_in_p = _permute_cols_bf16(w_in, cols, qscale, 256)
    w_out_p = _take_runs(w_out, mix_rows, 1).astype(_BF16)
    zpad = lambda n: jnp.zeros((depth, pg, n * pg), _F32)
    wpool_bd = jnp.concatenate(
        [jnp.concatenate([zpad(g), w_pool[:, g], zpad(n_grp - 1 - g)], axis=2)
         for g in range(n_grp)], axis=1).astype(_BF16)
    g1, g2 = norm1_g.reshape(depth, 1, d), norm2_g.reshape(depth, 1, d)
    ps = pool_scale.reshape(depth, 1, pool_w)
    tm = 512
    for l in range(depth):
        (ya, q0, k0, v0, q1, k1, v1, q2, k2, v2, qc, kc, vc), cast = _in_proj(
            x, mod, l, g1, w_in_p, wpool_bd, ps, 2 * tm,
            to_cast=(w_gate, w_up, w_down) if l == 0 else ())
        if l == 0:
            wg, wu, wd = cast
        outs, lses = [], []
        for g, ((_, dil), (q, k, v)) in enumerate(zip(
                DIL_PAIRS, ((q0, k0, v0), (q1, k1, v1), (q2, k2, v2)))):
            sl = tuple(float(v_) for v_ in slopes_dil[2 * g:2 * g + 2])
            o, ls = _band_attn(q.reshape(b, s, LANES), k.reshape(b, s, LANES),
                               v.reshape(b, s, LANES), dil, sl)
            outs.append(o)
            lses.append(ls)
        yc = _win_attn(sink_logit, l, qc, kc, vc, slopes_win, s)
        x = _token_mixers_out(x, mod, l, ya, outs, lses, yc, w_out_p, g2, wg, wu, wd,
                              final_g.reshape(1, d), l == depth - 1, 2 * tm)
    return x
```

```python
import functools
import math

import jax
import jax.numpy as jnp
import numpy as np
from jax import lax
from jax.experimental import pallas as pl
from jax.experimental.pallas import tpu as pltpu

LANES = 128
HEAD_DIM = 64
HALF = LANES // 2
POOL_WINDOWS = (2, 4, 8, 16)
POOL_HALO = 8
REGROUP_STRIDE = 4
POOL_PAD = 16
DIL_PAIRS = ((128, 1), (512, 4), (2048, 16))
DIL_RADIUS = 64
WIN_RADIUS = 128
WIN_GROUP = 3
N_MOD = 6
EPS = 1e-6
NEG = -1e30
QBLK = 128
ONES_ROWS = 16
LOG2E = math.log2(math.e)
BAND_UNROLL = 5
VMEM_LIMIT = 56 * 1024 * 1024

_F32 = jnp.float32
_BF16 = jnp.bfloat16


def _alibi_slopes(n):
    i = np.arange(1, n + 1, dtype=np.float32)
    return np.exp2(np.float32(-8.0) * i / np.float32(n)).astype(np.float32)


def _cparams(sem):
    return pltpu.CompilerParams(dimension_semantics=sem, vmem_limit_bytes=VMEM_LIMIT)


def _mod_kernel(c_ref, w_ref, b_ref, o_ref):
    c = c_ref[...]
    act = (c * (1.0 / (1.0 + jnp.exp(-c)))).astype(_BF16)
    w = w_ref[0].astype(_BF16)
    o_ref[0] = jnp.dot(act, w, preferred_element_type=_F32) + b_ref[0]


def _modulation(c, w_ada, b_ada):
    depth, d, nd = w_ada.shape
    b = c.shape[0]
    tn = d
    return pl.pallas_call(
        _mod_kernel,
        out_shape=jax.ShapeDtypeStruct((depth, b, nd), _F32),
        grid=(depth, nd // tn),
        in_specs=[
            pl.BlockSpec((b, d), lambda l, j: (0, 0)),
            pl.BlockSpec((1, d, tn), lambda l, j: (l, 0, j)),
            pl.BlockSpec((1, 1, tn), lambda l, j: (l, 0, j)),
        ],
        out_specs=pl.BlockSpec((1, b, tn), lambda l, j: (l, 0, j)),
        compiler_params=_cparams(("arbitrary", "arbitrary")),
        name="modulation",
    )(c, w_ada, b_ada.reshape(depth, 1, nd))


def _modulated_norm(x, g, shift, scale):
    ms = jnp.mean(x * x, axis=-1, keepdims=True)
    return (x * lax.rsqrt(ms + EPS)) * (g * (1.0 + scale)) + shift


def _pooled_tokens(ubuf_ref, t0, s_len):
    tm = ubuf_ref.shape[0] - 2 * POOL_HALO - POOL_PAD
    cols = []
    for c in range(2):
        r_lo, r_hi = POOL_WINDOWS[2 * c] // 2, POOL_WINDOWS[2 * c + 1] // 2
        lanes = slice(c * LANES, (c + 1) * LANES)
        rows_from = lambda k0, cnt: ubuf_ref[k0:k0 + cnt, lanes]
        lane = lax.broadcasted_iota(jnp.int32, (1, LANES), 1)
        radius = jnp.where(lane < HALF, r_lo, r_hi)
        u = rows_from(POOL_HALO, tm)
        if r_lo < 2:
            wsum = u
            for k in range(1, r_hi + 1):
                pair = rows_from(POOL_HALO - k, tm) + rows_from(POOL_HALO + k, tm)
                wsum = wsum + (pair if k <= r_lo else jnp.where(lane < HALF, 0.0, pair))
        else:
            rows = tm + 3 * POOL_HALO
            level, span, wsums = rows_from(0, rows) + rows_from(1, rows), 2, {}
            while True:
                if span in (r_lo, r_hi):
                    wsums[span] = (level[POOL_HALO - span:POOL_HALO - span + tm]
                                   + level[POOL_HALO:POOL_HALO + tm]
                                   + rows_from(POOL_HALO + span, tm))
                if span >= r_hi:
                    break
                rows -= POOL_HALO
                level = level[:rows] + level[span:span + rows]
                span *= 2
            wsum = jnp.where(lane < HALF, wsums[r_lo], wsums[r_hi])
        t = t0 + lax.broadcasted_iota(jnp.int32, (tm, LANES), 0)
        cnt = jnp.minimum(t + radius + 1, s_len) - jnp.maximum(t - radius, 0)
        cols.append((wsum / cnt.astype(_F32) - u).astype(_BF16))
    return jnp.concatenate(cols, axis=1)


def _in_proj_kernel(*refs, s_len, n_cast):
    for src, dst in zip(refs[:n_cast], refs[len(refs) - 3 - n_cast:len(refs) - 3]):
        dst[...] = src[...].astype(dst.dtype)
    (x_ref, xp_ref, xn_ref, mod_ref, g_ref, w_ref, wp_ref, ps_ref,
     ya_ref, q0_ref, k0_ref, v0_ref, q1_ref, k1_ref, v1_ref,
     q2_ref, k2_ref, v2_ref, qc_ref, kc_ref, vct_ref) = refs[n_cast:len(refs) - 3 - n_cast]
    zs_ref, zt_ref, ubuf_ref = refs[-3:]
    tm = x_ref.shape[1]
    i = pl.program_id(1)
    cw = 2 * LANES
    mod = mod_ref[0, 0]
    norm = lambda rows: _modulated_norm(rows, g_ref[...], mod[0:1], mod[1:2]).astype(_BF16)
    h = norm(x_ref[0])

    def proj(lhs, t0, t1):
        return jnp.dot(lhs, w_ref[:, t0 * LANES:t1 * LANES], preferred_element_type=_F32)

    halo = proj(norm(jnp.concatenate([xp_ref[0], xn_ref[0]], axis=0)), 0, 2)
    ubuf_ref[0:POOL_HALO] = jnp.where(i > 0, halo[:POOL_HALO], 0.0)
    ubuf_ref[POOL_HALO:POOL_HALO + tm] = proj(h, 0, 2)
    ubuf_ref[POOL_HALO + tm:2 * POOL_HALO + tm] = jnp.where(i < pl.num_programs(1) - 1,
                                                             halo[POOL_HALO:], 0.0)
    ubuf_ref[2 * POOL_HALO + tm:] = jnp.zeros((POOL_PAD, cw), _F32)
    pooled = _pooled_tokens(ubuf_ref, i * tm, s_len)

    z_lo, z_hi = proj(h, 2, 8), proj(h, 8, 16)

    def tile(i):
        z, j = (z_lo, i - 2) if i < 8 else (z_hi, i - 8)
        return z[:, j * LANES:(j + 1) * LANES]

    for j, ref in enumerate((q0_ref, k0_ref, v0_ref)):
        ref[0] = tile(2 + j).astype(_BF16)
    for group_refs, (_, dil), base in (((q1_ref, k1_ref, v1_ref), DIL_PAIRS[1], 5),
                                       ((q2_ref, k2_ref, v2_ref), DIL_PAIRS[2], 8)):
        inner = min(dil, REGROUP_STRIDE)
        outer = dil // inner
        for j, ref in enumerate(group_refs):
            zs_ref[base - 5 + j] = tile(base + j)
            for ra in range(inner):
                part = zs_ref[base - 5 + j, pl.ds(ra, tm // inner, stride=inner), :]
                if outer == 1:
                    ref[0, ra] = part.astype(_BF16)
                else:
                    zt_ref[ra] = part
            for rb in range(outer if outer > 1 else 0):
                for ra in range(inner):
                    ref[0, ra + inner * rb] = zt_ref[
                        ra, pl.ds(rb, tm // dil, stride=outer), :].astype(_BF16)
    for t in range(WIN_GROUP):
        qc_ref[0, t] = tile(11 + t).astype(_BF16)
    kc_ref[0] = tile(14).astype(_BF16)
    vct_ref[0] = tile(15).T.astype(_BF16)
    ya = jnp.dot(pooled, wp_ref[...], preferred_element_type=_F32) * ps_ref[...]
    ya_ref[0] = ya.astype(_BF16)


def _in_proj(x, mod, l, g, w_in_p, wpool_bd, pool_scale, tm, to_cast=()):
    b, s, d = x.shape
    n_in = w_in_p.shape[-1]
    cw = 2 * LANES
    hb = tm // POOL_HALO
    nhb = s // POOL_HALO
    d1, d2 = DIL_PAIRS[1][1], DIL_PAIRS[2][1]
    nat = lambda: pl.BlockSpec((1, tm, LANES), lambda bi, i: (bi, i, 0))
    res = lambda dil: pl.BlockSpec((1, dil, tm // dil, LANES), lambda bi, i: (bi, 0, i, 0))
    full = lambda shape: pl.BlockSpec((None,) + shape, lambda bi, i: (l,) + (0,) * len(shape))
    sds = jax.ShapeDtypeStruct
    out_shape = (
        [sds((b, s, cw), _BF16)]
        + [sds((b, s, LANES), _BF16)] * 3
        + [sds((b, d1, s // d1, LANES), _BF16)] * 3
        + [sds((b, d2, s // d2, LANES), _BF16)] * 3
        + [sds((b, WIN_GROUP, s, LANES), _BF16)]
        + [sds((b, s, LANES), _BF16), sds((b, LANES, s), _BF16)]
    )
    out_specs = (
        [pl.BlockSpec((1, tm, 2 * LANES), lambda bi, i: (bi, i, 0))]
        + [nat() for _ in range(3)]
        + [res(d1) for _ in range(3)]
        + [res(d2) for _ in range(3)]
        + [pl.BlockSpec((1, WIN_GROUP, tm, LANES), lambda bi, i: (bi, 0, i, 0))]
        + [nat(), pl.BlockSpec((1, LANES, tm), lambda bi, i: (bi, 0, i))]
    )
    nt = s // tm
    cast_in, cast_specs = [], []
    for w in to_cast:
        w2 = w.reshape(-1, w.shape[-1])
        slab = w2.shape[0] // (b * nt)
        assert slab * b * nt == w2.shape[0] and slab % 16 == 0
        cast_in.append(w2)
        cast_specs.append(pl.BlockSpec((slab, w2.shape[1]), lambda bi, i: (bi * nt + i, 0)))
    outs = pl.pallas_call(
        functools.partial(_in_proj_kernel, s_len=s, n_cast=len(to_cast)),
        out_shape=out_shape + [sds(w2.shape, _BF16) for w2 in cast_in],
        grid=(b, nt),
        in_specs=cast_specs + [
            pl.BlockSpec((1, tm, d), lambda bi, i: (bi, i, 0)),
            pl.BlockSpec((1, POOL_HALO, d), lambda bi, i: (bi, jnp.maximum(i * hb - 1, 0), 0)),
            pl.BlockSpec((1, POOL_HALO, d),
                         lambda bi, i: (bi, jnp.minimum((i + 1) * hb, nhb - 1), 0)),
            pl.BlockSpec((1, 1, N_MOD, d), lambda bi, i: (l, bi, 0, 0)),
            full((1, d)),
            full((d, n_in)),
            full((cw, cw)),
            full((1, cw)),
        ],
        out_specs=out_specs + cast_specs,
        scratch_shapes=[pltpu.VMEM((6, tm, LANES), _F32),
                        pltpu.VMEM((REGROUP_STRIDE, tm // REGROUP_STRIDE, LANES), _F32),
                        pltpu.VMEM((tm + 2 * POOL_HALO + POOL_PAD, cw), _F32)],
        compiler_params=_cparams(("arbitrary", "arbitrary")),
        name="in_proj",
    )(*cast_in, x, x, x, mod, g, w_in_p, wpool_bd, pool_scale)
    n_main = len(out_shape)
    return outs[:n_main], [o.reshape(w.shape) for o, w in zip(outs[n_main:], to_cast)]


def _band_bias_t(shape, col_heads, slopes, dist_scale, radius, offset):
    row = lax.broadcasted_iota(jnp.int32, shape, 0)
    col = lax.broadcasted_iota(jnp.int32, shape, 1)
    dist = jnp.abs(row - offset - (col % QBLK))
    slope = jnp.full(shape, slopes[col_heads[0]] * dist_scale, _F32)
    for i, hd in enumerate(col_heads[1:], start=1):
        slope = jnp.where(col >= i * QBLK, np.float32(slopes[hd] * dist_scale), slope)
    return jnp.where(dist <= radius, (-slope * dist.astype(_F32)) * LOG2E, NEG)


def _window(i, nblk, radius, width):
    seq = nblk * QBLK
    ws = jnp.clip(i * QBLK - radius, 0, seq - width)
    variant = jnp.where(i == 0, 0, jnp.where(i == nblk - 1, 2, 1))
    return ws, variant


def _trans_b_dot(a, b):
    return lax.dot_general(a, b, (((1,), (1,)), ((), ())), preferred_element_type=_F32)


def _band_attn_kernel(q_ref, k_ref, v_ref, o_ref, lse_ref, bias_ref, *scratch, dil, slopes):
    s_len = q_ref.shape[1]
    n = s_len // dil
    nblk = n // QBLK
    width = QBLK + 2 * DIL_RADIUS
    cols = 2 * QBLK

    @pl.when(pl.program_id(0) == 0)
    def _():
        for var, off in enumerate((0, DIL_RADIUS, width - QBLK)):
            bias_ref[var] = _band_bias_t((width, cols), (0, 1), slopes, float(dil),
                                         DIL_RADIUS, off)

    lo = lax.broadcasted_iota(jnp.int32, (QBLK, LANES), 1) < HALF
    s_bufs, e_bufs, m_bufs = scratch[0:2], scratch[2:4], scratch[4:6]
    ntot = s_len // QBLK

    def locate(blk):
        r = blk // nblk
        i = blk % nblk
        ws, variant = _window(i, nblk, DIL_RADIUS, width)
        return r, i, pl.multiple_of(r * n + ws, DIL_RADIUS), variant

    def scores(blk, par):
        _, _, kstart, variant = locate(blk)
        qstart = pl.multiple_of(blk * QBLK, QBLK)
        q = q_ref[0, pl.ds(qstart, QBLK), :]
        kw = k_ref[0, pl.ds(kstart, width), :]
        zero = jnp.zeros_like(q)
        qs = jnp.concatenate([jnp.where(lo, q, zero), jnp.where(lo, zero, q)], axis=0)
        s_bufs[par][...] = _trans_b_dot(kw, qs) + bias_ref[variant]

    def numerators(par):
        sc = s_bufs[par][...]
        m = jnp.max(sc, axis=0, keepdims=True)
        e_bufs[par][...] = jnp.exp2(sc - m).astype(_BF16)
        m_bufs[par][...] = jnp.broadcast_to(m, (8, cols))

    def outputs(blk, par):
        r, i, kstart, _ = locate(blk)
        vt = v_ref[0, pl.ds(kstart, width), :].T
        lhs = jnp.concatenate([vt, jnp.ones((ONES_ROWS, width), _BF16)], axis=0)
        o = jnp.dot(lhs, e_bufs[par][...], preferred_element_type=_F32)
        den = o[LANES:LANES + 1]
        lse_row = m_bufs[par][0:1] * (1.0 / LOG2E) + jnp.log(den)
        res = jnp.concatenate(
            [o[h * HEAD_DIM:(h + 1) * HEAD_DIM, h * QBLK:(h + 1) * QBLK]
             / den[:, h * QBLK:(h + 1) * QBLK] for h in range(2)], axis=0)
        lse2 = jnp.concatenate(
            [jnp.broadcast_to(lse_row[:, h * QBLK:(h + 1) * QBLK], (HEAD_DIM, QBLK))
             for h in range(2)], axis=0)
        out, lse = res.T, lse2.T
        if dil == 1:
            qstart = pl.multiple_of(blk * QBLK, QBLK)
            o_ref[0, pl.ds(qstart, QBLK), :] = out.astype(_BF16)
            lse_ref[0, pl.ds(qstart, QBLK), :] = lse
        else:
            tstart = i * (QBLK * dil) + r
            scratch[6][pl.ds(tstart, QBLK, stride=dil), :] = out
            lse_ref[0, pl.ds(tstart, QBLK, stride=dil), :] = lse

    scores(0, 0)
    scores(1, 1)
    numerators(0)

    def body(p, carry):
        it = 2 * p
        scores(it, 0)
        numerators(1)
        outputs(it - 2, 0)
        scores(it + 1, 1)
        numerators(0)
        outputs(it - 1, 1)
        return carry

    lax.fori_loop(1, ntot // 2, body, 0, unroll=BAND_UNROLL)
    numerators(1)
    outputs(ntot - 2, 0)
    outputs(ntot - 1, 1)
    if dil != 1:
        o_ref[0] = scratch[6][...].astype(_BF16)


def _band_attn(q, k, v, dil, slopes):
    b, s, _ = q.shape
    width = QBLK + 2 * DIL_RADIUS
    spec = lambda: pl.BlockSpec((1, s, LANES), lambda bi: (bi, 0, 0))
    cols = 2 * QBLK
    scratch = ([pltpu.VMEM((3, width, cols), _F32)]
               + [pltpu.VMEM((width, cols), _F32)] * 2
               + [pltpu.VMEM((width, cols), _BF16)] * 2
               + [pltpu.VMEM((8, cols), _F32)] * 2)
    if dil != 1:
        scratch.append(pltpu.VMEM((s, LANES), _F32))
    return pl.pallas_call(
        functools.partial(_band_attn_kernel, dil=dil, slopes=slopes),
        out_shape=[jax.ShapeDtypeStruct((b, s, LANES), _BF16),
                   jax.ShapeDtypeStruct((b, s, LANES), _F32)],
        grid=(b,),
        in_specs=[spec(), spec(), spec()],
        out_specs=[spec(), spec()],
        scratch_shapes=scratch,
        compiler_params=_cparams(("arbitrary",)),
        name=f"band_attn_d{dil}",
    )(q, k, v)


def _win_attn_kernel(sink_ref, q_ref, k_ref, vt_ref, o_ref, bias_ref,
                     s0_ref, s1_ref, e0_ref, e1_ref, t0_ref, t1_ref, *, slopes, layer):
    s_len = k_ref.shape[1]
    tc = q_ref.shape[2]
    nblk = s_len // QBLK
    nloc = tc // QBLK
    width = QBLK + 2 * WIN_RADIUS
    cols = WIN_GROUP * QBLK
    s_bufs, e_bufs, t_bufs = (s0_ref, s1_ref), (e0_ref, e1_ref), (t0_ref, t1_ref)

    @pl.when((pl.program_id(0) == 0) & (pl.program_id(1) == 0))
    def _():
        for var, off in enumerate((0, WIN_RADIUS, width - QBLK)):
            for j in range(2):
                heads = tuple(j * WIN_GROUP + t for t in range(WIN_GROUP))
                bias_ref[var, j] = _band_bias_t((width, cols), heads, slopes, 1.0,
                                                WIN_RADIUS, off)

    qlane = lax.broadcasted_iota(jnp.int32, (1, cols), 1)
    klo = lax.broadcasted_iota(jnp.int32, (width, LANES), 1) < HALF
    base = pl.program_id(1) * nloc

    def sink_row(j):
        row = jnp.full((1, cols), sink_ref[layer, j * WIN_GROUP] * LOG2E, _F32)
        for t in range(1, WIN_GROUP):
            row = jnp.where(qlane >= t * QBLK, sink_ref[layer, j * WIN_GROUP + t] * LOG2E, row)
        return row

    def kstart_of(bl):
        ws, variant = _window(base + bl, nblk, WIN_RADIUS, width)
        return pl.multiple_of(ws, QBLK), variant

    def scores(bl, par):
        kstart, variant = kstart_of(bl)
        qstart = pl.multiple_of(bl * QBLK, QBLK)
        q = q_ref[0, :, pl.ds(qstart, QBLK), :].reshape(cols, LANES)
        kw = k_ref[0, pl.ds(kstart, width), :]
        zero = jnp.zeros_like(kw)
        for j in range(2):
            kh = jnp.where(klo, kw, zero) if j == 0 else jnp.where(klo, zero, kw)
            s_bufs[par][j] = _trans_b_dot(kh, q) + bias_ref[variant, j]

    def numerators(par):
        for j in range(2):
            sc = s_bufs[par][j]
            sink = sink_row(j)
            m = jnp.maximum(jnp.max(sc, axis=0, keepdims=True), sink)
            e_bufs[par][j] = jnp.exp2(sc - m).astype(_BF16)
            t_bufs[par][j] = jnp.broadcast_to(jnp.exp2(sink - m), (8, cols))

    def outputs(bl, par):
        kstart, _ = kstart_of(bl)
        qstart = pl.multiple_of(bl * QBLK, QBLK)
        vt = vt_ref[0, :, pl.ds(kstart, width)]
        ones = jnp.ones((ONES_ROWS, width), _BF16)
        res = []
        for j in range(2):
            lhs = jnp.concatenate([vt[j * HEAD_DIM:(j + 1) * HEAD_DIM], ones], axis=0)
            o = jnp.dot(lhs, e_bufs[par][j], preferred_element_type=_F32)
            den = o[HEAD_DIM:HEAD_DIM + 1] + t_bufs[par][j][0:1]
            res.append(o[:HEAD_DIM] / den)
        for t in range(WIN_GROUP):
            both = jnp.concatenate([r[:, t * QBLK:(t + 1) * QBLK] for r in res], axis=0)
            o_ref[0, pl.ds(qstart, QBLK), t * LANES:(t + 1) * LANES] = both.T.astype(_BF16)

    scores(0, 0)
    scores(1, 1)
    numerators(0)

    def body(p, carry):
        it = 2 * p
        scores(it, 0)
        numerators(1)
        outputs(it - 2, 0)
        scores(it + 1, 1)
        numerators(0)
        outputs(it - 1, 1)
        return carry

    lax.fori_loop(1, nloc // 2, body, 0, unroll=3)
    numerators(1)
    outputs(nloc - 2, 0)
    outputs(nloc - 1, 1)


def _win_attn(sink, layer, q, k, vt, slopes, tc):
    b, _, s, _ = q.shape
    width = QBLK + 2 * WIN_RADIUS
    cols = WIN_GROUP * QBLK
    assert (tc // QBLK) % 2 == 0 and tc // QBLK >= 4
    return pl.pallas_call(
        functools.partial(_win_attn_kernel, slopes=slopes, layer=layer),
        out_shape=jax.ShapeDtypeStruct((b, s, WIN_GROUP * LANES), _BF16),
        grid=(b, s // tc),
        in_specs=[
            pl.BlockSpec(memory_space=pltpu.SMEM),
            pl.BlockSpec((1, WIN_GROUP, tc, LANES), lambda bi, i: (bi, 0, i, 0)),
            pl.BlockSpec((1, s, LANES), lambda bi, i: (bi, 0, 0)),
            pl.BlockSpec((1, LANES, s), lambda bi, i: (bi, 0, 0)),
        ],
        out_specs=pl.BlockSpec((1, tc, WIN_GROUP * LANES), lambda bi, i: (bi, i, 0)),
        scratch_shapes=[pltpu.VMEM((3, 2, width, cols), _F32)]
                       + [pltpu.VMEM((2, width, cols), _F32)] * 2
                       + [pltpu.VMEM((2, width, cols), _BF16)] * 2
                       + [pltpu.VMEM((2, 8, cols), _F32)] * 2,
        compiler_params=_cparams(("arbitrary", "arbitrary")),
        name="win_attn",
    )(sink, q, k, vt)


def _token_kernel(x_ref, mod_ref, ya_ref,
                  o0_ref, o1_ref, o2_ref, l0_ref, l1_ref, l2_ref, yc_ref, wo_ref,
                  g2_ref, wg_ref, wu_ref, wd_ref, fg_ref,
                  out_ref, mix_ref, *, final, ff_chunks):
    cw = 2 * LANES
    mod = mod_ref[0, 0]
    mix_ref[:, :cw] = ya_ref[0]

    lses = [l0_ref[0], l1_ref[0], l2_ref[0]]
    mx = jnp.maximum(jnp.maximum(lses[0], lses[1]), lses[2])
    es = [jnp.exp(ls - mx) for ls in lses]
    den = es[0] + es[1] + es[2]
    for g, o_ref in enumerate((o0_ref, o1_ref, o2_ref)):
        yb = (es[g] / den) * o_ref[0]
        mix_ref[:, cw + g * LANES:cw + (g + 1) * LANES] = yb.astype(_BF16)
    mix_ref[:, cw + 3 * LANES:] = yc_ref[0]

    out_ref[0] = x_ref[0] + mod[2:3] * jnp.dot(mix_ref[...], wo_ref[...],
                                               preferred_element_type=_F32)

    h = _modulated_norm(out_ref[0], g2_ref[...], mod[3:4], mod[4:5]).astype(_BF16)
    for c0, c1 in ff_chunks:
        gate = jnp.dot(h, wg_ref[:, c0:c1], preferred_element_type=_F32)
        up = jnp.dot(h, wu_ref[:, c0:c1], preferred_element_type=_F32)
        act = ((gate * (1.0 / (1.0 + jnp.exp(-gate)))) * up).astype(_BF16)
        out_ref[0] += mod[5:6] * jnp.dot(act, wd_ref[c0:c1, :], preferred_element_type=_F32)
    if final:
        y = out_ref[0]
        ms = jnp.mean(y * y, axis=-1, keepdims=True)
        out_ref[0] = (y * lax.rsqrt(ms + EPS)) * fg_ref[...]


def _ff_chunks(dff, n_chunks=4):
    mxu_k = 2 * LANES
    tiles = -(-dff // mxu_k)
    cuts = sorted({min(dff, -(-tiles * c // n_chunks) * mxu_k) for c in range(n_chunks + 1)})
    return tuple(zip(cuts[:-1], cuts[1:]))


def _token_mixers_out(x, mod, l, ya, outs, lses, yc, w_out_p,
                      g2, wg, wu, wd, final_g, final, tm):
    b, s, d = x.shape
    dff = wg.shape[-1]
    cw = 2 * LANES
    tok = lambda width: pl.BlockSpec((1, tm, width), lambda bi, i: (bi, i, 0))
    const = lambda shape: pl.BlockSpec((None,) + shape, lambda bi, i: (l,) + (0,) * len(shape),
                                       pipeline_mode=pl.Buffered(1))
    return pl.pallas_call(
        functools.partial(_token_kernel, final=final, ff_chunks=_ff_chunks(dff)),
        out_shape=jax.ShapeDtypeStruct((b, s, d), _F32),
        grid=(b, s // tm),
        in_specs=[
            tok(d),
            pl.BlockSpec((1, 1, N_MOD, d), lambda bi, i: (l, bi, 0, 0)),
            tok(cw),
            tok(LANES), tok(LANES), tok(LANES),
            tok(LANES), tok(LANES), tok(LANES),
            tok(WIN_GROUP * LANES),
            const((d, d)),
            const((1, d)),
            const((d, dff)), const((d, dff)), const((dff, d)),
            pl.BlockSpec((1, d), lambda bi, i: (0, 0)),
        ],
        out_specs=tok(d),
        scratch_shapes=[pltpu.VMEM((tm, d), _BF16)],
        compiler_params=_cparams(("arbitrary", "arbitrary")),
        name="token_mix_ffn",
    )(x, mod, ya, *outs, *lses, yc, w_out_p, g2, wg, wu, wd, final_g)


def _in_proj_columns(d_model):
    pool_w = d_model // 4
    dil_w = 2 * len(DIL_PAIRS) * HEAD_DIM
    o_qb = pool_w
    o_kb = o_qb + dil_w
    o_vb = o_kb + dil_w
    o_qc = o_vb + dil_w
    o_kc = o_qc + 2 * WIN_GROUP * HEAD_DIM
    o_vc = o_kc + 2 * HEAD_DIM
    cols = list(range(pool_w))
    for g in range(len(DIL_PAIRS)):
        for base in (o_qb, o_kb, o_vb):
            cols += range(base + g * LANES, base + (g + 1) * LANES)
    win_heads = []
    for t in range(WIN_GROUP):
        for j in range(2):
            win_heads.append(j * WIN_GROUP + t)
            cols += range(o_qc + (j * WIN_GROUP + t) * HEAD_DIM,
                          o_qc + (j * WIN_GROUP + t + 1) * HEAD_DIM)
    cols += range(o_kc, o_kc + LANES)
    cols += range(o_vc, o_vc + LANES)
    qscale = np.ones((len(cols),), np.float32)
    for g in range(len(DIL_PAIRS)):
        qscale[pool_w + 3 * g * LANES:pool_w + (3 * g + 1) * LANES] = HEAD_DIM ** -0.5 * LOG2E
    q0 = pool_w + 3 * len(DIL_PAIRS) * LANES
    qscale[q0:q0 + WIN_GROUP * LANES] = HEAD_DIM ** -0.5 * LOG2E
    return np.asarray(cols), qscale, win_heads


def _runs(idx, key=None):
    idx = np.asarray(idx)
    key = np.zeros(len(idx)) if key is None else np.asarray(key)
    cuts = ([0] + [i for i in range(1, len(idx))
                   if idx[i] != idx[i - 1] + 1 or key[i] != key[i - 1]] + [len(idx)])
    return [(int(idx[a]), b - a, a) for a, b in zip(cuts[:-1], cuts[1:])]


def _take_runs(w, idx, axis):
    parts = [lax.slice_in_dim(w, src, src + n, axis=axis) for src, n, _ in _runs(idx)]
    return jnp.concatenate(parts, axis=axis)


def _permute_cols_kernel(w_ref, o_ref, *, runs):
    for src, n, dst, scale in runs:
        o_ref[0, :, dst:dst + n] = (w_ref[0, :, src:src + n] * scale).astype(o_ref.dtype)


def _permute_cols_bf16(w, cols, scale, tr):
    depth, rows, n = w.shape
    runs = [(src, length, dst, float(scale[dst])) for src, length, dst in _runs(cols, scale)]
    spec = pl.BlockSpec((1, tr, n), lambda l, i: (l, i, 0))
    return pl.pallas_call(
        functools.partial(_permute_cols_kernel, runs=tuple(runs)),
        out_shape=jax.ShapeDtypeStruct(w.shape, _BF16),
        grid=(depth, rows // tr),
        in_specs=[spec],
        out_specs=spec,
        compiler_params=_cparams(("arbitrary", "arbitrary")),
        name="permute_cols",
    )(w)


def kernel(x, c, norm1_g, norm2_g, w_ada, b_ada, w_in, w_pool, pool_scale, sink_logit,
           w_out, w_gate, w_up, w_down, final_g):
    b, s, d = x.shape
    depth = w_in.shape[0]
    slopes = _alibi_slopes(2 * WIN_GROUP + 2 * len(DIL_PAIRS))
    slopes_win = tuple(float(v) for v in slopes[:2 * WIN_GROUP])
    slopes_dil = slopes[2 * WIN_GROUP:]
    cols, qscale, win_heads = _in_proj_columns(d)
    pool_w = d // 4
    dil_w = 2 * len(DIL_PAIRS) * HEAD_DIM
    mix_rows = np.concatenate([
        np.arange(pool_w + dil_w),
        np.concatenate([pool_w + dil_w + h * HEAD_DIM + np.arange(HEAD_DIM) for h in win_heads]),
    ])
    n_grp = len(POOL_WINDOWS)
    pg = pool_w // n_grp

    mod = _modulation(c, w_ada, b_ada).reshape(depth, b, N_MOD, d)
    w_in_p = _permute_cols_bf16(w_in, cols, qscale, 256)
    w_out_p = _take_runs(w_out, mix_rows, 1).astype(_BF16)
    zpad = lambda n: jnp.zeros((depth, pg, n * pg), _F32)
    wpool_bd = jnp.concatenate(
        [jnp.concatenate([zpad(g), w_pool[:, g], zpad(n_grp - 1 - g)], axis=2)
         for g in range(n_grp)], axis=1).astype(_BF16)
    g1, g2 = norm1_g.reshape(depth, 1, d), norm2_g.reshape(depth, 1, d)
    ps = pool_scale.reshape(depth, 1, pool_w)
    tm = 512
    for l in range(depth):
        (ya, q0, k0, v0, q1, k1, v1, q2, k2, v2, qc, kc, vc), cast = _in_proj(
            x, mod, l, g1, w_in_p, wpool_bd, ps, 2 * tm,
            to_cast=(w_gate, w_up, w_down) if l == 0 else ())
        if l == 0:
            wg, wu, wd = cast
        outs, lses = [], []
        for g, ((_, dil), (q, k, v)) in enumerate(zip(
                DIL_PAIRS, ((q0, k0, v0), (q1, k1, v1), (q2, k2, v2)))):
            sl = tuple(float(v_) for v_ in slopes_dil[2 * g:2 * g + 2])
            o, ls = _band_attn(q.reshape(b, s, LANES), k.reshape(b, s, LANES),
                               v.reshape(b, s, LANES), dil, sl)
            outs.append(o)
            lses.append(ls)
        yc = _win_attn(sink_logit, l, qc, kc, vc, slopes_win, s)
        x = _token_mixers_out(x, mod, l, ya, outs, lses, yc, w_out_p, g2, wg, wu, wd,
                              final_g.reshape(1, d), l == depth - 1, 2 * tm)
    return x
```

```python
import functools
import math

import jax
import jax.numpy as jnp
import numpy as np
from jax import lax
from jax.experimental import pallas as pl
from jax.experimental.pallas import tpu as pltpu

LANES = 128
HEAD_DIM = 64
HALF = LANES // 2
POOL_WINDOWS = (2, 4, 8, 16)
POOL_HALO = 8
REGROUP_STRIDE = 4
POOL_PAD = 16
DIL_PAIRS = ((128, 1), (512, 4), (2048, 16))
DIL_RADIUS = 64
WIN_RADIUS = 128
WIN_GROUP = 3
N_MOD = 6
EPS = 1e-6
NEG = -1e30
QBLK = 128
ONES_ROWS = 16
LOG2E = math.log2(math.e)
BAND_UNROLL = 6
VMEM_LIMIT = 56 * 1024 * 1024

_F32 = jnp.float32
_BF16 = jnp.bfloat16


def _alibi_slopes(n):
    i = np.arange(1, n + 1, dtype=np.float32)
    return np.exp2(np.float32(-8.0) * i / np.float32(n)).astype(np.float32)


def _cparams(sem):
    return pltpu.CompilerParams(dimension_semantics=sem, vmem_limit_bytes=VMEM_LIMIT)


def _mod_kernel(c_ref, w_ref, b_ref, o_ref):
    c = c_ref[...]
    act = (c * (1.0 / (1.0 + jnp.exp(-c)))).astype(_BF16)
    w = w_ref[0].astype(_BF16)
    o_ref[0] = jnp.dot(act, w, preferred_element_type=_F32) + b_ref[0]


def _modulation(c, w_ada, b_ada):
    depth, d, nd = w_ada.shape
    b = c.shape[0]
    tn = nd // 2
    return pl.pallas_call(
        _mod_kernel,
        out_shape=jax.ShapeDtypeStruct((depth, b, nd), _F32),
        grid=(depth, nd // tn),
        in_specs=[
            pl.BlockSpec((b, d), lambda l, j: (0, 0)),
            pl.BlockSpec((1, d, tn), lambda l, j: (l, 0, j)),
            pl.BlockSpec((1, 1, tn), lambda l, j: (l, 0, j)),
        ],
        out_specs=pl.BlockSpec((1, b, tn), lambda l, j: (l, 0, j)),
        compiler_params=_cparams(("arbitrary", "arbitrary")),
        name="modulation",
    )(c, w_ada, b_ada.reshape(depth, 1, nd))


def _modulated_norm(x, g, shift, scale):
    ms = jnp.mean(x * x, axis=-1, keepdims=True)
    return (x * lax.rsqrt(ms + EPS)) * (g * (1.0 + scale)) + shift


def _pooled_tokens(ubuf_ref, t0, s_len):
    tm = ubuf_ref.shape[0] - 2 * POOL_HALO - POOL_PAD
    cols = []
    for c in range(2):
        r_lo, r_hi = POOL_WINDOWS[2 * c] // 2, POOL_WINDOWS[2 * c + 1] // 2
        lanes = slice(c * LANES, (c + 1) * LANES)
        rows_from = lambda k0, cnt: ubuf_ref[k0:k0 + cnt, lanes]
        lane = lax.broadcasted_iota(jnp.int32, (1, LANES), 1)
        radius = jnp.where(lane < HALF, r_lo, r_hi)
        u = rows_from(POOL_HALO, tm)
        if r_lo < 2:
            wsum = u
            for k in range(1, r_hi + 1):
                pair = rows_from(POOL_HALO - k, tm) + rows_from(POOL_HALO + k, tm)
                wsum = wsum + (pair if k <= r_lo else jnp.where(lane < HALF, 0.0, pair))
        else:
            rows = tm + 3 * POOL_HALO
            level, span, wsums = rows_from(0, rows) + rows_from(1, rows), 2, {}
            while True:
                if span in (r_lo, r_hi):
                    wsums[span] = (level[POOL_HALO - span:POOL_HALO - span + tm]
                                   + level[POOL_HALO:POOL_HALO + tm]
                                   + rows_from(POOL_HALO + span, tm))
                if span >= r_hi:
                    break
                rows -= POOL_HALO
                level = level[:rows] + level[span:span + rows]
                span *= 2
            wsum = jnp.where(lane < HALF, wsums[r_lo], wsums[r_hi])
        t = t0 + lax.broadcasted_iota(jnp.int32, (tm, LANES), 0)
        cnt = jnp.minimum(t + radius + 1, s_len) - jnp.maximum(t - radius, 0)
        cols.append((wsum / cnt.astype(_F32) - u).astype(_BF16))
    return jnp.concatenate(cols, axis=1)


def _in_proj_kernel(*refs, s_len, n_cast):
    for src, dst in zip(refs[:n_cast], refs[len(refs) - 3 - n_cast:len(refs) - 3]):
        dst[...] = src[...].astype(dst.dtype)
    (x_ref, xp_ref, xn_ref, mod_ref, g_ref, w_ref, wp_ref, ps_ref,
     ya_ref, q0_ref, k0_ref, v0_ref, q1_ref, k1_ref, v1_ref,
     q2_ref, k2_ref, v2_ref, qc_ref, kc_ref, vct_ref) = refs[n_cast:len(refs) - 3 - n_cast]
    zs_ref, zt_ref, ubuf_ref = refs[-3:]
    tm = x_ref.shape[1]
    i = pl.program_id(1)
    cw = 2 * LANES
    mod = mod_ref[0, 0]
    norm = lambda rows: _modulated_norm(rows, g_ref[...], mod[0:1], mod[1:2]).astype(_BF16)
    h = norm(x_ref[0])

    def proj(lhs, t0, t1):
        return jnp.dot(lhs, w_ref[:, t0 * LANES:t1 * LANES], preferred_element_type=_F32)

    halo = proj(norm(jnp.concatenate([xp_ref[0], xn_ref[0]], axis=0)), 0, 2)
    ubuf_ref[0:POOL_HALO] = jnp.where(i > 0, halo[:POOL_HALO], 0.0)
    ubuf_ref[POOL_HALO:POOL_HALO + tm] = proj(h, 0, 2)
    ubuf_ref[POOL_HALO + tm:2 * POOL_HALO + tm] = jnp.where(i < pl.num_programs(1) - 1,
                                                             halo[POOL_HALO:], 0.0)
    ubuf_ref[2 * POOL_HALO + tm:] = jnp.zeros((POOL_PAD, cw), _F32)
    pooled = _pooled_tokens(ubuf_ref, i * tm, s_len)

    z_lo, z_hi = proj(h, 2, 8), proj(h, 8, 16)

    def tile(i):
        z, j = (z_lo, i - 2) if i < 8 else (z_hi, i - 8)
        return z[:, j * LANES:(j + 1) * LANES]

    for j, ref in enumerate((q0_ref, k0_ref, v0_ref)):
        ref[0] = tile(2 + j).astype(_BF16)
    for group_refs, (_, dil), base in (((q1_ref, k1_ref, v1_ref), DIL_PAIRS[1], 5),
                                       ((q2_ref, k2_ref, v2_ref), DIL_PAIRS[2], 8)):
        inner = min(dil, REGROUP_STRIDE)
        outer = dil // inner
        for j, ref in enumerate(group_refs):
            zs_ref[base - 5 + j] = tile(base + j)
            for ra in range(inner):
                part = zs_ref[base - 5 + j, pl.ds(ra, tm // inner, stride=inner), :]
                if outer == 1:
                    ref[0, ra] = part.astype(_BF16)
                else:
                    zt_ref[ra] = part
            for rb in range(outer if outer > 1 else 0):
                for ra in range(inner):
                    ref[0, ra + inner * rb] = zt_ref[
                        ra, pl.ds(rb, tm // dil, stride=outer), :].astype(_BF16)
    for t in range(WIN_GROUP):
        qc_ref[0, t] = tile(11 + t).astype(_BF16)
    kc_ref[0] = tile(14).astype(_BF16)
    vct_ref[0] = tile(15).T.astype(_BF16)
    ya = jnp.dot(pooled, wp_ref[...], preferred_element_type=_F32) * ps_ref[...]
    ya_ref[0] = ya.astype(_BF16)


def _in_proj(x, mod, l, g, w_in_p, wpool_bd, pool_scale, tm, to_cast=()):
    b, s, d = x.shape
    n_in = w_in_p.shape[-1]
    cw = 2 * LANES
    hb = tm // POOL_HALO
    nhb = s // POOL_HALO
    d1, d2 = DIL_PAIRS[1][1], DIL_PAIRS[2][1]
    nat = lambda: pl.BlockSpec((1, tm, LANES), lambda bi, i: (bi, i, 0))
    res = lambda dil: pl.BlockSpec((1, dil, tm // dil, LANES), lambda bi, i: (bi, 0, i, 0))
    full = lambda shape: pl.BlockSpec((None,) + shape, lambda bi, i: (l,) + (0,) * len(shape))
    sds = jax.ShapeDtypeStruct
    out_shape = (
        [sds((b, s, cw), _BF16)]
        + [sds((b, s, LANES), _BF16)] * 3
        + [sds((b, d1, s // d1, LANES), _BF16)] * 3
        + [sds((b, d2, s // d2, LANES), _BF16)] * 3
        + [sds((b, WIN_GROUP, s, LANES), _BF16)]
        + [sds((b, s, LANES), _BF16), sds((b, LANES, s), _BF16)]
    )
    out_specs = (
        [pl.BlockSpec((1, tm, 2 * LANES), lambda bi, i: (bi, i, 0))]
        + [nat() for _ in range(3)]
        + [res(d1) for _ in range(3)]
        + [res(d2) for _ in range(3)]
        + [pl.BlockSpec((1, WIN_GROUP, tm, LANES), lambda bi, i: (bi, 0, i, 0))]
        + [nat(), pl.BlockSpec((1, LANES, tm), lambda bi, i: (bi, 0, i))]
    )
    nt = s // tm
    cast_in, cast_specs = [], []
    for w in to_cast:
        w2 = w.reshape(-1, w.shape[-1])
        slab = w2.shape[0] // (b * nt)
        assert slab * b * nt == w2.shape[0] and slab % 16 == 0
        cast_in.append(w2)
        cast_specs.append(pl.BlockSpec((slab, w2.shape[1]), lambda bi, i: (bi * nt + i, 0)))
    outs = pl.pallas_call(
        functools.partial(_in_proj_kernel, s_len=s, n_cast=len(to_cast)),
        out_shape=out_shape + [sds(w2.shape, _BF16) for w2 in cast_in],
        grid=(b, nt),
        in_specs=cast_specs + [
            pl.BlockSpec((1, tm, d), lambda bi, i: (bi, i, 0)),
            pl.BlockSpec((1, POOL_HALO, d), lambda bi, i: (bi, jnp.maximum(i * hb - 1, 0), 0)),
            pl.BlockSpec((1, POOL_HALO, d),
                         lambda bi, i: (bi, jnp.minimum((i + 1) * hb, nhb - 1), 0)),
            pl.BlockSpec((1, 1, N_MOD, d), lambda bi, i: (l, bi, 0, 0)),
            full((1, d)),
            full((d, n_in)),
            full((cw, cw)),
            full((1, cw)),
        ],
        out_specs=out_specs + cast_specs,
        scratch_shapes=[pltpu.VMEM((6, tm, LANES), _F32),
                        pltpu.VMEM((REGROUP_STRIDE, tm // REGROUP_STRIDE, LANES), _F32),
                        pltpu.VMEM((tm + 2 * POOL_HALO + POOL_PAD, cw), _F32)],
        compiler_params=_cparams(("arbitrary", "arbitrary")),
        name="in_proj",
    )(*cast_in, x, x, x, mod, g, w_in_p, wpool_bd, pool_scale)
    n_main = len(out_shape)
    return outs[:n_main], [o.reshape(w.shape) for o, w in zip(outs[n_main:], to_cast)]


def _band_bias_t(shape, col_heads, slopes, dist_scale, radius, offset):
    row = lax.broadcasted_iota(jnp.int32, shape, 0)
    col = lax.broadcasted_iota(jnp.int32, shape, 1)
    dist = jnp.abs(row - offset - (col % QBLK))
    slope = jnp.full(shape, slopes[col_heads[0]] * dist_scale, _F32)
    for i, hd in enumerate(col_heads[1:], start=1):
        slope = jnp.where(col >= i * QBLK, np.float32(slopes[hd] * dist_scale), slope)
    return jnp.where(dist <= radius, (-slope * dist.astype(_F32)) * LOG2E, NEG)


def _window(i, nblk, radius, width):
    seq = nblk * QBLK
    ws = jnp.clip(i * QBLK - radius, 0, seq - width)
    variant = jnp.where(i == 0, 0, jnp.where(i == nblk - 1, 2, 1))
    return ws, variant


def _trans_b_dot(a, b):
    return lax.dot_general(a, b, (((1,), (1,)), ((), ())), preferred_element_type=_F32)


def _band_attn_kernel(q_ref, k_ref, v_ref, o_ref, lse_ref, bias_ref, *scratch, dil, slopes):
    s_len = q_ref.shape[1]
    n = s_len // dil
    nblk = n // QBLK
    width = QBLK + 2 * DIL_RADIUS
    cols = 2 * QBLK

    @pl.when(pl.program_id(0) == 0)
    def _():
        for var, off in enumerate((0, DIL_RADIUS, width - QBLK)):
            bias_ref[var] = _band_bias_t((width, cols), (0, 1), slopes, float(dil),
                                         DIL_RADIUS, off)

    lo = lax.broadcasted_iota(jnp.int32, (QBLK, LANES), 1) < HALF
    s_bufs, e_bufs, m_bufs = scratch[0:2], scratch[2:4], scratch[4:6]
    ntot = s_len // QBLK

    def locate(blk):
        r = blk // nblk
        i = blk % nblk
        ws, variant = _window(i, nblk, DIL_RADIUS, width)
        return r, i, pl.multiple_of(r * n + ws, DIL_RADIUS), variant

    def scores(blk, par):
        _, _, kstart, variant = locate(blk)
        qstart = pl.multiple_of(blk * QBLK, QBLK)
        q = q_ref[0, pl.ds(qstart, QBLK), :]
        kw = k_ref[0, pl.ds(kstart, width), :]
        zero = jnp.zeros_like(q)
        qs = jnp.concatenate([jnp.where(lo, q, zero), jnp.where(lo, zero, q)], axis=0)
        s_bufs[par][...] = _trans_b_dot(kw, qs) + bias_ref[variant]

    def numerators(par):
        sc = s_bufs[par][...]
        m = jnp.max(sc, axis=0, keepdims=True)
        e_bufs[par][...] = jnp.exp2(sc - m).astype(_BF16)
        m_bufs[par][...] = jnp.broadcast_to(m, (8, cols))

    def outputs(blk, par):
        r, i, kstart, _ = locate(blk)
        vt = v_ref[0, pl.ds(kstart, width), :].T
        lhs = jnp.concatenate([vt, jnp.ones((ONES_ROWS, width), _BF16)], axis=0)
        o = jnp.dot(lhs, e_bufs[par][...], preferred_element_type=_F32)
        den = o[LANES:LANES + 1]
        lse_row = m_bufs[par][0:1] * (1.0 / LOG2E) + jnp.log(den)
        res = jnp.concatenate(
            [o[h * HEAD_DIM:(h + 1) * HEAD_DIM, h * QBLK:(h + 1) * QBLK]
             / den[:, h * QBLK:(h + 1) * QBLK] for h in range(2)], axis=0)
        lse2 = jnp.concatenate(
            [jnp.broadcast_to(lse_row[:, h * QBLK:(h + 1) * QBLK], (HEAD_DIM, QBLK))
             for h in range(2)], axis=0)
        out, lse = res.T, lse2.T
        if dil == 1:
            qstart = pl.multiple_of(blk * QBLK, QBLK)
            o_ref[0, pl.ds(qstart, QBLK), :] = out.astype(_BF16)
            lse_ref[0, pl.ds(qstart, QBLK), :] = lse
        else:
            tstart = i * (QBLK * dil) + r
            scratch[6][pl.ds(tstart, QBLK, stride=dil), :] = out
            lse_ref[0, pl.ds(tstart, QBLK, stride=dil), :] = lse

    scores(0, 0)
    scores(1, 1)
    numerators(0)

    def body(p, carry):
        it = 2 * p
        scores(it, 0)
        numerators(1)
        outputs(it - 2, 0)
        scores(it + 1, 1)
        numerators(0)
        outputs(it - 1, 1)
        return carry

    lax.fori_loop(1, ntot // 2, body, 0, unroll=BAND_UNROLL)
    numerators(1)
    outputs(ntot - 2, 0)
    outputs(ntot - 1, 1)
    if dil != 1:
        o_ref[0] = scratch[6][...].astype(_BF16)


def _band_attn(q, k, v, dil, slopes):
    b, s, _ = q.shape
    width = QBLK + 2 * DIL_RADIUS
    spec = lambda: pl.BlockSpec((1, s, LANES), lambda bi: (bi, 0, 0))
    cols = 2 * QBLK
    scratch = ([pltpu.VMEM((3, width, cols), _F32)]
               + [pltpu.VMEM((width, cols), _F32)] * 2
               + [pltpu.VMEM((width, cols), _BF16)] * 2
               + [pltpu.VMEM((8, cols), _F32)] * 2)
    if dil != 1:
        scratch.append(pltpu.VMEM((s, LANES), _F32))
    return pl.pallas_call(
        functools.partial(_band_attn_kernel, dil=dil, slopes=slopes),
        out_shape=[jax.ShapeDtypeStruct((b, s, LANES), _BF16),
                   jax.ShapeDtypeStruct((b, s, LANES), _F32)],
        grid=(b,),
        in_specs=[spec(), spec(), spec()],
        out_specs=[spec(), spec()],
        scratch_shapes=scratch,
        compiler_params=_cparams(("arbitrary",)),
        name=f"band_attn_d{dil}",
    )(q, k, v)


def _win_attn_kernel(sink_ref, q_ref, k_ref, vt_ref, o_ref, bias_ref,
                     s0_ref, s1_ref, e0_ref, e1_ref, t0_ref, t1_ref, *, slopes, layer):
    s_len = k_ref.shape[1]
    tc = q_ref.shape[2]
    nblk = s_len // QBLK
    nloc = tc // QBLK
    width = QBLK + 2 * WIN_RADIUS
    cols = WIN_GROUP * QBLK
    s_bufs, e_bufs, t_bufs = (s0_ref, s1_ref), (e0_ref, e1_ref), (t0_ref, t1_ref)

    @pl.when((pl.program_id(0) == 0) & (pl.program_id(1) == 0))
    def _():
        for var, off in enumerate((0, WIN_RADIUS, width - QBLK)):
            for j in range(2):
                heads = tuple(j * WIN_GROUP + t for t in range(WIN_GROUP))
                bias_ref[var, j] = _band_bias_t((width, cols), heads, slopes, 1.0,
                                                WIN_RADIUS, off)

    qlane = lax.broadcasted_iota(jnp.int32, (1, cols), 1)
    klo = lax.broadcasted_iota(jnp.int32, (width, LANES), 1) < HALF
    base = pl.program_id(1) * nloc

    def sink_row(j):
        row = jnp.full((1, cols), sink_ref[layer, j * WIN_GROUP] * LOG2E, _F32)
        for t in range(1, WIN_GROUP):
            row = jnp.where(qlane >= t * QBLK, sink_ref[layer, j * WIN_GROUP + t] * LOG2E, row)
        return row

    def kstart_of(bl):
        ws, variant = _window(base + bl, nblk, WIN_RADIUS, width)
        return pl.multiple_of(ws, QBLK), variant

    def scores(bl, par):
        kstart, variant = kstart_of(bl)
        qstart = pl.multiple_of(bl * QBLK, QBLK)
        q = q_ref[0, :, pl.ds(qstart, QBLK), :].reshape(cols, LANES)
        kw = k_ref[0, pl.ds(kstart, width), :]
        zero = jnp.zeros_like(kw)
        for j in range(2):
            kh = jnp.where(klo, kw, zero) if j == 0 else jnp.where(klo, zero, kw)
            s_bufs[par][j] = _trans_b_dot(kh, q) + bias_ref[variant, j]

    def numerators(par):
        for j in range(2):
            sc = s_bufs[par][j]
            sink = sink_row(j)
            m = jnp.maximum(jnp.max(sc, axis=0, keepdims=True), sink)
            e_bufs[par][j] = jnp.exp2(sc - m).astype(_BF16)
            t_bufs[par][j] = jnp.broadcast_to(jnp.exp2(sink - m), (8, cols))

    def outputs(bl, par):
        kstart, _ = kstart_of(bl)
        qstart = pl.multiple_of(bl * QBLK, QBLK)
        vt = vt_ref[0, :, pl.ds(kstart, width)]
        ones = jnp.ones((ONES_ROWS, width), _BF16)
        res = []
        for j in range(2):
            lhs = jnp.concatenate([vt[j * HEAD_DIM:(j + 1) * HEAD_DIM], ones], axis=0)
            o = jnp.dot(lhs, e_bufs[par][j], preferred_element_type=_F32)
            den = o[HEAD_DIM:HEAD_DIM + 1] + t_bufs[par][j][0:1]
            res.append(o[:HEAD_DIM] / den)
        for t in range(WIN_GROUP):
            both = jnp.concatenate([r[:, t * QBLK:(t + 1) * QBLK] for r in res], axis=0)
            o_ref[0, pl.ds(qstart, QBLK), t * LANES:(t + 1) * LANES] = both.T.astype(_BF16)

    scores(0, 0)
    scores(1, 1)
    numerators(0)

    def body(p, carry):
        it = 2 * p
        outputs(it - 2, 0)
        numerators(1)
        scores(it, 0)
        outputs(it - 1, 1)
        numerators(0)
        scores(it + 1, 1)
        return carry

    lax.fori_loop(1, nloc // 2, body, 0, unroll=5)
    numerators(1)
    outputs(nloc - 2, 0)
    outputs(nloc - 1, 1)


def _win_attn(sink, layer, q, k, vt, slopes, tc):
    b, _, s, _ = q.shape
    width = QBLK + 2 * WIN_RADIUS
    cols = WIN_GROUP * QBLK
    assert (tc // QBLK) % 2 == 0 and tc // QBLK >= 4
    return pl.pallas_call(
        functools.partial(_win_attn_kernel, slopes=slopes, layer=layer),
        out_shape=jax.ShapeDtypeStruct((b, s, WIN_GROUP * LANES), _BF16),
        grid=(b, s // tc),
        in_specs=[
            pl.BlockSpec(memory_space=pltpu.SMEM),
            pl.BlockSpec((1, WIN_GROUP, tc, LANES), lambda bi, i: (bi, 0, i, 0)),
            pl.BlockSpec((1, s, LANES), lambda bi, i: (bi, 0, 0)),
            pl.BlockSpec((1, LANES, s), lambda bi, i: (bi, 0, 0)),
        ],
        out_specs=pl.BlockSpec((1, tc, WIN_GROUP * LANES), lambda bi, i: (bi, i, 0)),
        scratch_shapes=[pltpu.VMEM((3, 2, width, cols), _F32)]
                       + [pltpu.VMEM((2, width, cols), _F32)] * 2
                       + [pltpu.VMEM((2, width, cols), _BF16)] * 2
                       + [pltpu.VMEM((2, 8, cols), _F32)] * 2,
        compiler_params=_cparams(("arbitrary", "arbitrary")),
        name="win_attn",
    )(sink, q, k, vt)


def _token_kernel(x_ref, mod_ref, ya_ref,
                  o0_ref, o1_ref, o2_ref, l0_ref, l1_ref, l2_ref, yc_ref, wo_ref,
                  g2_ref, wg_ref, wu_ref, wd_ref, fg_ref,
                  out_ref, mix_ref, *, final, ff_chunks):
    cw = 2 * LANES
    mod = mod_ref[0, 0]
    mix_ref[:, :cw] = ya_ref[0]

    lses = [l0_ref[0], l1_ref[0], l2_ref[0]]
    mx = jnp.maximum(jnp.maximum(lses[0], lses[1]), lses[2])
    es = [jnp.exp(ls - mx) for ls in lses]
    den = es[0] + es[1] + es[2]
    for g, o_ref in enumerate((o0_ref, o1_ref, o2_ref)):
        yb = (es[g] / den) * o_ref[0]
        mix_ref[:, cw + g * LANES:cw + (g + 1) * LANES] = yb.astype(_BF16)
    mix_ref[:, cw + 3 * LANES:] = yc_ref[0]

    out_ref[0] = x_ref[0] + mod[2:3] * jnp.dot(mix_ref[...], wo_ref[...],
                                               preferred_element_type=_F32)

    h = _modulated_norm(out_ref[0], g2_ref[...], mod[3:4], mod[4:5]).astype(_BF16)
    for c0, c1 in ff_chunks:
        gate = jnp.dot(h, wg_ref[:, c0:c1], preferred_element_type=_F32)
        up = jnp.dot(h, wu_ref[:, c0:c1], preferred_element_type=_F32)
        act = ((gate * (1.0 / (1.0 + jnp.exp(-gate)))) * up).astype(_BF16)
        out_ref[0] += mod[5:6] * jnp.dot(act, wd_ref[c0:c1, :], preferred_element_type=_F32)
    if final:
        y = out_ref[0]
        ms = jnp.mean(y * y, axis=-1, keepdims=True)
        out_ref[0] = (y * lax.rsqrt(ms + EPS)) * fg_ref[...]


def _ff_chunks(dff, n_chunks=4):
    mxu_k = 2 * LANES
    tiles = -(-dff // mxu_k)
    cuts = sorted({min(dff, -(-tiles * c // n_chunks) * mxu_k) for c in range(n_chunks + 1)})
    return tuple(zip(cuts[:-1], cuts[1:]))


def _token_mixers_out(x, mod, l, ya, outs, lses, yc, w_out_p,
                      g2, wg, wu, wd, final_g, final, tm):
    b, s, d = x.shape
    dff = wg.shape[-1]
    cw = 2 * LANES
    tok = lambda width: pl.BlockSpec((1, tm, width), lambda bi, i: (bi, i, 0))
    const = lambda shape: pl.BlockSpec((None,) + shape, lambda bi, i: (l,) + (0,) * len(shape),
                                       pipeline_mode=pl.Buffered(1))
    return pl.pallas_call(
        functools.partial(_token_kernel, final=final, ff_chunks=_ff_chunks(dff)),
        out_shape=jax.ShapeDtypeStruct((b, s, d), _F32),
        grid=(b, s // tm),
        in_specs=[
            tok(d),
            pl.BlockSpec((1, 1, N_MOD, d), lambda bi, i: (l, bi, 0, 0)),
            tok(cw),
            tok(LANES), tok(LANES), tok(LANES),
            tok(LANES), tok(LANES), tok(LANES),
            tok(WIN_GROUP * LANES),
            const((d, d)),
            const((1, d)),
            const((d, dff)), const((d, dff)), const((dff, d)),
            pl.BlockSpec((1, d), lambda bi, i: (0, 0)),
        ],
        out_specs=tok(d),
        scratch_shapes=[pltpu.VMEM((tm, d), _BF16)],
        compiler_params=_cparams(("arbitrary", "arbitrary")),
        name="token_mix_ffn",
    )(x, mod, ya, *outs, *lses, yc, w_out_p, g2, wg, wu, wd, final_g)


def _in_proj_columns(d_model):
    pool_w = d_model // 4
    dil_w = 2 * len(DIL_PAIRS) * HEAD_DIM
    o_qb = pool_w
    o_kb = o_qb + dil_w
    o_vb = o_kb + dil_w
    o_qc = o_vb + dil_w
    o_kc = o_qc + 2 * WIN_GROUP * HEAD_DIM
    o_vc = o_kc + 2 * HEAD_DIM
    cols = list(range(pool_w))
    for g in range(len(DIL_PAIRS)):
        for base in (o_qb, o_kb, o_vb):
            cols += range(base + g * LANES, base + (g + 1) * LANES)
    win_heads = []
    for t in range(WIN_GROUP):
        for j in range(2):
            win_heads.append(j * WIN_GROUP + t)
            cols += range(o_qc + (j * WIN_GROUP + t) * HEAD_DIM,
                          o_qc + (j * WIN_GROUP + t + 1) * HEAD_DIM)
    cols += range(o_kc, o_kc + LANES)
    cols += range(o_vc, o_vc + LANES)
    qscale = np.ones((len(cols),), np.float32)
    for g in range(len(DIL_PAIRS)):
        qscale[pool_w + 3 * g * LANES:pool_w + (3 * g + 1) * LANES] = HEAD_DIM ** -0.5 * LOG2E
    q0 = pool_w + 3 * len(DIL_PAIRS) * LANES
    qscale[q0:q0 + WIN_GROUP * LANES] = HEAD_DIM ** -0.5 * LOG2E
    return np.asarray(cols), qscale, win_heads


def _runs(idx, key=None):
    idx = np.asarray(idx)
    key = np.zeros(len(idx)) if key is None else np.asarray(key)
    cuts = ([0] + [i for i in range(1, len(idx))
                   if idx[i] != idx[i - 1] + 1 or key[i] != key[i - 1]] + [len(idx)])
    return [(int(idx[a]), b - a, a) for a, b in zip(cuts[:-1], cuts[1:])]


def _take_runs(w, idx, axis):
    parts = [lax.slice_in_dim(w, src, src + n, axis=axis) for src, n, _ in _runs(idx)]
    return jnp.concatenate(parts, axis=axis)


def _permute_cols_kernel(w_ref, o_ref, *, runs):
    for src, n, dst, scale in runs:
        o_ref[0, :, dst:dst + n] = (w_ref[0, :, src:src + n] * scale).astype(o_ref.dtype)


def _permute_cols_bf16(w, cols, scale, tr):
    depth, rows, n = w.shape
    runs = [(src, length, dst, float(scale[dst])) for src, length, dst in _runs(cols, scale)]
    spec = pl.BlockSpec((1, tr, n), lambda l, i: (l, i, 0))
    return pl.pallas_call(
        functools.partial(_permute_cols_kernel, runs=tuple(runs)),
        out_shape=jax.ShapeDtypeStruct(w.shape, _BF16),
        grid=(depth, rows // tr),
        in_specs=[spec],
        out_specs=spec,
        compiler_params=_cparams(("arbitrary", "arbitrary")),
        name="permute_cols",
    )(w)


def kernel(x, c, norm1_g, norm2_g, w_ada, b_ada, w_in, w_pool, pool_scale, sink_logit,
           w_out, w_gate, w_up, w_down, final_g):
    b, s, d = x.shape
    depth = w_in.shape[0]
    slopes = _alibi_slopes(2 * WIN_GROUP + 2 * len(DIL_PAIRS))
    slopes_win = tuple(float(v) for v in slopes[:2 * WIN_GROUP])
    slopes_dil = slopes[2 * WIN_GROUP:]
    cols, qscale, win_heads = _in_proj_columns(d)
    pool_w = d // 4
    dil_w = 2 * len(DIL_PAIRS) * HEAD_DIM
    mix_rows = np.concatenate([
        np.arange(pool_w + dil_w),
        np.concatenate([pool_w + dil_w + h * HEAD_DIM + np.arange(HEAD_DIM) for h in win_heads]),
    ])
    n_grp = len(POOL_WINDOWS)
    pg = pool_w // n_grp

    mod = _modulation(c, w_ada, b_ada).reshape(depth, b, N_MOD, d)
    w_in_p = _permute_cols_bf16(w_in, cols, qscale, 256)
    w_out_p = _take_runs(w_out, mix_rows, 1).astype(_BF16)
    zpad = lambda n: jnp.zeros((depth, pg, n * pg), _F32)
    wpool_bd = jnp.concatenate(
        [jnp.concatenate([zpad(g), w_pool[:, g], zpad(n_grp - 1 - g)], axis=2)
         for g in range(n_grp)], axis=1).astype(_BF16)
    g1, g2 = norm1_g.reshape(depth, 1, d), norm2_g.reshape(depth, 1, d)
    ps = pool_scale.reshape(depth, 1, pool_w)
    tm = 512
    for l in range(depth):
        (ya, q0, k0, v0, q1, k1, v1, q2, k2, v2, qc, kc, vc), cast = _in_proj(
            x, mod, l, g1, w_in_p, wpool_bd, ps, 2 * tm,
            to_cast=(w_gate, w_up, w_down) if l == 0 else ())
        if l == 0:
            wg, wu, wd = cast
        outs, lses = [], []
        for g, ((_, dil), (q, k, v)) in enumerate(zip(
                DIL_PAIRS, ((q0, k0, v0), (q1, k1, v1), (q2, k2, v2)))):
            sl = tuple(float(v_) for v_ in slopes_dil[2 * g:2 * g + 2])
            o, ls = _band_attn(q.reshape(b, s, LANES), k.reshape(b, s, LANES),
                               v.reshape(b, s, LANES), dil, sl)
            outs.append(o)
            lses.append(ls)
        yc = _win_attn(sink_logit, l, qc, kc, vc, slopes_win, s)
        x = _token_mixers_out(x, mod, l, ya, outs, lses, yc, w_out_p, g2, wg, wu, wd,
                              final_g.reshape(1, d), l == depth - 1, 2 * tm)
    return x
```

```python
import functools
import math

import jax
import jax.numpy as jnp
import numpy as np
from jax import lax
from jax.experimental import pallas as pl
from jax.experimental.pallas import tpu as pltpu

LANES = 128
HEAD_DIM = 64
HALF = LANES // 2
POOL_WINDOWS = (2, 4, 8, 16)
POOL_HALO = 8
REGROUP_STRIDE = 4
POOL_PAD = 16
DIL_PAIRS = ((128, 1), (512, 4), (2048, 16))
DIL_RADIUS = 64
WIN_RADIUS = 128
WIN_GROUP = 3
N_MOD = 6
EPS = 1e-6
NEG = -1e30
QBLK = 128
ONES_ROWS = 16
LOG2E = math.log2(math.e)
BAND_UNROLL = 6
WIN_UNROLL = 3
TOKEN_TILE = 1024
WEIGHT_ROWS = 256
VMEM_LIMIT = 56 * 1024 * 1024

_F32 = jnp.float32
_BF16 = jnp.bfloat16


def _alibi_slopes(n):
    i = np.arange(1, n + 1, dtype=np.float32)
    return np.exp2(np.float32(-8.0) * i / np.float32(n)).astype(np.float32)


def _cparams(sem):
    return pltpu.CompilerParams(dimension_semantics=sem, vmem_limit_bytes=VMEM_LIMIT)


def _mod_kernel(c_ref, w_ref, b_ref, o_ref):
    c = c_ref[...]
    act = (c * (1.0 / (1.0 + jnp.exp(-c)))).astype(_BF16)
    w = w_ref[0].astype(_BF16)
    o_ref[0] = jnp.dot(act, w, preferred_element_type=_F32) + b_ref[0]


def _modulation(c, w_ada, b_ada):
    depth, d, nd = w_ada.shape
    b = c.shape[0]
    tn = nd // 2
    return pl.pallas_call(
        _mod_kernel,
        out_shape=jax.ShapeDtypeStruct((depth, b, nd), _F32),
        grid=(depth, nd // tn),
        in_specs=[
            pl.BlockSpec((b, d), lambda l, j: (0, 0)),
            pl.BlockSpec((1, d, tn), lambda l, j: (l, 0, j)),
            pl.BlockSpec((1, 1, tn), lambda l, j: (l, 0, j)),
        ],
        out_specs=pl.BlockSpec((1, b, tn), lambda l, j: (l, 0, j)),
        compiler_params=_cparams(("arbitrary", "arbitrary")),
        name="modulation",
    )(c, w_ada, b_ada.reshape(depth, 1, nd))


def _modulated_norm(x, g, shift, scale):
    ms = jnp.mean(x * x, axis=-1, keepdims=True)
    return (x * lax.rsqrt(ms + EPS)) * (g * (1.0 + scale)) + shift


def _pooled_tokens(ubuf_ref, t0, s_len):
    tm = ubuf_ref.shape[0] - 2 * POOL_HALO - POOL_PAD
    cols = []
    for c in range(2):
        r_lo, r_hi = POOL_WINDOWS[2 * c] // 2, POOL_WINDOWS[2 * c + 1] // 2
        lanes = slice(c * LANES, (c + 1) * LANES)
        rows_from = lambda k0, cnt: ubuf_ref[k0:k0 + cnt, lanes]
        lane = lax.broadcasted_iota(jnp.int32, (1, LANES), 1)
        radius = jnp.where(lane < HALF, r_lo, r_hi)
        u = rows_from(POOL_HALO, tm)
        if r_lo < 2:
            wsum = u
            for k in range(1, r_hi + 1):
                pair = rows_from(POOL_HALO - k, tm) + rows_from(POOL_HALO + k, tm)
                wsum = wsum + (pair if k <= r_lo else jnp.where(lane < HALF, 0.0, pair))
        else:
            rows = tm + 3 * POOL_HALO
            level, span, wsums = rows_from(0, rows) + rows_from(1, rows), 2, {}
            while True:
                if span in (r_lo, r_hi):
                    wsums[span] = (level[POOL_HALO - span:POOL_HALO - span + tm]
                                   + level[POOL_HALO:POOL_HALO + tm]
                                   + rows_from(POOL_HALO + span, tm))
                if span >= r_hi:
                    break
                rows -= POOL_HALO
                level = level[:rows] + level[span:span + rows]
                span *= 2
            wsum = jnp.where(lane < HALF, wsums[r_lo], wsums[r_hi])
        t = t0 + lax.broadcasted_iota(jnp.int32, (tm, LANES), 0)
        cnt = jnp.minimum(t + radius + 1, s_len) - jnp.maximum(t - radius, 0)
        cols.append((wsum / cnt.astype(_F32) - u).astype(_BF16))
    return jnp.concatenate(cols, axis=1)


def _in_proj_kernel(*refs, s_len, n_cast):
    for src, dst in zip(refs[:n_cast], refs[len(refs) - 3 - n_cast:len(refs) - 3]):
        dst[...] = src[...].astype(dst.dtype)
    (x_ref, xp_ref, xn_ref, mod_ref, g_ref, w_ref, wp_ref, ps_ref,
     ya_ref, q0_ref, k0_ref, v0_ref, q1_ref, k1_ref, v1_ref,
     q2_ref, k2_ref, v2_ref, qc_ref, kc_ref, vct_ref) = refs[n_cast:len(refs) - 3 - n_cast]
    zs_ref, zt_ref, ubuf_ref = refs[-3:]
    tm = x_ref.shape[1]
    i = pl.program_id(1)
    cw = 2 * LANES
    mod = mod_ref[0, 0]
    norm = lambda rows: _modulated_norm(rows, g_ref[...], mod[0:1], mod[1:2]).astype(_BF16)
    h = norm(x_ref[0])

    def proj(lhs, t0, t1):
        return jnp.dot(lhs, w_ref[:, t0 * LANES:t1 * LANES], preferred_element_type=_F32)

    halo = proj(norm(jnp.concatenate([xp_ref[0], xn_ref[0]], axis=0)), 0, 2)
    ubuf_ref[0:POOL_HALO] = jnp.where(i > 0, halo[:POOL_HALO], 0.0)
    ubuf_ref[POOL_HALO:POOL_HALO + tm] = proj(h, 0, 2)
    ubuf_ref[POOL_HALO + tm:2 * POOL_HALO + tm] = jnp.where(i < pl.num_programs(1) - 1,
                                                             halo[POOL_HALO:], 0.0)
    ubuf_ref[2 * POOL_HALO + tm:] = jnp.zeros((POOL_PAD, cw), _F32)
    pooled = _pooled_tokens(ubuf_ref, i * tm, s_len)

    z_lo, z_hi = proj(h, 2, 8), proj(h, 8, 16)

    def tile(i):
        z, j = (z_lo, i - 2) if i < 8 else (z_hi, i - 8)
        return z[:, j * LANES:(j + 1) * LANES]

    for j, ref in enumerate((q0_ref, k0_ref, v0_ref)):
        ref[0] = tile(2 + j).astype(_BF16)
    for group_refs, (_, dil), base in (((q1_ref, k1_ref, v1_ref), DIL_PAIRS[1], 5),
                                       ((q2_ref, k2_ref, v2_ref), DIL_PAIRS[2], 8)):
        inner = min(dil, REGROUP_STRIDE)
        outer = dil // inner
        for j, ref in enumerate(group_refs):
            zs_ref[base - 5 + j] = tile(base + j)
            for ra in range(inner):
                part = zs_ref[base - 5 + j, pl.ds(ra, tm // inner, stride=inner), :]
                if outer == 1:
                    ref[0, ra] = part.astype(_BF16)
                else:
                    zt_ref[ra] = part
            for rb in range(outer if outer > 1 else 0):
                for ra in range(inner):
                    ref[0, ra + inner * rb] = zt_ref[
                        ra, pl.ds(rb, tm // dil, stride=outer), :].astype(_BF16)
    for t in range(WIN_GROUP):
        qc_ref[0, t] = tile(11 + t).astype(_BF16)
    kc_ref[0] = tile(14).astype(_BF16)
    vct_ref[0] = tile(15).T.astype(_BF16)
    ya = jnp.dot(pooled, wp_ref[...], preferred_element_type=_F32) * ps_ref[...]
    ya_ref[0] = ya.astype(_BF16)


def _in_proj(x, mod, l, g, w_in_p, wpool_bd, pool_scale, tm, to_cast=()):
    b, s, d = x.shape
    n_in = w_in_p.shape[-1]
    cw = 2 * LANES
    hb = tm // POOL_HALO
    nhb = s // POOL_HALO
    d1, d2 = DIL_PAIRS[1][1], DIL_PAIRS[2][1]
    nat = lambda: pl.BlockSpec((1, tm, LANES), lambda bi, i: (bi, i, 0))
    res = lambda dil: pl.BlockSpec((1, dil, tm // dil, LANES), lambda bi, i: (bi, 0, i, 0))
    full = lambda shape: pl.BlockSpec((None,) + shape, lambda bi, i: (l,) + (0,) * len(shape))
    sds = jax.ShapeDtypeStruct
    out_shape = (
        [sds((b, s, cw), _BF16)]
        + [sds((b, s, LANES), _BF16)] * 3
        + [sds((b, d1, s // d1, LANES), _BF16)] * 3
        + [sds((b, d2, s // d2, LANES), _BF16)] * 3
        + [sds((b, WIN_GROUP, s, LANES), _BF16)]
        + [sds((b, s, LANES), _BF16), sds((b, LANES, s), _BF16)]
    )
    out_specs = (
        [pl.BlockSpec((1, tm, 2 * LANES), lambda bi, i: (bi, i, 0))]
        + [nat() for _ in range(3)]
        + [res(d1) for _ in range(3)]
        + [res(d2) for _ in range(3)]
        + [pl.BlockSpec((1, WIN_GROUP, tm, LANES), lambda bi, i: (bi, 0, i, 0))]
        + [nat(), pl.BlockSpec((1, LANES, tm), lambda bi, i: (bi, 0, i))]
    )
    nt = s // tm
    cast_in, cast_specs = [], []
    for w in to_cast:
        w2 = w.reshape(-1, w.shape[-1])
        slab = w2.shape[0] // (b * nt)
        assert slab * b * nt == w2.shape[0] and slab % 16 == 0
        cast_in.append(w2)
        cast_specs.append(pl.BlockSpec((slab, w2.shape[1]), lambda bi, i: (bi * nt + i, 0)))
    outs = pl.pallas_call(
        functools.partial(_in_proj_kernel, s_len=s, n_cast=len(to_cast)),
        out_shape=out_shape + [sds(w2.shape, _BF16) for w2 in cast_in],
        grid=(b, nt),
        in_specs=cast_specs + [
            pl.BlockSpec((1, tm, d), lambda bi, i: (bi, i, 0)),
            pl.BlockSpec((1, POOL_HALO, d), lambda bi, i: (bi, jnp.maximum(i * hb - 1, 0), 0)),
            pl.BlockSpec((1, POOL_HALO, d),
                         lambda bi, i: (bi, jnp.minimum((i + 1) * hb, nhb - 1), 0)),
            pl.BlockSpec((1, 1, N_MOD, d), lambda bi, i: (l, bi, 0, 0)),
            full((1, d)),
            full((d, n_in)),
            full((cw, cw)),
            full((1, cw)),
        ],
        out_specs=out_specs + cast_specs,
        scratch_shapes=[pltpu.VMEM((6, tm, LANES), _F32),
                        pltpu.VMEM((REGROUP_STRIDE, tm // REGROUP_STRIDE, LANES), _F32),
                        pltpu.VMEM((tm + 2 * POOL_HALO + POOL_PAD, cw), _F32)],
        compiler_params=_cparams(("arbitrary", "arbitrary")),
        name="in_proj",
    )(*cast_in, x, x, x, mod, g, w_in_p, wpool_bd, pool_scale)
    n_main = len(out_shape)
    return outs[:n_main], [o.reshape(w.shape) for o, w in zip(outs[n_main:], to_cast)]


def _band_bias_t(shape, col_heads, slopes, dist_scale, radius, offset):
    row = lax.broadcasted_iota(jnp.int32, shape, 0)
    col = lax.broadcasted_iota(jnp.int32, shape, 1)
    dist = jnp.abs(row - offset - (col % QBLK))
    slope = jnp.full(shape, slopes[col_heads[0]] * dist_scale, _F32)
    for i, hd in enumerate(col_heads[1:], start=1):
        slope = jnp.where(col >= i * QBLK, np.float32(slopes[hd] * dist_scale), slope)
    return jnp.where(dist <= radius, (-slope * dist.astype(_F32)) * LOG2E, NEG)


def _window(i, nblk, radius, width):
    seq = nblk * QBLK
    ws = jnp.clip(i * QBLK - radius, 0, seq - width)
    variant = jnp.where(i == 0, 0, jnp.where(i == nblk - 1, 2, 1))
    return ws, variant


def _trans_b_dot(a, b):
    return lax.dot_general(a, b, (((1,), (1,)), ((), ())), preferred_element_type=_F32)


def _band_attn_kernel(q_ref, k_ref, v_ref, o_ref, lse_ref, bias_ref, *scratch, dil, slopes):
    s_len = q_ref.shape[1]
    n = s_len // dil
    nblk = n // QBLK
    width = QBLK + 2 * DIL_RADIUS
    cols = 2 * QBLK

    @pl.when(pl.program_id(0) == 0)
    def _():
        for var, off in enumerate((0, DIL_RADIUS, width - QBLK)):
            bias_ref[var] = _band_bias_t((width, cols), (0, 1), slopes, float(dil),
                                         DIL_RADIUS, off)

    lo = lax.broadcasted_iota(jnp.int32, (QBLK, LANES), 1) < HALF
    s_bufs, e_bufs, m_bufs = scratch[0:2], scratch[2:4], scratch[4:6]
    ntot = s_len // QBLK

    def locate(blk):
        r = blk // nblk
        i = blk % nblk
        ws, variant = _window(i, nblk, DIL_RADIUS, width)
        return r, i, pl.multiple_of(r * n + ws, DIL_RADIUS), variant

    def scores(blk, par):
        _, _, kstart, variant = locate(blk)
        qstart = pl.multiple_of(blk * QBLK, QBLK)
        q = q_ref[0, pl.ds(qstart, QBLK), :]
        kw = k_ref[0, pl.ds(kstart, width), :]
        zero = jnp.zeros_like(q)
        qs = jnp.concatenate([jnp.where(lo, q, zero), jnp.where(lo, zero, q)], axis=0)
        s_bufs[par][...] = _trans_b_dot(kw, qs) + bias_ref[variant]

    def numerators(par):
        sc = s_bufs[par][...]
        m = jnp.max(sc, axis=0, keepdims=True)
        e_bufs[par][...] = jnp.exp2(sc - m).astype(_BF16)
        m_bufs[par][...] = jnp.broadcast_to(m, (8, cols))

    def outputs(blk, par):
        r, i, kstart, _ = locate(blk)
        vt = v_ref[0, pl.ds(kstart, width), :].T
        lhs = jnp.concatenate([vt, jnp.ones((ONES_ROWS, width), _BF16)], axis=0)
        o = jnp.dot(lhs, e_bufs[par][...], preferred_element_type=_F32)
        den = o[LANES:LANES + 1]
        lse_row = m_bufs[par][0:1] * (1.0 / LOG2E) + jnp.log(den)
        res = jnp.concatenate(
            [o[h * HEAD_DIM:(h + 1) * HEAD_DIM, h * QBLK:(h + 1) * QBLK]
             / den[:, h * QBLK:(h + 1) * QBLK] for h in range(2)], axis=0)
        lse2 = jnp.concatenate(
            [jnp.broadcast_to(lse_row[:, h * QBLK:(h + 1) * QBLK], (HEAD_DIM, QBLK))
             for h in range(2)], axis=0)
        out, lse = res.T, lse2.T
        if dil == 1:
            qstart = pl.multiple_of(blk * QBLK, QBLK)
            o_ref[0, pl.ds(qstart, QBLK), :] = out.astype(_BF16)
            lse_ref[0, pl.ds(qstart, QBLK), :] = lse
        else:
            tstart = i * (QBLK * dil) + r
            scratch[6][pl.ds(tstart, QBLK, stride=dil), :] = out
            lse_ref[0, pl.ds(tstart, QBLK, stride=dil), :] = lse

    scores(0, 0)
    scores(1, 1)
    numerators(0)

    def body(p, carry):
        it = 2 * p
        scores(it, 0)
        numerators(1)
        outputs(it - 2, 0)
        scores(it + 1, 1)
        numerators(0)
        outputs(it - 1, 1)
        return carry

    lax.fori_loop(1, ntot // 2, body, 0, unroll=BAND_UNROLL)
    numerators(1)
    outputs(ntot - 2, 0)
    outputs(ntot - 1, 1)
    if dil != 1:
        o_ref[0] = scratch[6][...].astype(_BF16)


def _band_attn(q, k, v, dil, slopes):
    b, s, _ = q.shape
    width = QBLK + 2 * DIL_RADIUS
    spec = lambda: pl.BlockSpec((1, s, LANES), lambda bi: (bi, 0, 0))
    cols = 2 * QBLK
    scratch = ([pltpu.VMEM((3, width, cols), _F32)]
               + [pltpu.VMEM((width, cols), _F32)] * 2
               + [pltpu.VMEM((width, cols), _BF16)] * 2
               + [pltpu.VMEM((8, cols), _F32)] * 2)
    if dil != 1:
        scratch.append(pltpu.VMEM((s, LANES), _F32))
    return pl.pallas_call(
        functools.partial(_band_attn_kernel, dil=dil, slopes=slopes),
        out_shape=[jax.ShapeDtypeStruct((b, s, LANES), _BF16),
                   jax.ShapeDtypeStruct((b, s, LANES), _F32)],
        grid=(b,),
        in_specs=[spec(), spec(), spec()],
        out_specs=[spec(), spec()],
        scratch_shapes=scratch,
        compiler_params=_cparams(("arbitrary",)),
        name=f"band_attn_d{dil}",
    )(q, k, v)


def _win_attn_kernel(sink_ref, q_ref, k_ref, vt_ref, o_ref, bias_ref,
                     s0_ref, s1_ref, e0_ref, e1_ref, t0_ref, t1_ref, *, slopes, layer):
    s_len = k_ref.shape[1]
    tc = q_ref.shape[2]
    nblk = s_len // QBLK
    nloc = tc // QBLK
    width = QBLK + 2 * WIN_RADIUS
    cols = WIN_GROUP * QBLK
    s_bufs, e_bufs, t_bufs = (s0_ref, s1_ref), (e0_ref, e1_ref), (t0_ref, t1_ref)

    @pl.when((pl.program_id(0) == 0) & (pl.program_id(1) == 0))
    def _():
        for var, off in enumerate((0, WIN_RADIUS, width - QBLK)):
            for j in range(2):
                heads = tuple(j * WIN_GROUP + t for t in range(WIN_GROUP))
                bias_ref[var, j] = _band_bias_t((width, cols), heads, slopes, 1.0,
                                                WIN_RADIUS, off)

    qlane = lax.broadcasted_iota(jnp.int32, (1, cols), 1)
    klo = lax.broadcasted_iota(jnp.int32, (width, LANES), 1) < HALF
    base = pl.program_id(1) * nloc

    def sink_row(j):
        row = jnp.full((1, cols), sink_ref[layer, j * WIN_GROUP] * LOG2E, _F32)
        for t in range(1, WIN_GROUP):
            row = jnp.where(qlane >= t * QBLK, sink_ref[layer, j * WIN_GROUP + t] * LOG2E, row)
        return row

    def kstart_of(bl):
        ws, variant = _window(base + bl, nblk, WIN_RADIUS, width)
        return pl.multiple_of(ws, QBLK), variant

    def scores(bl, par):
        kstart, variant = kstart_of(bl)
        qstart = pl.multiple_of(bl * QBLK, QBLK)
        q = q_ref[0, :, pl.ds(qstart, QBLK), :].reshape(cols, LANES)
        kw = k_ref[0, pl.ds(kstart, width), :]
        zero = jnp.zeros_like(kw)
        for j in range(2):
            kh = jnp.where(klo, kw, zero) if j == 0 else jnp.where(klo, zero, kw)
            s_bufs[par][j] = _trans_b_dot(kh, q) + bias_ref[variant, j]

    def numerators(par):
        for j in range(2):
            sc = s_bufs[par][j]
            sink = sink_row(j)
            m = jnp.maximum(jnp.max(sc, axis=0, keepdims=True), sink)
            e_bufs[par][j] = jnp.exp2(sc - m).astype(_BF16)
            t_bufs[par][j] = jnp.broadcast_to(jnp.exp2(sink - m), (8, cols))

    def outputs(bl, par):
        kstart, _ = kstart_of(bl)
        qstart = pl.multiple_of(bl * QBLK, QBLK)
        vt = vt_ref[0, :, pl.ds(kstart, width)]
        ones = jnp.ones((ONES_ROWS, width), _BF16)
        res = []
        for j in range(2):
            lhs = jnp.concatenate([vt[j * HEAD_DIM:(j + 1) * HEAD_DIM], ones], axis=0)
            o = jnp.dot(lhs, e_bufs[par][j], preferred_element_type=_F32)
            den = o[HEAD_DIM:HEAD_DIM + 1] + t_bufs[par][j][0:1]
            res.append(o[:HEAD_DIM] / den)
        for t in range(WIN_GROUP):
            both = jnp.concatenate([r[:, t * QBLK:(t + 1) * QBLK] for r in res], axis=0)
            o_ref[0, pl.ds(qstart, QBLK), t * LANES:(t + 1) * LANES] = both.T.astype(_BF16)

    scores(0, 0)
    scores(1, 1)
    numerators(0)

    def body(p, carry):
        it = 2 * p
        scores(it, 0)
        numerators(1)
        outputs(it - 2, 0)
        scores(it + 1, 1)
        numerators(0)
        outputs(it - 1, 1)
        return carry

    lax.fori_loop(1, nloc // 2, body, 0, unroll=WIN_UNROLL)
    numerators(1)
    outputs(nloc - 2, 0)
    outputs(nloc - 1, 1)


def _win_attn(sink, layer, q, k, vt, slopes, tc):
    b, _, s, _ = q.shape
    width = QBLK + 2 * WIN_RADIUS
    cols = WIN_GROUP * QBLK
    assert (tc // QBLK) % 2 == 0 and tc // QBLK >= 4
    return pl.pallas_call(
        functools.partial(_win_attn_kernel, slopes=slopes, layer=layer),
        out_shape=jax.ShapeDtypeStruct((b, s, WIN_GROUP * LANES), _BF16),
        grid=(b, s // tc),
        in_specs=[
            pl.BlockSpec(memory_space=pltpu.SMEM),
            pl.BlockSpec((1, WIN_GROUP, tc, LANES), lambda bi, i: (bi, 0, i, 0)),
            pl.BlockSpec((1, s, LANES), lambda bi, i: (bi, 0, 0)),
            pl.BlockSpec((1, LANES, s), lambda bi, i: (bi, 0, 0)),
        ],
        out_specs=pl.BlockSpec((1, tc, WIN_GROUP * LANES), lambda bi, i: (bi, i, 0)),
        scratch_shapes=[pltpu.VMEM((3, 2, width, cols), _F32)]
                       + [pltpu.VMEM((2, width, cols), _F32)] * 2
                       + [pltpu.VMEM((2, width, cols), _BF16)] * 2
                       + [pltpu.VMEM((2, 8, cols), _F32)] * 2,
        compiler_params=_cparams(("arbitrary", "arbitrary")),
        name="win_attn",
    )(sink, q, k, vt)


def _token_kernel(x_ref, mod_ref, ya_ref,
                  o0_ref, o1_ref, o2_ref, l0_ref, l1_ref, l2_ref, yc_ref, wo_ref,
                  g2_ref, wg_ref, wu_ref, wd_ref, fg_ref,
                  out_ref, mix_ref, *, final, ff_chunks):
    cw = 2 * LANES
    mod = mod_ref[0, 0]
    mix_ref[:, :cw] = ya_ref[0]

    lses = [l0_ref[0], l1_ref[0], l2_ref[0]]
    mx = jnp.maximum(jnp.maximum(lses[0], lses[1]), lses[2])
    es = [jnp.exp(ls - mx) for ls in lses]
    den = es[0] + es[1] + es[2]
    for g, o_ref in enumerate((o0_ref, o1_ref, o2_ref)):
        yb = (es[g] / den) * o_ref[0]
        mix_ref[:, cw + g * LANES:cw + (g + 1) * LANES] = yb.astype(_BF16)
    mix_ref[:, cw + 3 * LANES:] = yc_ref[0]

    out_ref[0] = x_ref[0] + mod[2:3] * jnp.dot(mix_ref[...], wo_ref[...],
                                               preferred_element_type=_F32)

    h = _modulated_norm(out_ref[0], g2_ref[...], mod[3:4], mod[4:5]).astype(_BF16)
    for c0, c1 in ff_chunks:
        gate = jnp.dot(h, wg_ref[:, c0:c1], preferred_element_type=_F32)
        up = jnp.dot(h, wu_ref[:, c0:c1], preferred_element_type=_F32)
        act = ((gate * (1.0 / (1.0 + jnp.exp(-gate)))) * up).astype(_BF16)
        out_ref[0] += mod[5:6] * jnp.dot(act, wd_ref[c0:c1, :], preferred_element_type=_F32)
    if final:
        y = out_ref[0]
        ms = jnp.mean(y * y, axis=-1, keepdims=True)
        out_ref[0] = (y * lax.rsqrt(ms + EPS)) * fg_ref[...]


def _ff_chunks(dff, n_chunks=4):
    mxu_k = 2 * LANES
    tiles = -(-dff // mxu_k)
    cuts = sorted({min(dff, -(-tiles * c // n_chunks) * mxu_k) for c in range(n_chunks + 1)})
    return tuple(zip(cuts[:-1], cuts[1:]))


def _token_mixers_out(x, mod, l, ya, outs, lses, yc, w_out_p,
                      g2, wg, wu, wd, final_g, final, tm):
    b, s, d = x.shape
    dff = wg.shape[-1]
    cw = 2 * LANES
    tok = lambda width: pl.BlockSpec((1, tm, width), lambda bi, i: (bi, i, 0))
    const = lambda shape: pl.BlockSpec((None,) + shape, lambda bi, i: (l,) + (0,) * len(shape),
                                       pipeline_mode=pl.Buffered(1))
    return pl.pallas_call(
        functools.partial(_token_kernel, final=final, ff_chunks=_ff_chunks(dff)),
        out_shape=jax.ShapeDtypeStruct((b, s, d), _F32),
        grid=(b, s // tm),
        in_specs=[
            tok(d),
            pl.BlockSpec((1, 1, N_MOD, d), lambda bi, i: (l, bi, 0, 0)),
            tok(cw),
            tok(LANES), tok(LANES), tok(LANES),
            tok(LANES), tok(LANES), tok(LANES),
            tok(WIN_GROUP * LANES),
            const((d, d)),
            const((1, d)),
            const((d, dff)), const((d, dff)), const((dff, d)),
            pl.BlockSpec((1, d), lambda bi, i: (0, 0)),
        ],
        out_specs=tok(d),
        scratch_shapes=[pltpu.VMEM((tm, d), _BF16)],
        compiler_params=_cparams(("arbitrary", "arbitrary")),
        name="token_mix_ffn",
    )(x, mod, ya, *outs, *lses, yc, w_out_p, g2, wg, wu, wd, final_g)


def _in_proj_columns(d_model):
    pool_w = d_model // 4
    dil_w = 2 * len(DIL_PAIRS) * HEAD_DIM
    o_qb = pool_w
    o_kb = o_qb + dil_w
    o_vb = o_kb + dil_w
    o_qc = o_vb + dil_w
    o_kc = o_qc + 2 * WIN_GROUP * HEAD_DIM
    o_vc = o_kc + 2 * HEAD_DIM
    cols = list(range(pool_w))
    for g in range(len(DIL_PAIRS)):
        for base in (o_qb, o_kb, o_vb):
            cols += range(base + g * LANES, base + (g + 1) * LANES)
    win_heads = []
    for t in range(WIN_GROUP):
        for j in range(2):
            win_heads.append(j * WIN_GROUP + t)
            cols += range(o_qc + (j * WIN_GROUP + t) * HEAD_DIM,
                          o_qc + (j * WIN_GROUP + t + 1) * HEAD_DIM)
    cols += range(o_kc, o_kc + LANES)
    cols += range(o_vc, o_vc + LANES)
    qscale = np.ones((len(cols),), np.float32)
    for g in range(len(DIL_PAIRS)):
        qscale[pool_w + 3 * g * LANES:pool_w + (3 * g + 1) * LANES] = HEAD_DIM ** -0.5 * LOG2E
    q0 = pool_w + 3 * len(DIL_PAIRS) * LANES
    qscale[q0:q0 + WIN_GROUP * LANES] = HEAD_DIM ** -0.5 * LOG2E
    return np.asarray(cols), qscale, win_heads


def _runs(idx, key=None):
    idx = np.asarray(idx)
    key = np.zeros(len(idx)) if key is None else np.asarray(key)
    cuts = ([0] + [i for i in range(1, len(idx))
                   if idx[i] != idx[i - 1] + 1 or key[i] != key[i - 1]] + [len(idx)])
    return [(int(idx[a]), b - a, a) for a, b in zip(cuts[:-1], cuts[1:])]


def _take_runs(w, idx, axis):
    parts = [lax.slice_in_dim(w, src, src + n, axis=axis) for src, n, _ in _runs(idx)]
    return jnp.concatenate(parts, axis=axis)


def _permute_cols_kernel(w_ref, o_ref, *, runs):
    for src, n, dst, scale in runs:
        o_ref[0, :, dst:dst + n] = (w_ref[0, :, src:src + n] * scale).astype(o_ref.dtype)


def _permute_cols_bf16(w, cols, scale, tr):
    depth, rows, n = w.shape
    runs = [(src, length, dst, float(scale[dst])) for src, length, dst in _runs(cols, scale)]
    spec = pl.BlockSpec((1, tr, n), lambda l, i: (l, i, 0))
    return pl.pallas_call(
        functools.partial(_permute_cols_kernel, runs=tuple(runs)),
        out_shape=jax.ShapeDtypeStruct(w.shape, _BF16),
        grid=(depth, rows // tr),
        in_specs=[spec],
        out_specs=spec,
        compiler_params=_cparams(("arbitrary", "arbitrary")),
        name="permute_cols",
    )(w)


def kernel(x, c, norm1_g, norm2_g, w_ada, b_ada, w_in, w_pool, pool_scale, sink_logit,
           w_out, w_gate, w_up, w_down, final_g):
    b, s, d = x.shape
    depth = w_in.shape[0]
    slopes = _alibi_slopes(2 * WIN_GROUP + 2 * len(DIL_PAIRS))
    slopes_win = tuple(float(v) for v in slopes[:2 * WIN_GROUP])
    slopes_dil = slopes[2 * WIN_GROUP:]
    cols, qscale, win_heads = _in_proj_columns(d)
    pool_w = d // 4
    dil_w = 2 * len(DIL_PAIRS) * HEAD_DIM
    mix_rows = np.concatenate([
        np.arange(pool_w + dil_w),
        np.concatenate([pool_w + dil_w + h * HEAD_DIM + np.arange(HEAD_DIM) for h in win_heads]),
    ])
    n_grp = len(POOL_WINDOWS)
    pg = pool_w // n_grp

    mod = _modulation(c, w_ada, b_ada).reshape(depth, b, N_MOD, d)
    w_in_p = _permute_cols_bf16(w_in, cols, qscale, WEIGHT_ROWS)
    w_out_p = _take_runs(w_out, mix_rows, 1).astype(_BF16)
    zpad = lambda n: jnp.zeros((depth, pg, n * pg), _F32)
    wpool_bd = jnp.concatenate(
        [jnp.concatenate([zpad(g), w_pool[:, g], zpad(n_grp - 1 - g)], axis=2)
         for g in range(n_grp)], axis=1).astype(_BF16)
    g1, g2 = norm1_g.reshape(depth, 1, d), norm2_g.reshape(depth, 1, d)
    ps = pool_scale.reshape(depth, 1, pool_w)
    tm = min(TOKEN_TILE, s)
    for l in range(depth):
        (ya, q0, k0, v0, q1, k1, v1, q2, k2, v2, qc, kc, vc), cast = _in_proj(
            x, mod, l, g1, w_in_p, wpool_bd, ps, tm,
            to_cast=(w_gate, w_up, w_down) if l == 0 else ())
        if l == 0:
            wg, wu, wd = cast
        outs, lses = [], []
        for g, ((_, dil), (q, k, v)) in enumerate(zip(
                DIL_PAIRS, ((q0, k0, v0), (q1, k1, v1), (q2, k2, v2)))):
            sl = tuple(float(v_) for v_ in slopes_dil[2 * g:2 * g + 2])
            o, ls = _band_attn(q.reshape(b, s, LANES), k.reshape(b, s, LANES),
                               v.reshape(b, s, LANES), dil, sl)
            outs.append(o)
            lses.append(ls)
        yc = _win_attn(sink_logit, l, qc, kc, vc, slopes_win, s)
        x = _token_mixers_out(x, mod, l, ya, outs, lses, yc, w_out_p, g2, wg, wu, wd,
                              final_g.reshape(1, d), l == depth - 1, tm)
    return x
```

```python
import functools
import math

import jax
import jax.numpy as jnp
import numpy as np
from jax import lax
from jax.experimental import pallas as pl
from jax.experimental.pallas import tpu as pltpu

LANES = 128
HEAD_DIM = 64
HALF = LANES // 2
POOL_WINDOWS = (2, 4, 8, 16)
POOL_HALO = 8
REGROUP_STRIDE = 4
POOL_PAD = 16
DIL_PAIRS = ((128, 1), (512, 4), (2048, 16))
DIL_RADIUS = 64
WIN_RADIUS = 128
WIN_GROUP = 3
N_MOD = 6
EPS = 1e-6
NEG = -1e30
QBLK = 128
ONES_ROWS = 16
LOG2E = math.log2(math.e)
BAND_UNROLL = 6
WIN_UNROLL = 3
TOKEN_TILE = 1024
WEIGHT_ROWS = 256
VMEM_LIMIT = 56 * 1024 * 1024

_F32 = jnp.float32
_BF16 = jnp.bfloat16


def _alibi_slopes(n):
    i = np.arange(1, n + 1, dtype=np.float32)
    return np.exp2(np.float32(-8.0) * i / np.float32(n)).astype(np.float32)


def _cparams(sem):
    return pltpu.CompilerParams(dimension_semantics=sem, vmem_limit_bytes=VMEM_LIMIT)


def _mod_kernel(c_ref, w_ref, b_ref, o_ref):
    c = c_ref[...]
    act = (c * (1.0 / (1.0 + jnp.exp(-c)))).astype(_BF16)
    w = w_ref[0].astype(_BF16)
    o_ref[0] = jnp.dot(act, w, preferred_element_type=_F32) + b_ref[0]


def _modulation(c, w_ada, b_ada):
    depth, d, nd = w_ada.shape
    b = c.shape[0]
    tn = nd // 2
    return pl.pallas_call(
        _mod_kernel,
        out_shape=jax.ShapeDtypeStruct((depth, b, nd), _F32),
        grid=(depth, nd // tn),
        in_specs=[
            pl.BlockSpec((b, d), lambda l, j: (0, 0)),
            pl.BlockSpec((1, d, tn), lambda l, j: (l, 0, j)),
            pl.BlockSpec((1, 1, tn), lambda l, j: (l, 0, j)),
        ],
        out_specs=pl.BlockSpec((1, b, tn), lambda l, j: (l, 0, j)),
        compiler_params=_cparams(("arbitrary", "arbitrary")),
        name="modulation",
    )(c, w_ada, b_ada.reshape(depth, 1, nd))


def _modulated_norm(x, g, shift, scale):
    ms = jnp.mean(x * x, axis=-1, keepdims=True)
    return (x * lax.rsqrt(ms + EPS)) * (g * (1.0 + scale)) + shift


def _pooled_tokens(ubuf_ref, t0, s_len):
    tm = ubuf_ref.shape[0] - 2 * POOL_HALO - POOL_PAD
    cols = []
    for c in range(2):
        r_lo, r_hi = POOL_WINDOWS[2 * c] // 2, POOL_WINDOWS[2 * c + 1] // 2
        lanes = slice(c * LANES, (c + 1) * LANES)
        rows_from = lambda k0, cnt: ubuf_ref[k0:k0 + cnt, lanes]
        lane = lax.broadcasted_iota(jnp.int32, (1, LANES), 1)
        radius = jnp.where(lane < HALF, r_lo, r_hi)
        u = rows_from(POOL_HALO, tm)
        if r_lo < 2:
            wsum = u
            for k in range(1, r_hi + 1):
                pair = rows_from(POOL_HALO - k, tm) + rows_from(POOL_HALO + k, tm)
                wsum = wsum + (pair if k <= r_lo else jnp.where(lane < HALF, 0.0, pair))
        else:
            rows = tm + 3 * POOL_HALO
            level, span, wsums = rows_from(0, rows) + rows_from(1, rows), 2, {}
            while True:
                if span in (r_lo, r_hi):
                    wsums[span] = (level[POOL_HALO - span:POOL_HALO - span + tm]
                                   + level[POOL_HALO:POOL_HALO + tm]
                                   + rows_from(POOL_HALO + span, tm))
                if span >= r_hi:
                    break
                rows -= POOL_HALO
                level = level[:rows] + level[span:span + rows]
                span *= 2
            wsum = jnp.where(lane < HALF, wsums[r_lo], wsums[r_hi])
        t = t0 + lax.broadcasted_iota(jnp.int32, (tm, LANES), 0)
        cnt = jnp.minimum(t + radius + 1, s_len) - jnp.maximum(t - radius, 0)
        cols.append((wsum / cnt.astype(_F32) - u).astype(_BF16))
    return jnp.concatenate(cols, axis=1)


def _in_proj_kernel(*refs, s_len, n_cast):
    for src, dst in zip(refs[:n_cast], refs[len(refs) - 3 - n_cast:len(refs) - 3]):
        dst[...] = src[...].astype(dst.dtype)
    (x_ref, xp_ref, xn_ref, mod_ref, g_ref, w_ref, wp_ref, ps_ref,
     ya_ref, q0_ref, k0_ref, v0_ref, q1_ref, k1_ref, v1_ref,
     q2_ref, k2_ref, v2_ref, qc_ref, kc_ref, vct_ref) = refs[n_cast:len(refs) - 3 - n_cast]
    zs_ref, zt_ref, ubuf_ref = refs[-3:]
    tm = x_ref.shape[1]
    i = pl.program_id(1)
    cw = 2 * LANES
    mod = mod_ref[0, 0]
    norm = lambda rows: _modulated_norm(rows, g_ref[...], mod[0:1], mod[1:2]).astype(_BF16)
    h = norm(x_ref[0])

    def proj(lhs, t0, t1):
        return jnp.dot(lhs, w_ref[:, t0 * LANES:t1 * LANES], preferred_element_type=_F32)

    halo = proj(norm(jnp.concatenate([xp_ref[0], xn_ref[0]], axis=0)), 0, 2)
    ubuf_ref[0:POOL_HALO] = jnp.where(i > 0, halo[:POOL_HALO], 0.0)
    ubuf_ref[POOL_HALO:POOL_HALO + tm] = proj(h, 0, 2)
    ubuf_ref[POOL_HALO + tm:2 * POOL_HALO + tm] = jnp.where(i < pl.num_programs(1) - 1,
                                                             halo[POOL_HALO:], 0.0)
    ubuf_ref[2 * POOL_HALO + tm:] = jnp.zeros((POOL_PAD, cw), _F32)
    pooled = _pooled_tokens(ubuf_ref, i * tm, s_len)

    z_lo, z_hi = proj(h, 2, 8), proj(h, 8, 16)

    def tile(i):
        z, j = (z_lo, i - 2) if i < 8 else (z_hi, i - 8)
        return z[:, j * LANES:(j + 1) * LANES]

    for j, ref in enumerate((q0_ref, k0_ref, v0_ref)):
        ref[0] = tile(2 + j).astype(_BF16)
    for group_refs, (_, dil), base in (((q1_ref, k1_ref, v1_ref), DIL_PAIRS[1], 5),
                                       ((q2_ref, k2_ref, v2_ref), DIL_PAIRS[2], 8)):
        inner = min(dil, REGROUP_STRIDE)
        outer = dil // inner
        for j, ref in enumerate(group_refs):
            zs_ref[base - 5 + j] = tile(base + j)
            for ra in range(inner):
                part = zs_ref[base - 5 + j, pl.ds(ra, tm // inner, stride=inner), :]
                if outer == 1:
                    ref[0, ra] = part.astype(_BF16)
                else:
                    zt_ref[ra] = part
            for rb in range(outer if outer > 1 else 0):
                for ra in range(inner):
                    ref[0, ra + inner * rb] = zt_ref[
                        ra, pl.ds(rb, tm // dil, stride=outer), :].astype(_BF16)
    for t in range(WIN_GROUP):
        qc_ref[0, t] = tile(11 + t).astype(_BF16)
    kc_ref[0] = tile(14).astype(_BF16)
    vct_ref[0] = tile(15).T.astype(_BF16)
    ya = jnp.dot(pooled, wp_ref[...], preferred_element_type=_F32) * ps_ref[...]
    ya_ref[0] = ya.astype(_BF16)


def _in_proj(x, mod, l, g, w_in_p, wpool_bd, pool_scale, tm, to_cast=()):
    b, s, d = x.shape
    n_in = w_in_p.shape[-1]
    cw = 2 * LANES
    hb = tm // POOL_HALO
    nhb = s // POOL_HALO
    d1, d2 = DIL_PAIRS[1][1], DIL_PAIRS[2][1]
    nat = lambda: pl.BlockSpec((1, tm, LANES), lambda bi, i: (bi, i, 0))
    res = lambda dil: pl.BlockSpec((1, dil, tm // dil, LANES), lambda bi, i: (bi, 0, i, 0))
    full = lambda shape: pl.BlockSpec((None,) + shape, lambda bi, i: (l,) + (0,) * len(shape))
    sds = jax.ShapeDtypeStruct
    out_shape = (
        [sds((b, s, cw), _BF16)]
        + [sds((b, s, LANES), _BF16)] * 3
        + [sds((b, d1, s // d1, LANES), _BF16)] * 3
        + [sds((b, d2, s // d2, LANES), _BF16)] * 3
        + [sds((b, WIN_GROUP, s, LANES), _BF16)]
        + [sds((b, s, LANES), _BF16), sds((b, LANES, s), _BF16)]
    )
    out_specs = (
        [pl.BlockSpec((1, tm, 2 * LANES), lambda bi, i: (bi, i, 0))]
        + [nat() for _ in range(3)]
        + [res(d1) for _ in range(3)]
        + [res(d2) for _ in range(3)]
        + [pl.BlockSpec((1, WIN_GROUP, tm, LANES), lambda bi, i: (bi, 0, i, 0))]
        + [nat(), pl.BlockSpec((1, LANES, tm), lambda bi, i: (bi, 0, i))]
    )
    nt = s // tm
    cast_in, cast_specs = [], []
    for w in to_cast:
        w2 = w.reshape(-1, w.shape[-1])
        slab = w2.shape[0] // (b * nt)
        assert slab * b * nt == w2.shape[0] and slab % 16 == 0
        cast_in.append(w2)
        cast_specs.append(pl.BlockSpec((slab, w2.shape[1]), lambda bi, i: (bi * nt + i, 0)))
    outs = pl.pallas_call(
        functools.partial(_in_proj_kernel, s_len=s, n_cast=len(to_cast)),
        out_shape=out_shape + [sds(w2.shape, _BF16) for w2 in cast_in],
        grid=(b, nt),
        in_specs=cast_specs + [
            pl.BlockSpec((1, tm, d), lambda bi, i: (bi, i, 0)),
            pl.BlockSpec((1, POOL_HALO, d), lambda bi, i: (bi, jnp.maximum(i * hb - 1, 0), 0)),
            pl.BlockSpec((1, POOL_HALO, d),
                         lambda bi, i: (bi, jnp.minimum((i + 1) * hb, nhb - 1), 0)),
            pl.BlockSpec((1, 1, N_MOD, d), lambda bi, i: (l, bi, 0, 0)),
            full((1, d)),
            full((d, n_in)),
            full((cw, cw)),
            full((1, cw)),
        ],
        out_specs=out_specs + cast_specs,
        scratch_shapes=[pltpu.VMEM((6, tm, LANES), _F32),
                        pltpu.VMEM((REGROUP_STRIDE, tm // REGROUP_STRIDE, LANES), _F32),
                        pltpu.VMEM((tm + 2 * POOL_HALO + POOL_PAD, cw), _F32)],
        compiler_params=_cparams(("arbitrary", "arbitrary")),
        name="in_proj",
    )(*cast_in, x, x, x, mod, g, w_in_p, wpool_bd, pool_scale)
    n_main = len(out_shape)
    return outs[:n_main], [o.reshape(w.shape) for o, w in zip(outs[n_main:], to_cast)]


def _band_bias_t(shape, col_heads, slopes, dist_scale, radius, offset):
    row = lax.broadcasted_iota(jnp.int32, shape, 0)
    col = lax.broadcasted_iota(jnp.int32, shape, 1)
    dist = jnp.abs(row - offset - (col % QBLK))
    slope = jnp.full(shape, slopes[col_heads[0]] * dist_scale, _F32)
    for i, hd in enumerate(col_heads[1:], start=1):
        slope = jnp.where(col >= i * QBLK, np.float32(slopes[hd] * dist_scale), slope)
    return jnp.where(dist <= radius, (-slope * dist.astype(_F32)) * LOG2E, NEG)


def _window(i, nblk, radius, width):
    seq = nblk * QBLK
    ws = jnp.clip(i * QBLK - radius, 0, seq - width)
    variant = jnp.where(i == 0, 0, jnp.where(i == nblk - 1, 2, 1))
    return ws, variant


def _trans_b_dot(a, b):
    return lax.dot_general(a, b, (((1,), (1,)), ((), ())), preferred_element_type=_F32)


def _band_attn_kernel(q_ref, k_ref, v_ref, o_ref, lse_ref, bias_ref, *scratch, dil, slopes):
    s_len = q_ref.shape[1]
    n = s_len // dil
    nblk = n // QBLK
    width = QBLK + 2 * DIL_RADIUS
    cols = 2 * QBLK

    @pl.when(pl.program_id(0) == 0)
    def _():
        for var, off in enumerate((0, DIL_RADIUS, width - QBLK)):
            bias_ref[var] = _band_bias_t((width, cols), (0, 1), slopes, float(dil),
                                         DIL_RADIUS, off)

    lo = lax.broadcasted_iota(jnp.int32, (QBLK, LANES), 1) < HALF
    s_bufs, e_bufs, m_bufs = scratch[0:2], scratch[2:4], scratch[4:6]
    ntot = s_len // QBLK

    def locate(blk):
        r = blk // nblk
        i = blk % nblk
        ws, variant = _window(i, nblk, DIL_RADIUS, width)
        return r, i, pl.multiple_of(r * n + ws, DIL_RADIUS), variant

    def scores(blk, par):
        _, _, kstart, variant = locate(blk)
        qstart = pl.multiple_of(blk * QBLK, QBLK)
        q = q_ref[0, pl.ds(qstart, QBLK), :]
        kw = k_ref[0, pl.ds(kstart, width), :]
        zero = jnp.zeros_like(q)
        qs = jnp.concatenate([jnp.where(lo, q, zero), jnp.where(lo, zero, q)], axis=0)
        s_bufs[par][...] = _trans_b_dot(kw, qs) + bias_ref[variant]

    def numerators(par):
        sc = s_bufs[par][...]
        m = jnp.max(sc, axis=0, keepdims=True)
        e_bufs[par][...] = jnp.exp2(sc - m).astype(_BF16)
        m_bufs[par][...] = jnp.broadcast_to(m, (8, cols))

    def outputs(blk, par):
        r, i, kstart, _ = locate(blk)
        vt = v_ref[0, pl.ds(kstart, width), :].T
        lhs = jnp.concatenate([vt, jnp.ones((ONES_ROWS, width), _BF16)], axis=0)
        o = jnp.dot(lhs, e_bufs[par][...], preferred_element_type=_F32)
        den = o[LANES:LANES + 1]
        lse_row = m_bufs[par][0:1] * (1.0 / LOG2E) + jnp.log(den)
        res = jnp.concatenate(
            [o[h * HEAD_DIM:(h + 1) * HEAD_DIM, h * QBLK:(h + 1) * QBLK]
             / den[:, h * QBLK:(h + 1) * QBLK] for h in range(2)], axis=0)
        lse2 = jnp.concatenate(
            [jnp.broadcast_to(lse_row[:, h * QBLK:(h + 1) * QBLK], (HEAD_DIM, QBLK))
             for h in range(2)], axis=0)
        out, lse = res.T, lse2.T
        if dil == 1:
            qstart = pl.multiple_of(blk * QBLK, QBLK)
            o_ref[0, pl.ds(qstart, QBLK), :] = out.astype(_BF16)
            lse_ref[0, pl.ds(qstart, QBLK), :] = lse
        else:
            tstart = i * (QBLK * dil) + r
            scratch[6][pl.ds(tstart, QBLK, stride=dil), :] = out
            lse_ref[0, pl.ds(tstart, QBLK, stride=dil), :] = lse

    scores(0, 0)
    scores(1, 1)
    numerators(0)

    def body(p, carry):
        it = 2 * p
        scores(it, 0)
        numerators(1)
        outputs(it - 2, 0)
        scores(it + 1, 1)
        numerators(0)
        outputs(it - 1, 1)
        return carry

    lax.fori_loop(1, ntot // 2, body, 0, unroll=BAND_UNROLL)
    numerators(1)
    outputs(ntot - 2, 0)
    outputs(ntot - 1, 1)
    if dil != 1:
        o_ref[0] = scratch[6][...].astype(_BF16)


def _band_attn(q, k, v, dil, slopes):
    b, s, _ = q.shape
    width = QBLK + 2 * DIL_RADIUS
    spec = lambda: pl.BlockSpec((1, s, LANES), lambda bi: (bi, 0, 0))
    cols = 2 * QBLK
    scratch = ([pltpu.VMEM((3, width, cols), _F32)]
               + [pltpu.VMEM((width, cols), _F32)] * 2
               + [pltpu.VMEM((width, cols), _BF16)] * 2
               + [pltpu.VMEM((8, cols), _F32)] * 2)
    if dil != 1:
        scratch.append(pltpu.VMEM((s, LANES), _F32))
    return pl.pallas_call(
        functools.partial(_band_attn_kernel, dil=dil, slopes=slopes),
        out_shape=[jax.ShapeDtypeStruct((b, s, LANES), _BF16),
                   jax.ShapeDtypeStruct((b, s, LANES), _F32)],
        grid=(b,),
        in_specs=[spec(), spec(), spec()],
        out_specs=[spec(), spec()],
        scratch_shapes=scratch,
        compiler_params=_cparams(("arbitrary",)),
        name=f"band_attn_d{dil}",
    )(q, k, v)


def _win_attn_kernel(sink_ref, q_ref, k_ref, vt_ref, o_ref, bias_ref,
                     s0_ref, s1_ref, e0_ref, e1_ref, t0_ref, t1_ref, *, slopes, layer):
    s_len = k_ref.shape[1]
    tc = q_ref.shape[2]
    nblk = s_len // QBLK
    nloc = tc // QBLK
    width = QBLK + 2 * WIN_RADIUS
    cols = WIN_GROUP * QBLK
    s_bufs, e_bufs, t_bufs = (s0_ref, s1_ref), (e0_ref, e1_ref), (t0_ref, t1_ref)

    @pl.when((pl.program_id(0) == 0) & (pl.program_id(1) == 0))
    def _():
        for var, off in enumerate((0, WIN_RADIUS, width - QBLK)):
            for j in range(2):
                heads = tuple(j * WIN_GROUP + t for t in range(WIN_GROUP))
                bias_ref[var, j] = _band_bias_t((width, cols), heads, slopes, 1.0,
                                                WIN_RADIUS, off)

    qlane = lax.broadcasted_iota(jnp.int32, (1, cols), 1)
    klo = lax.broadcasted_iota(jnp.int32, (width, LANES), 1) < HALF
    base = pl.program_id(1) * nloc

    def sink_row(j):
        row = jnp.full((1, cols), sink_ref[layer, j * WIN_GROUP] * LOG2E, _F32)
        for t in range(1, WIN_GROUP):
            row = jnp.where(qlane >= t * QBLK, sink_ref[layer, j * WIN_GROUP + t] * LOG2E, row)
        return row

    def kstart_of(bl):
        ws, variant = _window(base + bl, nblk, WIN_RADIUS, width)
        return pl.multiple_of(ws, QBLK), variant

    def scores(bl, par):
        kstart, variant = kstart_of(bl)
        qstart = pl.multiple_of(bl * QBLK, QBLK)
        q = q_ref[0, :, pl.ds(qstart, QBLK), :].reshape(cols, LANES)
        kw = k_ref[0, pl.ds(kstart, width), :]
        zero = jnp.zeros_like(kw)
        for j in range(2):
            kh = jnp.where(klo, kw, zero) if j == 0 else jnp.where(klo, zero, kw)
            s_bufs[par][j] = _trans_b_dot(kh, q) + bias_ref[variant, j]

    def numerators(par):
        for j in range(2):
            sc = s_bufs[par][j]
            sink = sink_row(j)
            m = jnp.maximum(jnp.max(sc, axis=0, keepdims=True), sink)
            e_bufs[par][j] = jnp.exp2(sc - m).astype(_BF16)
            t_bufs[par][j] = jnp.broadcast_to(jnp.exp2(sink - m), (8, cols))

    def outputs(bl, par):
        kstart, _ = kstart_of(bl)
        qstart = pl.multiple_of(bl * QBLK, QBLK)
        vt = vt_ref[0, :, pl.ds(kstart, width)]
        ones = jnp.ones((ONES_ROWS, width), _BF16)
        res = []
        for j in range(2):
            lhs = jnp.concatenate([vt[j * HEAD_DIM:(j + 1) * HEAD_DIM], ones], axis=0)
            o = jnp.dot(lhs, e_bufs[par][j], preferred_element_type=_F32)
            den = o[HEAD_DIM:HEAD_DIM + 1] + t_bufs[par][j][0:1]
            res.append(o[:HEAD_DIM] / den)
        for t in range(WIN_GROUP):
            both = jnp.concatenate([r[:, t * QBLK:(t + 1) * QBLK] for r in res], axis=0)
            o_ref[0, pl.ds(qstart, QBLK), t * LANES:(t + 1) * LANES] = both.T.astype(_BF16)

    scores(0, 0)
    scores(1, 1)
    numerators(0)

    def body(p, carry):
        it = 2 * p
        scores(it, 0)
        numerators(1)
        outputs(it - 2, 0)
        scores(it + 1, 1)
        numerators(0)
        outputs(it - 1, 1)
        return carry

    lax.fori_loop(1, nloc // 2, body, 0, unroll=WIN_UNROLL)
    numerators(1)
    outputs(nloc - 2, 0)
    outputs(nloc - 1, 1)


def _win_attn(sink, layer, q, k, vt, slopes, tc):
    b, _, s, _ = q.shape
    width = QBLK + 2 * WIN_RADIUS
    cols = WIN_GROUP * QBLK
    assert (tc // QBLK) % 2 == 0 and tc // QBLK >= 4
    return pl.pallas_call(
        functools.partial(_win_attn_kernel, slopes=slopes, layer=layer),
        out_shape=jax.ShapeDtypeStruct((b, s, WIN_GROUP * LANES), _BF16),
        grid=(b, s // tc),
        in_specs=[
            pl.BlockSpec(memory_space=pltpu.SMEM),
            pl.BlockSpec((1, WIN_GROUP, tc, LANES), lambda bi, i: (bi, 0, i, 0)),
            pl.BlockSpec((1, s, LANES), lambda bi, i: (bi, 0, 0)),
            pl.BlockSpec((1, LANES, s), lambda bi, i: (bi, 0, 0)),
        ],
        out_specs=pl.BlockSpec((1, tc, WIN_GROUP * LANES), lambda bi, i: (bi, i, 0)),
        scratch_shapes=[pltpu.VMEM((3, 2, width, cols), _F32)]
                       + [pltpu.VMEM((2, width, cols), _F32)] * 2
                       + [pltpu.VMEM((2, width, cols), _BF16)] * 2
                       + [pltpu.VMEM((2, 8, cols), _F32)] * 2,
        compiler_params=_cparams(("arbitrary", "arbitrary")),
        name="win_attn",
    )(sink, q, k, vt)


def _token_kernel(x_ref, mod_ref, ya_ref,
                  o0_ref, o1_ref, o2_ref, l0_ref, l1_ref, l2_ref, yc_ref, wo_ref,
                  g2_ref, wg_hbm, wu_hbm, wd_hbm, fg_ref,
                  out_ref, mix_ref, wg_ref, wu_ref, wd_ref, sem, *, final, ff_chunks, layer):
    cw = 2 * LANES
    mod = mod_ref[0, 0]
    first = (pl.program_id(0) == 0) & (pl.program_id(1) == 0)
    fetches = [pltpu.make_async_copy(src.at[layer], dst, sem.at[k])
               for k, (src, dst) in enumerate(((wg_hbm, wg_ref), (wu_hbm, wu_ref),
                                               (wd_hbm, wd_ref)))]

    @pl.when(first)
    def _():
        for fetch in fetches:
            fetch.start()

    mix_ref[:, :cw] = ya_ref[0]

    lses = [l0_ref[0], l1_ref[0], l2_ref[0]]
    mx = jnp.maximum(jnp.maximum(lses[0], lses[1]), lses[2])
    es = [jnp.exp(ls - mx) for ls in lses]
    den = es[0] + es[1] + es[2]
    for g, o_ref in enumerate((o0_ref, o1_ref, o2_ref)):
        yb = (es[g] / den) * o_ref[0]
        mix_ref[:, cw + g * LANES:cw + (g + 1) * LANES] = yb.astype(_BF16)
    mix_ref[:, cw + 3 * LANES:] = yc_ref[0]

    out_ref[0] = x_ref[0] + mod[2:3] * jnp.dot(mix_ref[...], wo_ref[...],
                                               preferred_element_type=_F32)

    h = _modulated_norm(out_ref[0], g2_ref[...], mod[3:4], mod[4:5]).astype(_BF16)

    @pl.when(first)
    def _():
        for fetch in fetches:
            fetch.wait()

    for c0, c1 in ff_chunks:
        gate = jnp.dot(h, wg_ref[:, c0:c1], preferred_element_type=_F32)
        up = jnp.dot(h, wu_ref[:, c0:c1], preferred_element_type=_F32)
        act = ((gate * (1.0 / (1.0 + jnp.exp(-gate)))) * up).astype(_BF16)
        out_ref[0] += mod[5:6] * jnp.dot(act, wd_ref[c0:c1, :], preferred_element_type=_F32)
    if final:
        y = out_ref[0]
        ms = jnp.mean(y * y, axis=-1, keepdims=True)
        out_ref[0] = (y * lax.rsqrt(ms + EPS)) * fg_ref[...]


def _ff_chunks(dff, n_chunks=4):
    mxu_k = 2 * LANES
    tiles = -(-dff // mxu_k)
    cuts = sorted({min(dff, -(-tiles * c // n_chunks) * mxu_k) for c in range(n_chunks + 1)})
    return tuple(zip(cuts[:-1], cuts[1:]))


def _token_mixers_out(x, mod, l, ya, outs, lses, yc, w_out_p,
                      g2, wg, wu, wd, final_g, final, tm):
    b, s, d = x.shape
    dff = wg.shape[-1]
    cw = 2 * LANES
    tok = lambda width: pl.BlockSpec((1, tm, width), lambda bi, i: (bi, i, 0))
    const = lambda shape: pl.BlockSpec((None,) + shape, lambda bi, i: (l,) + (0,) * len(shape),
                                       pipeline_mode=pl.Buffered(1))
    return pl.pallas_call(
        functools.partial(_token_kernel, final=final, ff_chunks=_ff_chunks(dff), layer=l),
        out_shape=jax.ShapeDtypeStruct((b, s, d), _F32),
        grid=(b, s // tm),
        in_specs=[
            tok(d),
            pl.BlockSpec((1, 1, N_MOD, d), lambda bi, i: (l, bi, 0, 0)),
            tok(cw),
            tok(LANES), tok(LANES), tok(LANES),
            tok(LANES), tok(LANES), tok(LANES),
            tok(WIN_GROUP * LANES),
            const((d, d)),
            const((1, d)),
            pl.BlockSpec(memory_space=pl.ANY), pl.BlockSpec(memory_space=pl.ANY),
            pl.BlockSpec(memory_space=pl.ANY),
            pl.BlockSpec((1, d), lambda bi, i: (0, 0)),
        ],
        out_specs=tok(d),
        scratch_shapes=[pltpu.VMEM((tm, d), _BF16),
                        pltpu.VMEM((d, dff), _BF16), pltpu.VMEM((d, dff), _BF16),
                        pltpu.VMEM((dff, d), _BF16), pltpu.SemaphoreType.DMA((3,))],
        compiler_params=_cparams(("arbitrary", "arbitrary")),
        name="token_mix_ffn",
    )(x, mod, ya, *outs, *lses, yc, w_out_p, g2, wg, wu, wd, final_g)


def _in_proj_columns(d_model):
    pool_w = d_model // 4
    dil_w = 2 * len(DIL_PAIRS) * HEAD_DIM
    o_qb = pool_w
    o_kb = o_qb + dil_w
    o_vb = o_kb + dil_w
    o_qc = o_vb + dil_w
    o_kc = o_qc + 2 * WIN_GROUP * HEAD_DIM
    o_vc = o_kc + 2 * HEAD_DIM
    cols = list(range(pool_w))
    for g in range(len(DIL_PAIRS)):
        for base in (o_qb, o_kb, o_vb):
            cols += range(base + g * LANES, base + (g + 1) * LANES)
    win_heads = []
    for t in range(WIN_GROUP):
        for j in range(2):
            win_heads.append(j * WIN_GROUP + t)
            cols += range(o_qc + (j * WIN_GROUP + t) * HEAD_DIM,
                          o_qc + (j * WIN_GROUP + t + 1) * HEAD_DIM)
    cols += range(o_kc, o_kc + LANES)
    cols += range(o_vc, o_vc + LANES)
    qscale = np.ones((len(cols),), np.float32)
    for g in range(len(DIL_PAIRS)):
        qscale[pool_w + 3 * g * LANES:pool_w + (3 * g + 1) * LANES] = HEAD_DIM ** -0.5 * LOG2E
    q0 = pool_w + 3 * len(DIL_PAIRS) * LANES
    qscale[q0:q0 + WIN_GROUP * LANES] = HEAD_DIM ** -0.5 * LOG2E
    return np.asarray(cols), qscale, win_heads


def _runs(idx, key=None):
    idx = np.asarray(idx)
    key = np.zeros(len(idx)) if key is None else np.asarray(key)
    cuts = ([0] + [i for i in range(1, len(idx))
                   if idx[i] != idx[i - 1] + 1 or key[i] != key[i - 1]] + [len(idx)])
    return [(int(idx[a]), b - a, a) for a, b in zip(cuts[:-1], cuts[1:])]


def _take_runs(w, idx, axis):
    parts = [lax.slice_in_dim(w, src, src + n, axis=axis) for src, n, _ in _runs(idx)]
    return jnp.concatenate(parts, axis=axis)


def _permute_cols_kernel(w_ref, o_ref, *, runs):
    for src, n, dst, scale in runs:
        o_ref[0, :, dst:dst + n] = (w_ref[0, :, src:src + n] * scale).astype(o_ref.dtype)


def _permute_cols_bf16(w, cols, scale, tr):
    depth, rows, n = w.shape
    runs = [(src, length, dst, float(scale[dst])) for src, length, dst in _runs(cols, scale)]
    spec = pl.BlockSpec((1, tr, n), lambda l, i: (l, i, 0))
    return pl.pallas_call(
        functools.partial(_permute_cols_kernel, runs=tuple(runs)),
        out_shape=jax.ShapeDtypeStruct(w.shape, _BF16),
        grid=(depth, rows // tr),
        in_specs=[spec],
        out_specs=spec,
        compiler_params=_cparams(("arbitrary", "arbitrary")),
        name="permute_cols",
    )(w)


def kernel(x, c, norm1_g, norm2_g, w_ada, b_ada, w_in, w_pool, pool_scale, sink_logit,
           w_out, w_gate, w_up, w_down, final_g):
    b, s, d = x.shape
    depth = w_in.shape[0]
    slopes = _alibi_slopes(2 * WIN_GROUP + 2 * len(DIL_PAIRS))
    slopes_win = tuple(float(v) for v in slopes[:2 * WIN_GROUP])
    slopes_dil = slopes[2 * WIN_GROUP:]
    cols, qscale, win_heads = _in_proj_columns(d)
    pool_w = d // 4
    dil_w = 2 * len(DIL_PAIRS) * HEAD_DIM
    mix_rows = np.concatenate([
        np.arange(pool_w + dil_w),
        np.concatenate([pool_w + dil_w + h * HEAD_DIM + np.arange(HEAD_DIM) for h in win_heads]),
    ])
    n_grp = len(POOL_WINDOWS)
    pg = pool_w // n_grp

    mod = _modulation(c, w_ada, b_ada).reshape(depth, b, N_MOD, d)
    w_in_p = _permute_cols_bf16(w_in, cols, qscale, WEIGHT_ROWS)
    w_out_p = _take_runs(w_out, mix_rows, 1).astype(_BF16)
    zpad = lambda n: jnp.zeros((depth, pg, n * pg), _F32)
    wpool_bd = jnp.concatenate(
        [jnp.concatenate([zpad(g), w_pool[:, g], zpad(n_grp - 1 - g)], axis=2)
         for g in range(n_grp)], axis=1).astype(_BF16)
    g1, g2 = norm1_g.reshape(depth, 1, d), norm2_g.reshape(depth, 1, d)
    ps = pool_scale.reshape(depth, 1, pool_w)
    tm = min(TOKEN_TILE, s)
    for l in range(depth):
        (ya, q0, k0, v0, q1, k1, v1, q2, k2, v2, qc, kc, vc), cast = _in_proj(
            x, mod, l, g1, w_in_p, wpool_bd, ps, tm,
            to_cast=(w_gate, w_up, w_down) if l == 0 else ())
        if l == 0:
            wg, wu, wd = cast
        outs, lses = [], []
        for g, ((_, dil), (q, k, v)) in enumerate(zip(
                DIL_PAIRS, ((q0, k0, v0), (q1, k1, v1), (q2, k2, v2)))):
            sl = tuple(float(v_) for v_ in slopes_dil[2 * g:2 * g + 2])
            o, ls = _band_attn(q.reshape(b, s, LANES), k.reshape(b, s, LANES),
                               v.reshape(b, s, LANES), dil, sl)
            outs.append(o)
            lses.append(ls)
        yc = _win_attn(sink_logit, l, qc, kc, vc, slopes_win, s)
        x = _token_mixers_out(x, mod, l, ya, outs, lses, yc, w_out_p, g2, wg, wu, wd,
                              final_g.reshape(1, d), l == depth - 1, tm)
    return x
```

```python
import functools
import math

import jax
import jax.numpy as jnp
import numpy as np
from jax import lax
from jax.experimental import pallas as pl
from jax.experimental.pallas import tpu as pltpu

LANES = 128
HEAD_DIM = 64
HALF = LANES // 2
POOL_WINDOWS = (2, 4, 8, 16)
POOL_HALO = 8
REGROUP_STRIDE = 4
POOL_PAD = 16
DIL_PAIRS = ((128, 1), (512, 4), (2048, 16))
DIL_RADIUS = 64
WIN_RADIUS = 128
WIN_GROUP = 3
N_MOD = 6
EPS = 1e-6
NEG = -1e30
QBLK = 128
ONES_ROWS = 16
LOG2E = math.log2(math.e)
BAND_UNROLL = 6
WIN_UNROLL = 3
TOKEN_TILE = 1024
WEIGHT_ROWS = 256
VMEM_LIMIT = 56 * 1024 * 1024

_F32 = jnp.float32
_BF16 = jnp.bfloat16


def _alibi_slopes(n):
    i = np.arange(1, n + 1, dtype=np.float32)
    return np.exp2(np.float32(-8.0) * i / np.float32(n)).astype(np.float32)


def _cparams(sem):
    return pltpu.CompilerParams(dimension_semantics=sem, vmem_limit_bytes=VMEM_LIMIT)


def _mod_kernel(c_ref, w_ref, b_ref, o_ref):
    c = c_ref[...]
    act = (c * (1.0 / (1.0 + jnp.exp(-c)))).astype(_BF16)
    w = w_ref[0].astype(_BF16)
    o_ref[0] = jnp.dot(act, w, preferred_element_type=_F32) + b_ref[0]


def _modulation(c, w_ada, b_ada):
    depth, d, nd = w_ada.shape
    b = c.shape[0]
    tn = nd // 2
    return pl.pallas_call(
        _mod_kernel,
        out_shape=jax.ShapeDtypeStruct((depth, b, nd), _F32),
        grid=(depth, nd // tn),
        in_specs=[
            pl.BlockSpec((b, d), lambda l, j: (0, 0)),
            pl.BlockSpec((1, d, tn), lambda l, j: (l, 0, j)),
            pl.BlockSpec((1, 1, tn), lambda l, j: (l, 0, j)),
        ],
        out_specs=pl.BlockSpec((1, b, tn), lambda l, j: (l, 0, j)),
        compiler_params=_cparams(("arbitrary", "arbitrary")),
        name="modulation",
    )(c, w_ada, b_ada.reshape(depth, 1, nd))


def _modulated_norm(x, g, shift, scale):
    ms = jnp.mean(x * x, axis=-1, keepdims=True)
    return (x * lax.rsqrt(ms + EPS)) * (g * (1.0 + scale)) + shift


def _pooled_tokens(ubuf_ref, t0, s_len):
    tm = ubuf_ref.shape[0] - 2 * POOL_HALO - POOL_PAD
    cols = []
    for c in range(2):
        r_lo, r_hi = POOL_WINDOWS[2 * c] // 2, POOL_WINDOWS[2 * c + 1] // 2
        lanes = slice(c * LANES, (c + 1) * LANES)
        rows_from = lambda k0, cnt: ubuf_ref[k0:k0 + cnt, lanes]
        lane = lax.broadcasted_iota(jnp.int32, (1, LANES), 1)
        radius = jnp.where(lane < HALF, r_lo, r_hi)
        u = rows_from(POOL_HALO, tm)
        if r_lo < 2:
            wsum = u
            for k in range(1, r_hi + 1):
                pair = rows_from(POOL_HALO - k, tm) + rows_from(POOL_HALO + k, tm)
                wsum = wsum + (pair if k <= r_lo else jnp.where(lane < HALF, 0.0, pair))
        else:
            rows = tm + 3 * POOL_HALO
            level, span, wsums = rows_from(0, rows) + rows_from(1, rows), 2, {}
            while True:
                if span in (r_lo, r_hi):
                    wsums[span] = (level[POOL_HALO - span:POOL_HALO - span + tm]
                                   + level[POOL_HALO:POOL_HALO + tm]
                                   + rows_from(POOL_HALO + span, tm))
                if span >= r_hi:
                    break
                rows -= POOL_HALO
                level = level[:rows] + level[span:span + rows]
                span *= 2
            wsum = jnp.where(lane < HALF, wsums[r_lo], wsums[r_hi])
        t = t0 + lax.broadcasted_iota(jnp.int32, (tm, LANES), 0)
        cnt = jnp.minimum(t + radius + 1, s_len) - jnp.maximum(t - radius, 0)
        cols.append((wsum / cnt.astype(_F32) - u).astype(_BF16))
    return jnp.concatenate(cols, axis=1)


def _in_proj_kernel(*refs, s_len, n_cast):
    for src, dst in zip(refs[:n_cast], refs[len(refs) - 3 - n_cast:len(refs) - 3]):
        dst[...] = src[...].astype(dst.dtype)
    (x_ref, xp_ref, xn_ref, mod_ref, g_ref, w_ref, wp_ref, ps_ref,
     ya_ref, q0_ref, k0_ref, v0_ref, q1_ref, k1_ref, v1_ref,
     q2_ref, k2_ref, v2_ref, qc_ref, kc_ref, vct_ref) = refs[n_cast:len(refs) - 3 - n_cast]
    zs_ref, zt_ref, ubuf_ref = refs[-3:]
    tm = x_ref.shape[1]
    i = pl.program_id(1)
    cw = 2 * LANES
    mod = mod_ref[0, 0]
    norm = lambda rows: _modulated_norm(rows, g_ref[...], mod[0:1], mod[1:2]).astype(_BF16)
    h = norm(x_ref[0])

    def proj(lhs, t0, t1):
        return jnp.dot(lhs, w_ref[:, t0 * LANES:t1 * LANES], preferred_element_type=_F32)

    halo = proj(norm(jnp.concatenate([xp_ref[0], xn_ref[0]], axis=0)), 0, 2)
    ubuf_ref[0:POOL_HALO] = jnp.where(i > 0, halo[:POOL_HALO], 0.0)
    ubuf_ref[POOL_HALO:POOL_HALO + tm] = proj(h, 0, 2)
    ubuf_ref[POOL_HALO + tm:2 * POOL_HALO + tm] = jnp.where(i < pl.num_programs(1) - 1,
                                                             halo[POOL_HALO:], 0.0)
    ubuf_ref[2 * POOL_HALO + tm:] = jnp.zeros((POOL_PAD, cw), _F32)
    pooled = _pooled_tokens(ubuf_ref, i * tm, s_len)

    z_lo, z_hi = proj(h, 2, 8), proj(h, 8, 16)

    def tile(i):
        z, j = (z_lo, i - 2) if i < 8 else (z_hi, i - 8)
        return z[:, j * LANES:(j + 1) * LANES]

    for j, ref in enumerate((q0_ref, k0_ref, v0_ref)):
        ref[0] = tile(2 + j).astype(_BF16)
    for group_refs, (_, dil), base in (((q1_ref, k1_ref, v1_ref), DIL_PAIRS[1], 5),
                                       ((q2_ref, k2_ref, v2_ref), DIL_PAIRS[2], 8)):
        inner = min(dil, REGROUP_STRIDE)
        outer = dil // inner
        for j, ref in enumerate(group_refs):
            zs_ref[base - 5 + j] = tile(base + j)
            for ra in range(inner):
                part = zs_ref[base - 5 + j, pl.ds(ra, tm // inner, stride=inner), :]
                if outer == 1:
                    ref[0, ra] = part.astype(_BF16)
                else:
                    zt_ref[ra] = part
            for rb in range(outer if outer > 1 else 0):
                for ra in range(inner):
                    ref[0, ra + inner * rb] = zt_ref[
                        ra, pl.ds(rb, tm // dil, stride=outer), :].astype(_BF16)
    for t in range(WIN_GROUP):
        qc_ref[0, t] = tile(11 + t).astype(_BF16)
    kc_ref[0] = tile(14).astype(_BF16)
    vct_ref[0] = tile(15).T.astype(_BF16)
    ya = jnp.dot(pooled, wp_ref[...], preferred_element_type=_F32) * ps_ref[...]
    ya_ref[0] = ya.astype(_BF16)


def _in_proj(x, mod, l, g, w_in_p, wpool_bd, pool_scale, tm, to_cast=()):
    b, s, d = x.shape
    n_in = w_in_p.shape[-1]
    cw = 2 * LANES
    hb = tm // POOL_HALO
    nhb = s // POOL_HALO
    d1, d2 = DIL_PAIRS[1][1], DIL_PAIRS[2][1]
    nat = lambda: pl.BlockSpec((1, tm, LANES), lambda bi, i: (bi, i, 0))
    res = lambda dil: pl.BlockSpec((1, dil, tm // dil, LANES), lambda bi, i: (bi, 0, i, 0))
    full = lambda shape: pl.BlockSpec((None,) + shape, lambda bi, i: (l,) + (0,) * len(shape))
    sds = jax.ShapeDtypeStruct
    out_shape = (
        [sds((b, s, cw), _BF16)]
        + [sds((b, s, LANES), _BF16)] * 3
        + [sds((b, d1, s // d1, LANES), _BF16)] * 3
        + [sds((b, d2, s // d2, LANES), _BF16)] * 3
        + [sds((b, WIN_GROUP, s, LANES), _BF16)]
        + [sds((b, s, LANES), _BF16), sds((b, LANES, s), _BF16)]
    )
    out_specs = (
        [pl.BlockSpec((1, tm, 2 * LANES), lambda bi, i: (bi, i, 0))]
        + [nat() for _ in range(3)]
        + [res(d1) for _ in range(3)]
        + [res(d2) for _ in range(3)]
        + [pl.BlockSpec((1, WIN_GROUP, tm, LANES), lambda bi, i: (bi, 0, i, 0))]
        + [nat(), pl.BlockSpec((1, LANES, tm), lambda bi, i: (bi, 0, i))]
    )
    nt = s // tm
    cast_in, cast_specs = [], []
    for w in to_cast:
        w2 = w.reshape(-1, w.shape[-1])
        slab = w2.shape[0] // (b * nt)
        assert slab * b * nt == w2.shape[0] and slab % 16 == 0
        cast_in.append(w2)
        cast_specs.append(pl.BlockSpec((slab, w2.shape[1]), lambda bi, i: (bi * nt + i, 0)))
    outs = pl.pallas_call(
        functools.partial(_in_proj_kernel, s_len=s, n_cast=len(to_cast)),
        out_shape=out_shape + [sds(w2.shape, _BF16) for w2 in cast_in],
        grid=(b, nt),
        in_specs=cast_specs + [
            pl.BlockSpec((1, tm, d), lambda bi, i: (bi, i, 0)),
            pl.BlockSpec((1, POOL_HALO, d), lambda bi, i: (bi, jnp.maximum(i * hb - 1, 0), 0)),
            pl.BlockSpec((1, POOL_HALO, d),
                         lambda bi, i: (bi, jnp.minimum((i + 1) * hb, nhb - 1), 0)),
            pl.BlockSpec((1, 1, N_MOD, d), lambda bi, i: (l, bi, 0, 0)),
            full((1, d)),
            full((d, n_in)),
            full((cw, cw)),
            full((1, cw)),
        ],
        out_specs=out_specs + cast_specs,
        scratch_shapes=[pltpu.VMEM((6, tm, LANES), _F32),
                        pltpu.VMEM((REGROUP_STRIDE, tm // REGROUP_STRIDE, LANES), _F32),
                        pltpu.VMEM((tm + 2 * POOL_HALO + POOL_PAD, cw), _F32)],
        compiler_params=_cparams(("arbitrary", "arbitrary")),
        name="in_proj",
    )(*cast_in, x, x, x, mod, g, w_in_p, wpool_bd, pool_scale)
    n_main = len(out_shape)
    return outs[:n_main], [o.reshape(w.shape) for o, w in zip(outs[n_main:], to_cast)]


def _band_bias_t(shape, col_heads, slopes, dist_scale, radius, offset):
    row = lax.broadcasted_iota(jnp.int32, shape, 0)
    col = lax.broadcasted_iota(jnp.int32, shape, 1)
    dist = jnp.abs(row - offset - (col % QBLK))
    slope = jnp.full(shape, slopes[col_heads[0]] * dist_scale, _F32)
    for i, hd in enumerate(col_heads[1:], start=1):
        slope = jnp.where(col >= i * QBLK, np.float32(slopes[hd] * dist_scale), slope)
    return jnp.where(dist <= radius, (-slope * dist.astype(_F32)) * LOG2E, NEG)


def _window(i, nblk, radius, width):
    seq = nblk * QBLK
    ws = jnp.clip(i * QBLK - radius, 0, seq - width)
    variant = jnp.where(i == 0, 0, jnp.where(i == nblk - 1, 2, 1))
    return ws, variant


def _trans_b_dot(a, b):
    return lax.dot_general(a, b, (((1,), (1,)), ((), ())), preferred_element_type=_F32)


def _band_attn_kernel(q_ref, k_ref, v_ref, o_ref, lse_ref, bias_ref, *scratch, dil, slopes):
    s_len = q_ref.shape[1]
    n = s_len // dil
    nblk = n // QBLK
    width = QBLK + 2 * DIL_RADIUS
    cols = 2 * QBLK

    @pl.when(pl.program_id(0) == 0)
    def _():
        for var, off in enumerate((0, DIL_RADIUS, width - QBLK)):
            bias_ref[var] = _band_bias_t((width, cols), (0, 1), slopes, float(dil),
                                         DIL_RADIUS, off)

    lo = lax.broadcasted_iota(jnp.int32, (QBLK, LANES), 1) < HALF
    s_bufs, e_bufs, m_bufs = scratch[0:2], scratch[2:4], scratch[4:6]
    ntot = s_len // QBLK

    def locate(blk):
        r = blk // nblk
        i = blk % nblk
        ws, variant = _window(i, nblk, DIL_RADIUS, width)
        return r, i, pl.multiple_of(r * n + ws, DIL_RADIUS), variant

    def scores(blk, par):
        _, _, kstart, variant = locate(blk)
        qstart = pl.multiple_of(blk * QBLK, QBLK)
        q = q_ref[0, pl.ds(qstart, QBLK), :]
        kw = k_ref[0, pl.ds(kstart, width), :]
        zero = jnp.zeros_like(q)
        qs = jnp.concatenate([jnp.where(lo, q, zero), jnp.where(lo, zero, q)], axis=0)
        s_bufs[par][...] = _trans_b_dot(kw, qs) + bias_ref[variant]

    def numerators(par):
        sc = s_bufs[par][...]
        m = jnp.max(sc, axis=0, keepdims=True)
        e_bufs[par][...] = jnp.exp2(sc - m).astype(_BF16)
        m_bufs[par][...] = jnp.broadcast_to(m, (8, cols))

    def outputs(blk, par):
        r, i, kstart, _ = locate(blk)
        vt = v_ref[0, pl.ds(kstart, width), :].T
        lhs = jnp.concatenate([vt, jnp.ones((ONES_ROWS, width), _BF16)], axis=0)
        o = jnp.dot(lhs, e_bufs[par][...], preferred_element_type=_F32)
        den = o[LANES:LANES + 1]
        lse_row = m_bufs[par][0:1] * (1.0 / LOG2E) + jnp.log(den)
        res = jnp.concatenate(
            [o[h * HEAD_DIM:(h + 1) * HEAD_DIM, h * QBLK:(h + 1) * QBLK]
             / den[:, h * QBLK:(h + 1) * QBLK] for h in range(2)], axis=0)
        lse2 = jnp.concatenate(
            [jnp.broadcast_to(lse_row[:, h * QBLK:(h + 1) * QBLK], (HEAD_DIM, QBLK))
             for h in range(2)], axis=0)
        out, lse = res.T, lse2.T
        if dil == 1:
            qstart = pl.multiple_of(blk * QBLK, QBLK)
            o_ref[0, pl.ds(qstart, QBLK), :] = out.astype(_BF16)
            lse_ref[0, pl.ds(qstart, QBLK), :] = lse
        else:
            tstart = i * (QBLK * dil) + r
            scratch[6][pl.ds(tstart, QBLK, stride=dil), :] = out
            lse_ref[0, pl.ds(tstart, QBLK, stride=dil), :] = lse

    scores(0, 0)
    scores(1, 1)
    numerators(0)

    def body(p, carry):
        it = 2 * p
        scores(it, 0)
        numerators(1)
        outputs(it - 2, 0)
        scores(it + 1, 1)
        numerators(0)
        outputs(it - 1, 1)
        return carry

    lax.fori_loop(1, ntot // 2, body, 0, unroll=BAND_UNROLL)
    numerators(1)
    outputs(ntot - 2, 0)
    outputs(ntot - 1, 1)
    if dil != 1:
        o_ref[0] = scratch[6][...].astype(_BF16)


def _band_attn(q, k, v, dil, slopes):
    b, s, _ = q.shape
    width = QBLK + 2 * DIL_RADIUS
    spec = lambda: pl.BlockSpec((1, s, LANES), lambda bi: (bi, 0, 0))
    cols = 2 * QBLK
    scratch = ([pltpu.VMEM((3, width, cols), _F32)]
               + [pltpu.VMEM((width, cols), _F32)] * 2
               + [pltpu.VMEM((width, cols), _BF16)] * 2
               + [pltpu.VMEM((8, cols), _F32)] * 2)
    if dil != 1:
        scratch.append(pltpu.VMEM((s, LANES), _F32))
    return pl.pallas_call(
        functools.partial(_band_attn_kernel, dil=dil, slopes=slopes),
        out_shape=[jax.ShapeDtypeStruct((b, s, LANES), _BF16),
                   jax.ShapeDtypeStruct((b, s, LANES), _F32)],
        grid=(b,),
        in_specs=[spec(), spec(), spec()],
        out_specs=[spec(), spec()],
        scratch_shapes=scratch,
        compiler_params=_cparams(("arbitrary",)),
        name=f"band_attn_d{dil}",
    )(q, k, v)


def _win_attn_kernel(sink_ref, q_ref, k_ref, vt_ref, o_ref, bias_ref,
                     s0_ref, s1_ref, e0_ref, e1_ref, t0_ref, t1_ref, *, slopes, layer):
    s_len = k_ref.shape[1]
    tc = q_ref.shape[2]
    nblk = s_len // QBLK
    nloc = tc // QBLK
    width = QBLK + 2 * WIN_RADIUS
    cols = WIN_GROUP * QBLK
    s_bufs, e_bufs, t_bufs = (s0_ref, s1_ref), (e0_ref, e1_ref), (t0_ref, t1_ref)

    @pl.when((pl.program_id(0) == 0) & (pl.program_id(1) == 0))
    def _():
        for var, off in enumerate((0, WIN_RADIUS, width - QBLK)):
            for j in range(2):
                heads = tuple(j * WIN_GROUP + t for t in range(WIN_GROUP))
                bias_ref[var, j] = _band_bias_t((width, cols), heads, slopes, 1.0,
                                                WIN_RADIUS, off)

    qlane = lax.broadcasted_iota(jnp.int32, (1, cols), 1)
    klo = lax.broadcasted_iota(jnp.int32, (width, LANES), 1) < HALF
    base = pl.program_id(1) * nloc

    def sink_row(j):
        row = jnp.full((1, cols), sink_ref[layer, j * WIN_GROUP] * LOG2E, _F32)
        for t in range(1, WIN_GROUP):
            row = jnp.where(qlane >= t * QBLK, sink_ref[layer, j * WIN_GROUP + t] * LOG2E, row)
        return row

    def kstart_of(bl):
        ws, variant = _window(base + bl, nblk, WIN_RADIUS, width)
        return pl.multiple_of(ws, QBLK), variant

    def scores(bl, par):
        kstart, variant = kstart_of(bl)
        qstart = pl.multiple_of(bl * QBLK, QBLK)
        q = q_ref[0, :, pl.ds(qstart, QBLK), :].reshape(cols, LANES)
        kw = k_ref[0, pl.ds(kstart, width), :]
        zero = jnp.zeros_like(kw)
        for j in range(2):
            kh = jnp.where(klo, kw, zero) if j == 0 else jnp.where(klo, zero, kw)
            s_bufs[par][j] = _trans_b_dot(kh, q) + bias_ref[variant, j]

    def numerators(par):
        for j in range(2):
            sc = s_bufs[par][j]
            sink = sink_row(j)
            m = jnp.maximum(jnp.max(sc, axis=0, keepdims=True), sink)
            e_bufs[par][j] = jnp.exp2(sc - m).astype(_BF16)
            t_bufs[par][j] = jnp.broadcast_to(jnp.exp2(sink - m), (8, cols))

    def outputs(bl, par):
        kstart, _ = kstart_of(bl)
        qstart = pl.multiple_of(bl * QBLK, QBLK)
        vt = vt_ref[0, :, pl.ds(kstart, width)]
        ones = jnp.ones((ONES_ROWS, width), _BF16)
        res = []
        for j in range(2):
            lhs = jnp.concatenate([vt[j * HEAD_DIM:(j + 1) * HEAD_DIM], ones], axis=0)
            o = jnp.dot(lhs, e_bufs[par][j], preferred_element_type=_F32)
            den = o[HEAD_DIM:HEAD_DIM + 1] + t_bufs[par][j][0:1]
            res.append(o[:HEAD_DIM] / den)
        for t in range(WIN_GROUP):
            both = jnp.concatenate([r[:, t * QBLK:(t + 1) * QBLK] for r in res], axis=0)
            o_ref[0, pl.ds(qstart, QBLK), t * LANES:(t + 1) * LANES] = both.T.astype(_BF16)

    scores(0, 0)
    scores(1, 1)
    numerators(0)

    def body(p, carry):
        it = 2 * p
        scores(it, 0)
        numerators(1)
        outputs(it - 2, 0)
        scores(it + 1, 1)
        numerators(0)
        outputs(it - 1, 1)
        return carry

    lax.fori_loop(1, nloc // 2, body, 0, unroll=WIN_UNROLL)
    numerators(1)
    outputs(nloc - 2, 0)
    outputs(nloc - 1, 1)


def _win_attn(sink, layer, q, k, vt, slopes, tc):
    b, _, s, _ = q.shape
    width = QBLK + 2 * WIN_RADIUS
    cols = WIN_GROUP * QBLK
    assert (tc // QBLK) % 2 == 0 and tc // QBLK >= 4
    return pl.pallas_call(
        functools.partial(_win_attn_kernel, slopes=slopes, layer=layer),
        out_shape=jax.ShapeDtypeStruct((b, s, WIN_GROUP * LANES), _BF16),
        grid=(b, s // tc),
        in_specs=[
            pl.BlockSpec(memory_space=pltpu.SMEM),
            pl.BlockSpec((1, WIN_GROUP, tc, LANES), lambda bi, i: (bi, 0, i, 0)),
            pl.BlockSpec((1, s, LANES), lambda bi, i: (bi, 0, 0)),
            pl.BlockSpec((1, LANES, s), lambda bi, i: (bi, 0, 0)),
        ],
        out_specs=pl.BlockSpec((1, tc, WIN_GROUP * LANES), lambda bi, i: (bi, i, 0)),
        scratch_shapes=[pltpu.VMEM((3, 2, width, cols), _F32)]
                       + [pltpu.VMEM((2, width, cols), _F32)] * 2
                       + [pltpu.VMEM((2, width, cols), _BF16)] * 2
                       + [pltpu.VMEM((2, 8, cols), _F32)] * 2,
        compiler_params=_cparams(("arbitrary", "arbitrary")),
        name="win_attn",
    )(sink, q, k, vt)


def _token_kernel(x_ref, mod_ref, ya_ref,
                  o0_ref, o1_ref, o2_ref, l0_ref, l1_ref, l2_ref, yc_ref, wo_ref,
                  g2_ref, wg_ref, wu_ref, wd_ref, fg_ref,
                  out_ref, mix_ref, *, final, ff_chunks):
    cw = 2 * LANES
    mod = mod_ref[0, 0]
    mix_ref[:, :cw] = ya_ref[0]

    lses = [l0_ref[0], l1_ref[0], l2_ref[0]]
    mx = jnp.maximum(jnp.maximum(lses[0], lses[1]), lses[2])
    es = [jnp.exp(ls - mx) for ls in lses]
    den = es[0] + es[1] + es[2]
    for g, o_ref in enumerate((o0_ref, o1_ref, o2_ref)):
        yb = (es[g] / den) * o_ref[0]
        mix_ref[:, cw + g * LANES:cw + (g + 1) * LANES] = yb.astype(_BF16)
    mix_ref[:, cw + 3 * LANES:] = yc_ref[0]

    out_ref[0] = x_ref[0] + mod[2:3] * jnp.dot(mix_ref[...], wo_ref[...],
                                               preferred_element_type=_F32)

    h = _modulated_norm(out_ref[0], g2_ref[...], mod[3:4], mod[4:5]).astype(_BF16)
    for c0, c1 in ff_chunks:
        gate = jnp.dot(h, wg_ref[:, c0:c1], preferred_element_type=_F32)
        up = jnp.dot(h, wu_ref[:, c0:c1], preferred_element_type=_F32)
        act = ((gate * (1.0 / (1.0 + jnp.exp(-gate)))) * up).astype(_BF16)
        out_ref[0] += mod[5:6] * jnp.dot(act, wd_ref[c0:c1, :], preferred_element_type=_F32)
    if final:
        y = out_ref[0]
        ms = jnp.mean(y * y, axis=-1, keepdims=True)
        out_ref[0] = (y * lax.rsqrt(ms + EPS)) * fg_ref[...]


def _ff_chunks(dff, n_chunks=4):
    mxu_k = 2 * LANES
    tiles = -(-dff // mxu_k)
    cuts = sorted({min(dff, -(-tiles * c // n_chunks) * mxu_k) for c in range(n_chunks + 1)})
    return tuple(zip(cuts[:-1], cuts[1:]))


def _token_mixers_out(x, mod, l, ya, outs, lses, yc, w_out_p,
                      g2, wg, wu, wd, final_g, final, tm):
    b, s, d = x.shape
    dff = wg.shape[-1]
    cw = 2 * LANES
    tok = lambda width: pl.BlockSpec((1, tm, width), lambda bi, i: (bi, i, 0))
    const = lambda shape: pl.BlockSpec((None,) + shape, lambda bi, i: (l,) + (0,) * len(shape),
                                       pipeline_mode=pl.Buffered(1))
    return pl.pallas_call(
        functools.partial(_token_kernel, final=final, ff_chunks=_ff_chunks(dff)),
        out_shape=jax.ShapeDtypeStruct((b, s, d), _F32),
        grid=(b, s // tm),
        in_specs=[
            tok(d),
            pl.BlockSpec((1, 1, N_MOD, d), lambda bi, i: (l, bi, 0, 0)),
            tok(cw),
            tok(LANES), tok(LANES), tok(LANES),
            tok(LANES), tok(LANES), tok(LANES),
            tok(WIN_GROUP * LANES),
            const((d, d)),
            const((1, d)),
            const((d, dff)), const((d, dff)), const((dff, d)),
            pl.BlockSpec((1, d), lambda bi, i: (0, 0)),
        ],
        out_specs=tok(d),
        scratch_shapes=[pltpu.VMEM((tm, d), _BF16)],
        compiler_params=_cparams(("arbitrary", "arbitrary")),
        name="token_mix_ffn",
    )(x, mod, ya, *outs, *lses, yc, w_out_p, g2, wg, wu, wd, final_g)


def _in_proj_columns(d_model):
    pool_w = d_model // 4
    dil_w = 2 * len(DIL_PAIRS) * HEAD_DIM
    o_qb = pool_w
    o_kb = o_qb + dil_w
    o_vb = o_kb + dil_w
    o_qc = o_vb + dil_w
    o_kc = o_qc + 2 * WIN_GROUP * HEAD_DIM
    o_vc = o_kc + 2 * HEAD_DIM
    cols = list(range(pool_w))
    for g in range(len(DIL_PAIRS)):
        for base in (o_qb, o_kb, o_vb):
            cols += range(base + g * LANES, base + (g + 1) * LANES)
    win_heads = []
    for t in range(WIN_GROUP):
        for j in range(2):
            win_heads.append(j * WIN_GROUP + t)
            cols += range(o_qc + (j * WIN_GROUP + t) * HEAD_DIM,
                          o_qc + (j * WIN_GROUP + t + 1) * HEAD_DIM)
    cols += range(o_kc, o_kc + LANES)
    cols += range(o_vc, o_vc + LANES)
    qscale = np.ones((len(cols),), np.float32)
    for g in range(len(DIL_PAIRS)):
        qscale[pool_w + 3 * g * LANES:pool_w + (3 * g + 1) * LANES] = HEAD_DIM ** -0.5 * LOG2E
    q0 = pool_w + 3 * len(DIL_PAIRS) * LANES
    qscale[q0:q0 + WIN_GROUP * LANES] = HEAD_DIM ** -0.5 * LOG2E
    return np.asarray(cols), qscale, win_heads


def _runs(idx, key=None):
    idx = np.asarray(idx)
    key = np.zeros(len(idx)) if key is None else np.asarray(key)
    cuts = ([0] + [i for i in range(1, len(idx))
                   if idx[i] != idx[i - 1] + 1 or key[i] != key[i - 1]] + [len(idx)])
    return [(int(idx[a]), b - a, a) for a, b in zip(cuts[:-1], cuts[1:])]


def _take_runs(w, idx, axis):
    parts = [lax.slice_in_dim(w, src, src + n, axis=axis) for src, n, _ in _runs(idx)]
    return jnp.concatenate(parts, axis=axis)


def _permute_cols_kernel(w_ref, o_ref, *, runs):
    for src, n, dst, scale in runs:
        o_ref[0, :, dst:dst + n] = (w_ref[0, :, src:src + n] * scale).astype(o_ref.dtype)


def _permute_cols_bf16(w, cols, scale, tr):
    depth, rows, n = w.shape
    runs = [(src, length, dst, float(scale[dst])) for src, length, dst in _runs(cols, scale)]
    spec = pl.BlockSpec((1, tr, n), lambda l, i: (l, i, 0))
    return pl.pallas_call(
        functools.partial(_permute_cols_kernel, runs=tuple(runs)),
        out_shape=jax.ShapeDtypeStruct(w.shape, _BF16),
        grid=(depth, rows // tr),
        in_specs=[spec],
        out_specs=spec,
        compiler_params=_cparams(("arbitrary", "arbitrary")),
        name="permute_cols",
    )(w)


def kernel(x, c, norm1_g, norm2_g, w_ada, b_ada, w_in, w_pool, pool_scale, sink_logit,
           w_out, w_gate, w_up, w_down, final_g):
    b, s, d = x.shape
    depth = w_in.shape[0]
    slopes = _alibi_slopes(2 * WIN_GROUP + 2 * len(DIL_PAIRS))
    slopes_win = tuple(float(v) for v in slopes[:2 * WIN_GROUP])
    slopes_dil = slopes[2 * WIN_GROUP:]
    cols, qscale, win_heads = _in_proj_columns(d)
    pool_w = d // 4
    dil_w = 2 * len(DIL_PAIRS) * HEAD_DIM
    mix_rows = np.concatenate([
        np.arange(pool_w + dil_w),
        np.concatenate([pool_w + dil_w + h * HEAD_DIM + np.arange(HEAD_DIM) for h in win_heads]),
    ])
    n_grp = len(POOL_WINDOWS)
    pg = pool_w // n_grp

    mod = _modulation(c, w_ada, b_ada).reshape(depth, b, N_MOD, d)
    w_in_p = _permute_cols_bf16(w_in, cols, qscale, WEIGHT_ROWS)
    zpad = lambda n: jnp.zeros((depth, pg, n * pg), _F32)
    wpool_bd = jnp.concatenate(
        [jnp.concatenate([zpad(g), w_pool[:, g], zpad(n_grp - 1 - g)], axis=2)
         for g in range(n_grp)], axis=1).astype(_BF16)
    g1, g2 = norm1_g.reshape(depth, 1, d), norm2_g.reshape(depth, 1, d)
    ps = pool_scale.reshape(depth, 1, pool_w)
    tm = min(TOKEN_TILE, s)
    for l in range(depth):
        (ya, q0, k0, v0, q1, k1, v1, q2, k2, v2, qc, kc, vc), cast = _in_proj(
            x, mod, l, g1, w_in_p, wpool_bd, ps, tm,
            to_cast=(w_gate, w_up, w_down, w_out) if l == 0 else ())
        if l == 0:
            wg, wu, wd, w_out_bf16 = cast
            w_out_p = _take_runs(w_out_bf16, mix_rows, 1)
        outs, lses = [], []
        for g, ((_, dil), (q, k, v)) in enumerate(zip(
                DIL_PAIRS, ((q0, k0, v0), (q1, k1, v1), (q2, k2, v2)))):
            sl = tuple(float(v_) for v_ in slopes_dil[2 * g:2 * g + 2])
            o, ls = _band_attn(q.reshape(b, s, LANES), k.reshape(b, s, LANES),
                               v.reshape(b, s, LANES), dil, sl)
            outs.append(o)
            lses.append(ls)
        yc = _win_attn(sink_logit, l, qc, kc, vc, slopes_win, s)
        x = _token_mixers_out(x, mod, l, ya, outs, lses, yc, w_out_p, g2, wg, wu, wd,
                              final_g.reshape(1, d), l == depth - 1, tm)
    return x
```
